```python
import math
import jax, jax.numpy as jnp
from jax import lax
import numpy as np

D_MODEL = 2048
BATCH = 8
SEQ = 2048
DEPTH = 4

GRID_W = 64
CTX_LEN = 256
N_MIXERS = 4
CHUNK = 64
CONV_W = 3
D_FF = 4 * D_MODEL
NORM_EPS = 1e-6

SSD_D_INNER = 2 * D_MODEL
SSD_HEAD_DIM = 64
SSD_HEADS = SSD_D_INNER // SSD_HEAD_DIM
SSD_GROUPS = 8
SSD_STATE = 128
SSD_CONV_CH = SSD_D_INNER + 2 * SSD_GROUPS * SSD_STATE
SSD_IN = SSD_D_INNER + SSD_CONV_CH + 2 * SSD_HEADS

RET_HEADS = 8
RET_QK_DIM = D_MODEL // RET_HEADS
RET_V_DIM = 2 * RET_QK_DIM
RET_DV = RET_HEADS * RET_V_DIM
RET_IN = 2 * D_MODEL + 2 * RET_DV
ROPE_BASE = 10000.0

HGRN_EXPAND = 128
HGRN_HEADS = D_MODEL // HGRN_EXPAND
HGRN_IN = 5 * D_MODEL

GDN_HEAD_DIM = 128
GDN_K_HEADS = D_MODEL // GDN_HEAD_DIM
GDN_V_HEADS = 2 * GDN_K_HEADS
GDN_DK = GDN_K_HEADS * GDN_HEAD_DIM
GDN_DV = GDN_V_HEADS * GDN_HEAD_DIM
GDN_CONV_CH = 2 * GDN_DK + GDN_DV
GDN_IN = GDN_CONV_CH + GDN_DV + 4 * GDN_V_HEADS

kernel_name = 'hybrid_bidir_recurrent_dit_block'


def _rmsnorm(x, g, eps=NORM_EPS):
    x32 = x.astype(jnp.float32)
    y = x32 * lax.rsqrt(jnp.mean(x32 * x32, axis=-1, keepdims=True) + eps)
    return y.astype(x.dtype) * g


def _adaln(x, g, shift, scale):
    return _rmsnorm(x, g) * (1 + scale) + shift


def _l2norm(x, eps=1e-6):
    x32 = x.astype(jnp.float32)
    return x32 * lax.rsqrt(jnp.sum(x32 * x32, axis=-1, keepdims=True) + eps)


def _sq_relu_mlp(h, w1, w2):
    return jnp.square(jax.nn.relu(h @ w1)) @ w2


def _dwconv(u, w):
    return lax.conv_general_dilated(u, w[:, None, :], window_strides=(1,),
                                    padding=[(CONV_W // 2, CONV_W // 2)],
                                    dimension_numbers=('NWC', 'WIO', 'NWC'),
                                    feature_group_count=u.shape[-1])


def _conv_split(u, w, lc):
    return jnp.concatenate([_dwconv(u[:, :lc], w), _dwconv(u[:, lc:], w)], axis=1)


def _rev(t, lc):
    return jnp.concatenate([jnp.flip(t[:, :lc], 1), jnp.flip(t[:, lc:], 1)], axis=1)


def _split_out(out, lc, keep_ctx):
    if keep_ctx:
        return out[:, :lc], out[:, lc:]
    return None, out


def _to_chunks(t):
    b, n = t.shape[:2]
    return jnp.moveaxis(t.reshape((b, n // CHUNK, CHUNK) + t.shape[2:]), 1, 0)


def _from_chunks(t):
    nc, b, q = t.shape[:3]
    return jnp.moveaxis(t, 0, 1).reshape((b, nc * q) + t.shape[3:])


def _chunk_masks():
    idx = jnp.arange(CHUNK)
    return idx[:, None] >= idx[None, :], idx[:, None] > idx[None, :]


def _scalar_decay_scan(q, k, v, log_a):
    f32 = jnp.float32
    q, k, v, log_a = (t.astype(f32) for t in (q, k, v, log_a))
    bsz, _, g, n = q.shape
    r, p = v.shape[-2:]
    incl, _ = _chunk_masks()

    def body(s, xs):
        qc, kc, vc, la = xs
        cum = jnp.cumsum(la, axis=1)
        cum_t = jnp.moveaxis(cum, 1, -1)
        seg = cum_t[..., :, None] - cum_t[..., None, :]
        scores = jnp.einsum('btgn,bsgn->bgts', qc, kc)
        attn = scores[:, :, None] * jnp.exp(jnp.where(incl, seg, -jnp.inf))
        y = jnp.einsum('bgrts,bsgrp->btgrp', attn, vc)
        y = y + jnp.einsum('btgn,bgrnp->btgrp', qc, s) * jnp.exp(cum)[..., None]
        to_end = jnp.exp(cum[:, -1:] - cum)
        s = jnp.exp(cum[:, -1])[..., None, None] * s + jnp.einsum('bsgn,bsgr,bsgrp->bgrnp', kc, to_end, vc)
        return s, y

    s0 = jnp.zeros((bsz, g, r, n, p), f32)
    _, y = lax.scan(body, s0, tuple(_to_chunks(t) for t in (q, k, v, log_a)))
    return _from_chunks(y)


def _vector_decay_scan(q, k, v, log_f):
    f32 = jnp.float32
    q, k, v, log_f = (t.astype(f32) for t in (q, k, v, log_f))
    bsz, _, h, kd = q.shape
    vd = v.shape[-1]
    incl, _ = _chunk_masks()

    def body(s, xs):
        qc, kc, vc, lf = xs
        cum = jnp.cumsum(lf, axis=1)
        seg = cum[:, :, None] - cum[:, None, :]
        decay = jnp.exp(jnp.where(incl[:, :, None, None], seg, -jnp.inf))
        attn = jnp.einsum('bthk,bshk,btshk->bhts', qc, kc, decay)
        y = jnp.einsum('bhts,bshv->bthv', attn, vc)
        y = y + jnp.einsum('bthk,bhkv->bthv', qc * jnp.exp(cum), s)
        s = jnp.exp(cum[:, -1])[..., None] * s + jnp.einsum('bshk,bshv->bhkv', kc * jnp.exp(cum[:, -1:] - cum), vc)
        return s, y

    s0 = jnp.zeros((bsz, h, kd, vd), f32)
    _, y = lax.scan(body, s0, tuple(_to_chunks(t) for t in (q, k, v, log_f)))
    return _from_chunks(y)


def _delta_scan(q, k, v, beta, log_a):
    f32 = jnp.float32
    q, k, v, beta, log_a = (t.astype(f32) for t in (q, k, v, beta, log_a))
    bsz, _, g, kd = q.shape
    r, vd = v.shape[-2:]
    incl, strict = _chunk_masks()

    def body(s, xs):
        qc, kc, vc, bc, la = xs
        cum = jnp.cumsum(la, axis=1)
        cum_t = jnp.moveaxis(cum, 1, -1)
        seg = cum_t[..., :, None] - cum_t[..., None, :]
        beta_t = jnp.moveaxis(bc, 1, -1)
        kk = jnp.einsum('btgk,bsgk->bgts', kc, kc)
        lower = beta_t[..., :, None] * kk[:, :, None] * jnp.exp(jnp.where(strict, seg, -jnp.inf))
        rhs_v = jnp.moveaxis(vc * bc[..., None], 1, 3)
        rhs_k = jnp.moveaxis(kc[:, :, :, None, :] * (bc * jnp.exp(cum))[..., None], 1, 3)
        sol = lax.linalg.triangular_solve(lower, jnp.concatenate([rhs_v, rhs_k], axis=-1),
                                          left_side=True, lower=True, unit_diagonal=True)
        u, w = sol[..., :vd], sol[..., vd:]
        v_new = u - jnp.einsum('bgrtk,bgrkv->bgrtv', w, s)
        qk = jnp.einsum('btgk,bsgk->bgts', qc, kc)
        attn = qk[:, :, None] * jnp.exp(jnp.where(incl, seg, -jnp.inf))
        y = jnp.einsum('bgrts,bgrsv->btgrv', attn, v_new)
        y = y + jnp.einsum('btgk,bgrkv->btgrv', qc, s) * jnp.exp(cum)[..., None]
        to_end = jnp.exp(cum_t[..., -1:] - cum_t)
        s = jnp.exp(cum_t[..., -1])[..., None, None] * s + jnp.einsum('bsgk,bgrs,bgrsv->bgrkv', kc, to_end, v_new)
        return s, y

    s0 = jnp.zeros((bsz, g, r, kd, vd), f32)
    _, y = lax.scan(body, s0, tuple(_to_chunks(t) for t in (q, k, v, beta, log_a)))
    return _from_chunks(y)


def _rope_2d(t, rows):
    f32 = jnp.float32
    pos = jnp.arange(rows * GRID_W)
    row = (pos // GRID_W).astype(f32)
    col = (pos % GRID_W).astype(f32)
    half = t.shape[-1] // 2
    inv_freq = ROPE_BASE ** (-jnp.arange(0, half, 2, dtype=f32) / half)

    def rot(u, p):
        ang = p[:, None] * inv_freq
        cos, sin = jnp.cos(ang)[:, None, :], jnp.sin(ang)[:, None, :]
        u1, u2 = jnp.split(u, 2, axis=-1)
        return jnp.concatenate([u1 * cos - u2 * sin, u2 * cos + u1 * sin], axis=-1)

    t32 = t.astype(f32)
    return jnp.concatenate([rot(t32[..., :half], row), rot(t32[..., half:], col)], axis=-1).astype(t.dtype)


def _ssd_mixer(h_ctx, h_lat, w_in, conv_w, conv_b, dt_bias, a_log, d_skip, norm_g, w_out, keep_ctx):
    f32 = jnp.float32
    lc = h_ctx.shape[1]
    u = jnp.concatenate([h_ctx, h_lat], axis=1) @ w_in
    bsz, t = u.shape[:2]
    z, xbc, dt = jnp.split(u, [SSD_D_INNER, SSD_D_INNER + SSD_CONV_CH], axis=-1)
    xbc = jax.nn.silu(_conv_split(xbc, conv_w, lc) + conv_b)
    xs, bm, cm = jnp.split(xbc, [SSD_D_INNER, SSD_D_INNER + SSD_GROUPS * SSD_STATE], axis=-1)
    r = SSD_HEADS // SSD_GROUPS
    xs = xs.reshape(bsz, t, SSD_GROUPS, r, SSD_HEAD_DIM)
    bm = bm.reshape(bsz, t, SSD_GROUPS, SSD_STATE)
    cm = cm.reshape(bsz, t, SSD_GROUPS, SSD_STATE)
    dt = jax.nn.softplus(dt.reshape(bsz, t, 2, SSD_HEADS).astype(f32) + dt_bias.astype(f32))
    log_a = -jnp.exp(a_log.astype(f32)) * dt
    grp = lambda a: a.reshape(bsz, t, SSD_GROUPS, r)
    y = _scalar_decay_scan(cm, bm, xs * grp(dt[:, :, 0])[..., None], grp(log_a[:, :, 0]))
    y = y + _rev(_scalar_decay_scan(_rev(cm, lc), _rev(bm, lc), _rev(xs * grp(dt[:, :, 1])[..., None], lc),
                                    _rev(grp(log_a[:, :, 1]), lc)), lc)
    y = y + d_skip.reshape(SSD_GROUPS, r)[..., None] * xs
    start = 0 if keep_ctx else lc
    n = t - start
    y = y.reshape(bsz, t, SSD_D_INNER)[:, start:].astype(h_lat.dtype) * jax.nn.silu(z[:, start:])
    y = _rmsnorm(y.reshape(bsz, n, SSD_GROUPS, -1), norm_g.reshape(SSD_GROUPS, -1)).reshape(bsz, n, SSD_D_INNER)
    return _split_out(y @ w_out, lc, keep_ctx)


def _retention_mixer(h_ctx, h_lat, w_in, log_decay, w_out, rows, keep_ctx):
    f32 = jnp.float32
    lc = h_ctx.shape[1]
    u = jnp.concatenate([h_ctx, h_lat], axis=1) @ w_in
    bsz, t = u.shape[:2]
    q, k, v, g = jnp.split(u, [D_MODEL, 2 * D_MODEL, 2 * D_MODEL + RET_DV], axis=-1)
    q = q.reshape(bsz, t, RET_HEADS, RET_QK_DIM)
    k = k.reshape(bsz, t, RET_HEADS, RET_QK_DIM) * RET_QK_DIM ** -0.5
    q = jnp.concatenate([q[:, :lc], _rope_2d(q[:, lc:], rows)], axis=1)
    k = jnp.concatenate([k[:, :lc], _rope_2d(k[:, lc:], rows)], axis=1)
    v = v.reshape(bsz, t, RET_HEADS, 1, RET_V_DIM)
    ld_f = jnp.broadcast_to(log_decay[0].astype(f32)[:, None], (bsz, t, RET_HEADS, 1))
    ld_b = jnp.broadcast_to(log_decay[1].astype(f32)[:, None], (bsz, t, RET_HEADS, 1))
    y = _scalar_decay_scan(q, k, v, ld_f)
    y = y + _rev(_scalar_decay_scan(_rev(q, lc), _rev(k, lc), _rev(v, lc), ld_b), lc)
    start = 0 if keep_ctx else lc
    n = t - start
    y = y[:, start:].reshape(bsz, n, RET_HEADS, RET_V_DIM)
    mu = jnp.mean(y, axis=-1, keepdims=True)
    var = jnp.mean(jnp.square(y - mu), axis=-1, keepdims=True)
    y = ((y - mu) * lax.rsqrt(var + NORM_EPS)).reshape(bsz, n, RET_DV).astype(h_lat.dtype)
    y = y * jax.nn.silu(g[:, start:])
    return _split_out(y @ w_out, lc, keep_ctx)


def _lower_bound(lb_logits, layer):
    p = jax.nn.softmax(lb_logits.astype(jnp.float32), axis=0)
    return jnp.cumsum(p, axis=0)[layer] - p[0]


def _hgrn2_mixer(h_ctx, h_lat, w_in, lb, norm_g, w_out, keep_ctx):
    f32 = jnp.float32
    lc = h_ctx.shape[1]
    u = jnp.concatenate([h_ctx, h_lat], axis=1) @ w_in
    bsz, t = u.shape[:2]
    q, f_f, f_b, i, g = jnp.split(u, 5, axis=-1)
    shp = (bsz, t, HGRN_HEADS, HGRN_EXPAND)
    q, i = q.reshape(shp), i.reshape(shp)
    lb = lb.reshape(HGRN_HEADS, HGRN_EXPAND)

    def gates(f):
        f = f.reshape(shp).astype(f32)
        log_f = jnp.logaddexp(jnp.log(lb), jnp.log1p(-lb) + jax.nn.log_sigmoid(f))
        return log_f, (1 - lb) * jax.nn.sigmoid(-f)

    lf_f, k_f = gates(f_f)
    lf_b, k_b = gates(f_b)
    y = _vector_decay_scan(q, k_f, i, lf_f)
    y = y + _rev(_vector_decay_scan(_rev(q, lc), _rev(k_b, lc), _rev(i, lc), _rev(lf_b, lc)), lc)
    start = 0 if keep_ctx else lc
    n = t - start
    y = _rmsnorm(y[:, start:].astype(h_lat.dtype), norm_g.reshape(HGRN_HEADS, HGRN_EXPAND))
    y = (y * jax.nn.silu(g[:, start:].reshape(bsz, n, HGRN_HEADS, HGRN_EXPAND))).reshape(bsz, n, D_MODEL)
    return _split_out(y @ w_out, lc, keep_ctx)


def _gdn_mixer(h_ctx, h_lat, w_in, conv_w, dt_bias, a_log, norm_g, w_out, keep_ctx):
    f32 = jnp.float32
    lc = h_ctx.shape[1]
    u = jnp.concatenate([h_ctx, h_lat], axis=1) @ w_in
    bsz, t = u.shape[:2]
    qkv, z, bt, a = jnp.split(u, [GDN_CONV_CH, GDN_CONV_CH + GDN_DV, GDN_CONV_CH + GDN_DV + 2 * GDN_V_HEADS], axis=-1)
    qkv = jax.nn.silu(_conv_split(qkv, conv_w, lc))
    q, k, v = jnp.split(qkv, [GDN_DK, 2 * GDN_DK], axis=-1)
    r = GDN_V_HEADS // GDN_K_HEADS
    q = _l2norm(q.reshape(bsz, t, GDN_K_HEADS, GDN_HEAD_DIM)) * GDN_HEAD_DIM ** -0.5
    k = _l2norm(k.reshape(bsz, t, GDN_K_HEADS, GDN_HEAD_DIM))
    v = v.reshape(bsz, t, GDN_K_HEADS, r, GDN_HEAD_DIM)
    beta = jax.nn.sigmoid(bt.reshape(bsz, t, 2, GDN_K_HEADS, r).astype(f32))
    log_a = -jnp.exp(a_log.astype(f32)).reshape(2, GDN_K_HEADS, r) * jax.nn.softplus(
        a.reshape(bsz, t, 2, GDN_K_HEADS, r).astype(f32) + dt_bias.astype(f32).reshape(2, GDN_K_HEADS, r))
    y = _delta_scan(q, k, v, beta[:, :, 0], log_a[:, :, 0])
    y = y + _rev(_delta_scan(_rev(q, lc), _rev(k, lc), _rev(v, lc), _rev(beta[:, :, 1], lc),
                             _rev(log_a[:, :, 1], lc)), lc)
    start = 0 if keep_ctx else lc
    n = t - start
    y = y[:, start:].reshape(bsz, n, GDN_V_HEADS, GDN_HEAD_DIM).astype(h_lat.dtype)
    y = _rmsnorm(y, norm_g) * jax.nn.silu(z[:, start:].reshape(bsz, n, GDN_V_HEADS, GDN_HEAD_DIM))
    return _split_out(y.reshape(bsz, n, GDN_DV) @ w_out, lc, keep_ctx)


def _n_occ(m):
    return len(range(m, DEPTH, N_MIXERS))


def _fwd_setup_inputs(seed: int = 0) -> dict:
    key = jax.random.key(seed)
    ks = iter(jax.random.split(key, 48))
    f32 = jnp.float32
    d = D_MODEL

    def nrm(shape, std):
        return std * jax.random.normal(next(ks), shape, f32)

    def gain(shape):
        return 1.0 + nrm(shape, 0.02)

    def dt_bias(shape):
        dt = jnp.exp(jax.random.uniform(next(ks), shape, f32, math.log(1e-3), math.log(1e-1)))
        return dt + jnp.log(-jnp.expm1(-dt))

    def a_log(shape):
        return jnp.log(jax.random.uniform(next(ks), shape, f32, 1.0, 16.0))

    n_a, n_b, n_c, n_d = (_n_occ(m) for m in range(N_MIXERS))
    ret_base = jnp.log1p(-(2.0 ** (-5.0 - jnp.arange(RET_HEADS, dtype=f32))))
    return {
        'x': nrm((BATCH, SEQ, d), 1.0),
        'c': nrm((BATCH, d), 1.0),
        'ctx': nrm((BATCH, CTX_LEN, d), 1.0),
        'c_ctx': nrm((d,), 1.0),
        'ada_w': nrm((DEPTH, d, 6 * d), 0.5 * d ** -0.5),
        'ada_b': nrm((DEPTH, 6 * d), 0.02),
        'norm_g': gain((DEPTH, 2, d)),
        'mlp_w1': nrm((DEPTH, d, D_FF), d ** -0.5),
        'mlp_w2': nrm((DEPTH, D_FF, d), D_FF ** -0.5),
        'final_g': gain((d,)),
        'ssd_w_in': nrm((n_a, d, SSD_IN), d ** -0.5),
        'ssd_conv_w': nrm((n_a, CONV_W, SSD_CONV_CH), CONV_W ** -0.5),
        'ssd_conv_b': nrm((n_a, SSD_CONV_CH), 0.02),
        'ssd_dt_bias': dt_bias((n_a, 2, SSD_HEADS)),
        'ssd_a_log': a_log((n_a, 2, SSD_HEADS)),
        'ssd_d': gain((n_a, SSD_HEADS)),
        'ssd_norm_g': gain((n_a, SSD_D_INNER)),
        'ssd_w_out': nrm((n_a, SSD_D_INNER, d), SSD_D_INNER ** -0.5),
        'ret_w_in': nrm((n_b, d, RET_IN), d ** -0.5),
        'ret_log_decay': ret_base * jnp.exp(nrm((n_b, 2, RET_HEADS), 0.1)),
        'ret_w_out': nrm((n_b, RET_DV, d), RET_DV ** -0.5),
        'hgrn_w_in': nrm((n_c, d, HGRN_IN), d ** -0.5),
        'hgrn_lb_logits': nrm((DEPTH, d), 0.1),
        'hgrn_norm_g': gain((n_c, d)),
        'hgrn_w_out': nrm((n_c, d, d), d ** -0.5),
        'gdn_w_in': nrm((n_d, d, GDN_IN), d ** -0.5),
        'gdn_conv_w': nrm((n_d, CONV_W, GDN_CONV_CH), CONV_W ** -0.5),
        'gdn_dt_bias': dt_bias((n_d, 2, GDN_V_HEADS)),
        'gdn_a_log': a_log((n_d, 2, GDN_V_HEADS)),
        'gdn_norm_g': gain((n_d, GDN_HEAD_DIM)),
        'gdn_w_out': nrm((n_d, GDN_DV, d), GDN_DV ** -0.5),
    }


def _fwd_reference(x, c, ctx, c_ctx, ada_w, ada_b, norm_g, mlp_w1, mlp_w2, final_g,
              ssd_w_in, ssd_conv_w, ssd_conv_b, ssd_dt_bias, ssd_a_log, ssd_d, ssd_norm_g, ssd_w_out,
              ret_w_in, ret_log_decay, ret_w_out,
              hgrn_w_in, hgrn_lb_logits, hgrn_norm_g, hgrn_w_out,
              gdn_w_in, gdn_conv_w, gdn_dt_bias, gdn_a_log, gdn_norm_g, gdn_w_out):
    bsz = x.shape[0]
    rows = x.shape[1] // GRID_W
    x_lat, x_ctx = x, ctx
    cond = jax.nn.silu(jnp.concatenate([c, c_ctx[None]], axis=0))
    for i in range(DEPTH):
        mixer, occ = i % N_MIXERS, i // N_MIXERS
        keep_ctx = i < DEPTH - 1
        mod = cond @ ada_w[i] + ada_b[i]
        sh1, sc1, g1, sh2, sc2, g2 = jnp.split(mod, 6, axis=-1)
        h_lat = _adaln(x_lat, norm_g[i, 0], sh1[:bsz, None], sc1[:bsz, None])
        h_ctx = _adaln(x_ctx, norm_g[i, 0], sh1[bsz], sc1[bsz])
        if mixer == 0:
            y_ctx, y_lat = _ssd_mixer(h_ctx, h_lat, ssd_w_in[occ], ssd_conv_w[occ], ssd_conv_b[occ], ssd_dt_bias[occ],
                                      ssd_a_log[occ], ssd_d[occ], ssd_norm_g[occ], ssd_w_out[occ], keep_ctx)
        elif mixer == 1:
            y_ctx, y_lat = _retention_mixer(h_ctx, h_lat, ret_w_in[occ], ret_log_decay[occ], ret_w_out[occ],
                                            rows, keep_ctx)
        elif mixer == 2:
            y_ctx, y_lat = _hgrn2_mixer(h_ctx, h_lat, hgrn_w_in[occ], _lower_bound(hgrn_lb_logits, i),
                                        hgrn_norm_g[occ], hgrn_w_out[occ], keep_ctx)
        else:
            y_ctx, y_lat = _gdn_mixer(h_ctx, h_lat, gdn_w_in[occ], gdn_conv_w[occ], gdn_dt_bias[occ],
                                      gdn_a_log[occ], gdn_norm_g[occ], gdn_w_out[occ], keep_ctx)
        x_lat = x_lat + g1[:bsz, None] * y_lat
        h_lat = _adaln(x_lat, norm_g[i, 1], sh2[:bsz, None], sc2[:bsz, None])
        x_lat = x_lat + g2[:bsz, None] * _sq_relu_mlp(h_lat, mlp_w1[i], mlp_w2[i])
        if keep_ctx:
            x_ctx = x_ctx + g1[bsz] * y_ctx
            h_ctx = _adaln(x_ctx, norm_g[i, 1], sh2[bsz], sc2[bsz])
            x_ctx = x_ctx + g2[bsz] * _sq_relu_mlp(h_ctx, mlp_w1[i], mlp_w2[i])
    return _rmsnorm(x_lat, final_g)


import jax as _jax
import jax.numpy as _jnp

TWIN_FORMAT = 'train_step'
FWD_PARAMS = ['x', 'c', 'ctx', 'c_ctx', 'ada_w', 'ada_b', 'norm_g', 'mlp_w1', 'mlp_w2', 'final_g', 'ssd_w_in', 'ssd_conv_w', 'ssd_conv_b', 'ssd_dt_bias', 'ssd_a_log', 'ssd_d', 'ssd_norm_g', 'ssd_w_out', 'ret_w_in', 'ret_log_decay', 'ret_w_out', 'hgrn_w_in', 'hgrn_lb_logits', 'hgrn_norm_g', 'hgrn_w_out', 'gdn_w_in', 'gdn_conv_w', 'gdn_dt_bias', 'gdn_a_log', 'gdn_norm_g', 'gdn_w_out']
TWIN_WEIGHTS = ['c_ctx', 'ada_w', 'ada_b', 'norm_g', 'mlp_w1', 'mlp_w2', 'final_g', 'ssd_w_in', 'ssd_conv_w', 'ssd_conv_b', 'ssd_dt_bias', 'ssd_a_log', 'ssd_d', 'ssd_norm_g', 'ssd_w_out', 'ret_w_in', 'ret_log_decay', 'ret_w_out', 'hgrn_w_in', 'hgrn_lb_logits', 'hgrn_norm_g', 'hgrn_w_out', 'gdn_w_in', 'gdn_conv_w', 'gdn_dt_bias', 'gdn_a_log', 'gdn_norm_g', 'gdn_w_out']
TWIN_DIFF_INPUT = 'x'
TWIN_INPUTS = ['x', 'c', 'ctx', 'c_ctx', 'ada_w', 'ada_b', 'norm_g', 'mlp_w1', 'mlp_w2', 'final_g', 'ssd_w_in', 'ssd_conv_w', 'ssd_conv_b', 'ssd_dt_bias', 'ssd_a_log', 'ssd_d', 'ssd_norm_g', 'ssd_w_out', 'ret_w_in', 'ret_log_decay', 'ret_w_out', 'hgrn_w_in', 'hgrn_lb_logits', 'hgrn_norm_g', 'hgrn_w_out', 'gdn_w_in', 'gdn_conv_w', 'gdn_dt_bias', 'gdn_a_log', 'gdn_norm_g', 'gdn_w_out', 'loss_target', 'm_c_ctx', 'm_ada_w', 'm_ada_b', 'm_norm_g', 'm_mlp_w1', 'm_mlp_w2', 'm_final_g', 'm_ssd_w_in', 'm_ssd_conv_w', 'm_ssd_conv_b', 'm_ssd_dt_bias', 'm_ssd_a_log', 'm_ssd_d', 'm_ssd_norm_g', 'm_ssd_w_out', 'm_ret_w_in', 'm_ret_log_decay', 'm_ret_w_out', 'm_hgrn_w_in', 'm_hgrn_lb_logits', 'm_hgrn_norm_g', 'm_hgrn_w_out', 'm_gdn_w_in', 'm_gdn_conv_w', 'm_gdn_dt_bias', 'm_gdn_a_log', 'm_gdn_norm_g', 'm_gdn_w_out', 'v_c_ctx', 'v_ada_w', 'v_ada_b', 'v_norm_g', 'v_mlp_w1', 'v_mlp_w2', 'v_final_g', 'v_ssd_w_in', 'v_ssd_conv_w', 'v_ssd_conv_b', 'v_ssd_dt_bias', 'v_ssd_a_log', 'v_ssd_d', 'v_ssd_norm_g', 'v_ssd_w_out', 'v_ret_w_in', 'v_ret_log_decay', 'v_ret_w_out', 'v_hgrn_w_in', 'v_hgrn_lb_logits', 'v_hgrn_norm_g', 'v_hgrn_w_out', 'v_gdn_w_in', 'v_gdn_conv_w', 'v_gdn_dt_bias', 'v_gdn_a_log', 'v_gdn_norm_g', 'v_gdn_w_out']
TWIN_OUTPUTS = ['loss', 'grad_x', 'grad_c_ctx', 'grad_ada_w', 'grad_ada_b', 'grad_norm_g', 'grad_mlp_w1', 'grad_mlp_w2', 'grad_final_g', 'grad_ssd_w_in', 'grad_ssd_conv_w', 'grad_ssd_conv_b', 'grad_ssd_dt_bias', 'grad_ssd_a_log', 'grad_ssd_d', 'grad_ssd_norm_g', 'grad_ssd_w_out', 'grad_ret_w_in', 'grad_ret_log_decay', 'grad_ret_w_out', 'grad_hgrn_w_in', 'grad_hgrn_lb_logits', 'grad_hgrn_norm_g', 'grad_hgrn_w_out', 'grad_gdn_w_in', 'grad_gdn_conv_w', 'grad_gdn_dt_bias', 'grad_gdn_a_log', 'grad_gdn_norm_g', 'grad_gdn_w_out', 'delta_c_ctx', 'delta_ada_w', 'delta_ada_b', 'delta_norm_g', 'delta_mlp_w1', 'delta_mlp_w2', 'delta_final_g', 'delta_ssd_w_in', 'delta_ssd_conv_w', 'delta_ssd_conv_b', 'delta_ssd_dt_bias', 'delta_ssd_a_log', 'delta_ssd_d', 'delta_ssd_norm_g', 'delta_ssd_w_out', 'delta_ret_w_in', 'delta_ret_log_decay', 'delta_ret_w_out', 'delta_hgrn_w_in', 'delta_hgrn_lb_logits', 'delta_hgrn_norm_g', 'delta_hgrn_w_out', 'delta_gdn_w_in', 'delta_gdn_conv_w', 'delta_gdn_dt_bias', 'delta_gdn_a_log', 'delta_gdn_norm_g', 'delta_gdn_w_out', 'new_m_c_ctx', 'new_m_ada_w', 'new_m_ada_b', 'new_m_norm_g', 'new_m_mlp_w1', 'new_m_mlp_w2', 'new_m_final_g', 'new_m_ssd_w_in', 'new_m_ssd_conv_w', 'new_m_ssd_conv_b', 'new_m_ssd_dt_bias', 'new_m_ssd_a_log', 'new_m_ssd_d', 'new_m_ssd_norm_g', 'new_m_ssd_w_out', 'new_m_ret_w_in', 'new_m_ret_log_decay', 'new_m_ret_w_out', 'new_m_hgrn_w_in', 'new_m_hgrn_lb_logits', 'new_m_hgrn_norm_g', 'new_m_hgrn_w_out', 'new_m_gdn_w_in', 'new_m_gdn_conv_w', 'new_m_gdn_dt_bias', 'new_m_gdn_a_log', 'new_m_gdn_norm_g', 'new_m_gdn_w_out', 'new_v_c_ctx', 'new_v_ada_w', 'new_v_ada_b', 'new_v_norm_g', 'new_v_mlp_w1', 'new_v_mlp_w2', 'new_v_final_g', 'new_v_ssd_w_in', 'new_v_ssd_conv_w', 'new_v_ssd_conv_b', 'new_v_ssd_dt_bias', 'new_v_ssd_a_log', 'new_v_ssd_d', 'new_v_ssd_norm_g', 'new_v_ssd_w_out', 'new_v_ret_w_in', 'new_v_ret_log_decay', 'new_v_ret_w_out', 'new_v_hgrn_w_in', 'new_v_hgrn_lb_logits', 'new_v_hgrn_norm_g', 'new_v_hgrn_w_out', 'new_v_gdn_w_in', 'new_v_gdn_conv_w', 'new_v_gdn_dt_bias', 'new_v_gdn_a_log', 'new_v_gdn_norm_g', 'new_v_gdn_w_out']
TWIN_LEAF_KINDS = {'loss': 'loss', 'grad_x': 'grad_x', 'grad_c_ctx': 'grad_w', 'grad_ada_w': 'grad_w', 'grad_ada_b': 'grad_w', 'grad_norm_g': 'grad_w', 'grad_mlp_w1': 'grad_w', 'grad_mlp_w2': 'grad_w', 'grad_final_g': 'grad_w', 'grad_ssd_w_in': 'grad_w', 'grad_ssd_conv_w': 'grad_w', 'grad_ssd_conv_b': 'grad_w', 'grad_ssd_dt_bias': 'grad_w', 'grad_ssd_a_log': 'grad_w', 'grad_ssd_d': 'grad_w', 'grad_ssd_norm_g': 'grad_w', 'grad_ssd_w_out': 'grad_w', 'grad_ret_w_in': 'grad_w', 'grad_ret_log_decay': 'grad_w', 'grad_ret_w_out': 'grad_w', 'grad_hgrn_w_in': 'grad_w', 'grad_hgrn_lb_logits': 'grad_w', 'grad_hgrn_norm_g': 'grad_w', 'grad_hgrn_w_out': 'grad_w', 'grad_gdn_w_in': 'grad_w', 'grad_gdn_conv_w': 'grad_w', 'grad_gdn_dt_bias': 'grad_w', 'grad_gdn_a_log': 'grad_w', 'grad_gdn_norm_g': 'grad_w', 'grad_gdn_w_out': 'grad_w', 'delta_c_ctx': 'delta_w', 'delta_ada_w': 'delta_w', 'delta_ada_b': 'delta_w', 'delta_norm_g': 'delta_w', 'delta_mlp_w1': 'delta_w', 'delta_mlp_w2': 'delta_w', 'delta_final_g': 'delta_w', 'delta_ssd_w_in': 'delta_w', 'delta_ssd_conv_w': 'delta_w', 'delta_ssd_conv_b': 'delta_w', 'delta_ssd_dt_bias': 'delta_w', 'delta_ssd_a_log': 'delta_w', 'delta_ssd_d': 'delta_w', 'delta_ssd_norm_g': 'delta_w', 'delta_ssd_w_out': 'delta_w', 'delta_ret_w_in': 'delta_w', 'delta_ret_log_decay': 'delta_w', 'delta_ret_w_out': 'delta_w', 'delta_hgrn_w_in': 'delta_w', 'delta_hgrn_lb_logits': 'delta_w', 'delta_hgrn_norm_g': 'delta_w', 'delta_hgrn_w_out': 'delta_w', 'delta_gdn_w_in': 'delta_w', 'delta_gdn_conv_w': 'delta_w', 'delta_gdn_dt_bias': 'delta_w', 'delta_gdn_a_log': 'delta_w', 'delta_gdn_norm_g': 'delta_w', 'delta_gdn_w_out': 'delta_w', 'new_m_c_ctx': 'new_m', 'new_m_ada_w': 'new_m', 'new_m_ada_b': 'new_m', 'new_m_norm_g': 'new_m', 'new_m_mlp_w1': 'new_m', 'new_m_mlp_w2': 'new_m', 'new_m_final_g': 'new_m', 'new_m_ssd_w_in': 'new_m', 'new_m_ssd_conv_w': 'new_m', 'new_m_ssd_conv_b': 'new_m', 'new_m_ssd_dt_bias': 'new_m', 'new_m_ssd_a_log': 'new_m', 'new_m_ssd_d': 'new_m', 'new_m_ssd_norm_g': 'new_m', 'new_m_ssd_w_out': 'new_m', 'new_m_ret_w_in': 'new_m', 'new_m_ret_log_decay': 'new_m', 'new_m_ret_w_out': 'new_m', 'new_m_hgrn_w_in': 'new_m', 'new_m_hgrn_lb_logits': 'new_m', 'new_m_hgrn_norm_g': 'new_m', 'new_m_hgrn_w_out': 'new_m', 'new_m_gdn_w_in': 'new_m', 'new_m_gdn_conv_w': 'new_m', 'new_m_gdn_dt_bias': 'new_m', 'new_m_gdn_a_log': 'new_m', 'new_m_gdn_norm_g': 'new_m', 'new_m_gdn_w_out': 'new_m', 'new_v_c_ctx': 'new_v', 'new_v_ada_w': 'new_v', 'new_v_ada_b': 'new_v', 'new_v_norm_g': 'new_v', 'new_v_mlp_w1': 'new_v', 'new_v_mlp_w2': 'new_v', 'new_v_final_g': 'new_v', 'new_v_ssd_w_in': 'new_v', 'new_v_ssd_conv_w': 'new_v', 'new_v_ssd_conv_b': 'new_v', 'new_v_ssd_dt_bias': 'new_v', 'new_v_ssd_a_log': 'new_v', 'new_v_ssd_d': 'new_v', 'new_v_ssd_norm_g': 'new_v', 'new_v_ssd_w_out': 'new_v', 'new_v_ret_w_in': 'new_v', 'new_v_ret_log_decay': 'new_v', 'new_v_ret_w_out': 'new_v', 'new_v_hgrn_w_in': 'new_v', 'new_v_hgrn_lb_logits': 'new_v', 'new_v_hgrn_norm_g': 'new_v', 'new_v_hgrn_w_out': 'new_v', 'new_v_gdn_w_in': 'new_v', 'new_v_gdn_conv_w': 'new_v', 'new_v_gdn_dt_bias': 'new_v', 'new_v_gdn_a_log': 'new_v', 'new_v_gdn_norm_g': 'new_v', 'new_v_gdn_w_out': 'new_v'}


def _forward(args):
    return _fwd_reference(*[args[k] for k in FWD_PARAMS])


def _output_shape():
    out = _jax.eval_shape(lambda: _forward(_fwd_setup_inputs(0)))
    return out.shape, out.dtype

N_MICROBATCH = 1
ADAM_LR = 0.001
ADAM_B1 = 0.9
ADAM_B2 = 0.999
ADAM_EPS = 1e-08
ADAM_WD = 0.01
ADAM_STEP = 10
PER_EXAMPLE_BATCH_AXIS = {'x': 0, 'c': 0, 'ctx': 0, 'loss_target': 0}
SHARED_INPUTS = []
_WEIGHT_DTYPES = {'c_ctx': _jnp.float32, 'ada_w': _jnp.float32, 'ada_b': _jnp.float32, 'norm_g': _jnp.float32, 'mlp_w1': _jnp.float32, 'mlp_w2': _jnp.float32, 'final_g': _jnp.float32, 'ssd_w_in': _jnp.float32, 'ssd_conv_w': _jnp.float32, 'ssd_conv_b': _jnp.float32, 'ssd_dt_bias': _jnp.float32, 'ssd_a_log': _jnp.float32, 'ssd_d': _jnp.float32, 'ssd_norm_g': _jnp.float32, 'ssd_w_out': _jnp.float32, 'ret_w_in': _jnp.float32, 'ret_log_decay': _jnp.float32, 'ret_w_out': _jnp.float32, 'hgrn_w_in': _jnp.float32, 'hgrn_lb_logits': _jnp.float32, 'hgrn_norm_g': _jnp.float32, 'hgrn_w_out': _jnp.float32, 'gdn_w_in': _jnp.float32, 'gdn_conv_w': _jnp.float32, 'gdn_dt_bias': _jnp.float32, 'gdn_a_log': _jnp.float32, 'gdn_norm_g': _jnp.float32, 'gdn_w_out': _jnp.float32}
MOMENT_SCALE = {'c_ctx': 1.196477e-02, 'ada_w': 2.794223e-02, 'ada_b': 4.773367e-02, 'norm_g': 2.491142e-02, 'mlp_w1': 1.355449e-02, 'mlp_w2': 2.440782e-02, 'final_g': 8.096885e+00, 'ssd_w_in': 1.427026e-02, 'ssd_conv_w': 1.319950e-02, 'ssd_conv_b': 1.721369e-02, 'ssd_dt_bias': 2.955965e-02, 'ssd_a_log': 3.922751e-02, 'ssd_d': 6.814184e-02, 'ssd_norm_g': 1.511980e-02, 'ssd_w_out': 2.139869e-02, 'ret_w_in': 1.137110e-02, 'ret_log_decay': 2.267824e+01, 'ret_w_out': 1.205596e-02, 'hgrn_w_in': 1.145339e-02, 'hgrn_lb_logits': 3.281306e-03, 'hgrn_norm_g': 1.060687e-02, 'hgrn_w_out': 1.046229e-02, 'gdn_w_in': 6.414469e-03, 'gdn_conv_w': 6.400155e-03, 'gdn_dt_bias': 1.897015e-02, 'gdn_a_log': 1.928821e-02, 'gdn_norm_g': 4.611196e-02, 'gdn_w_out': 9.671527e-03}


def _to_microbatches(a, axis):
    t = _jnp.moveaxis(a, axis, 0)
    t = t.reshape((N_MICROBATCH, t.shape[0] // N_MICROBATCH) + t.shape[1:])
    return _jnp.moveaxis(t, 1, axis + 1)


def setup_inputs(seed: int = 0) -> dict:
    inp = _fwd_setup_inputs(seed)
    key = _jax.random.fold_in(_jax.random.key(seed), 7919)
    shape, _ = _output_shape()
    out = dict(inp)
    out["loss_target"] = _jax.random.normal(_jax.random.fold_in(key, 0), shape, _jnp.float32)
    for i, name in enumerate(TWIN_WEIGHTS):
        w = inp[name].astype(_jnp.float32)
        if MOMENT_SCALE is None:
            s = _jnp.sqrt(_jnp.mean(_jnp.square(w)) + 1e-30)
        else:
            s = MOMENT_SCALE[name]
        km, kv = _jax.random.split(_jax.random.fold_in(key, i + 1))
        out[name] = w
        out["m_" + name] = s * _jax.random.normal(km, w.shape, _jnp.float32)
        out["v_" + name] = (s * s) * _jax.random.uniform(kv, w.shape, _jnp.float32, 0.5, 1.5)
    if N_MICROBATCH > 1:
        for name, axis in PER_EXAMPLE_BATCH_AXIS.items():
            out[name] = _to_microbatches(out[name], axis)
    return {'x': out['x'], 'c': out['c'], 'ctx': out['ctx'], 'c_ctx': out['c_ctx'], 'ada_w': out['ada_w'], 'ada_b': out['ada_b'], 'norm_g': out['norm_g'], 'mlp_w1': out['mlp_w1'], 'mlp_w2': out['mlp_w2'], 'final_g': out['final_g'], 'ssd_w_in': out['ssd_w_in'], 'ssd_conv_w': out['ssd_conv_w'], 'ssd_conv_b': out['ssd_conv_b'], 'ssd_dt_bias': out['ssd_dt_bias'], 'ssd_a_log': out['ssd_a_log'], 'ssd_d': out['ssd_d'], 'ssd_norm_g': out['ssd_norm_g'], 'ssd_w_out': out['ssd_w_out'], 'ret_w_in': out['ret_w_in'], 'ret_log_decay': out['ret_log_decay'], 'ret_w_out': out['ret_w_out'], 'hgrn_w_in': out['hgrn_w_in'], 'hgrn_lb_logits': out['hgrn_lb_logits'], 'hgrn_norm_g': out['hgrn_norm_g'], 'hgrn_w_out': out['hgrn_w_out'], 'gdn_w_in': out['gdn_w_in'], 'gdn_conv_w': out['gdn_conv_w'], 'gdn_dt_bias': out['gdn_dt_bias'], 'gdn_a_log': out['gdn_a_log'], 'gdn_norm_g': out['gdn_norm_g'], 'gdn_w_out': out['gdn_w_out'], 'loss_target': out['loss_target'], 'm_c_ctx': out['m_c_ctx'], 'm_ada_w': out['m_ada_w'], 'm_ada_b': out['m_ada_b'], 'm_norm_g': out['m_norm_g'], 'm_mlp_w1': out['m_mlp_w1'], 'm_mlp_w2': out['m_mlp_w2'], 'm_final_g': out['m_final_g'], 'm_ssd_w_in': out['m_ssd_w_in'], 'm_ssd_conv_w': out['m_ssd_conv_w'], 'm_ssd_conv_b': out['m_ssd_conv_b'], 'm_ssd_dt_bias': out['m_ssd_dt_bias'], 'm_ssd_a_log': out['m_ssd_a_log'], 'm_ssd_d': out['m_ssd_d'], 'm_ssd_norm_g': out['m_ssd_norm_g'], 'm_ssd_w_out': out['m_ssd_w_out'], 'm_ret_w_in': out['m_ret_w_in'], 'm_ret_log_decay': out['m_ret_log_decay'], 'm_ret_w_out': out['m_ret_w_out'], 'm_hgrn_w_in': out['m_hgrn_w_in'], 'm_hgrn_lb_logits': out['m_hgrn_lb_logits'], 'm_hgrn_norm_g': out['m_hgrn_norm_g'], 'm_hgrn_w_out': out['m_hgrn_w_out'], 'm_gdn_w_in': out['m_gdn_w_in'], 'm_gdn_conv_w': out['m_gdn_conv_w'], 'm_gdn_dt_bias': out['m_gdn_dt_bias'], 'm_gdn_a_log': out['m_gdn_a_log'], 'm_gdn_norm_g': out['m_gdn_norm_g'], 'm_gdn_w_out': out['m_gdn_w_out'], 'v_c_ctx': out['v_c_ctx'], 'v_ada_w': out['v_ada_w'], 'v_ada_b': out['v_ada_b'], 'v_norm_g': out['v_norm_g'], 'v_mlp_w1': out['v_mlp_w1'], 'v_mlp_w2': out['v_mlp_w2'], 'v_final_g': out['v_final_g'], 'v_ssd_w_in': out['v_ssd_w_in'], 'v_ssd_conv_w': out['v_ssd_conv_w'], 'v_ssd_conv_b': out['v_ssd_conv_b'], 'v_ssd_dt_bias': out['v_ssd_dt_bias'], 'v_ssd_a_log': out['v_ssd_a_log'], 'v_ssd_d': out['v_ssd_d'], 'v_ssd_norm_g': out['v_ssd_norm_g'], 'v_ssd_w_out': out['v_ssd_w_out'], 'v_ret_w_in': out['v_ret_w_in'], 'v_ret_log_decay': out['v_ret_log_decay'], 'v_ret_w_out': out['v_ret_w_out'], 'v_hgrn_w_in': out['v_hgrn_w_in'], 'v_hgrn_lb_logits': out['v_hgrn_lb_logits'], 'v_hgrn_norm_g': out['v_hgrn_norm_g'], 'v_hgrn_w_out': out['v_hgrn_w_out'], 'v_gdn_w_in': out['v_gdn_w_in'], 'v_gdn_conv_w': out['v_gdn_conv_w'], 'v_gdn_dt_bias': out['v_gdn_dt_bias'], 'v_gdn_a_log': out['v_gdn_a_log'], 'v_gdn_norm_g': out['v_gdn_norm_g'], 'v_gdn_w_out': out['v_gdn_w_out']}


def _loss(weights, diff, rest, loss_target):
    with _jax.named_scope("forward"):
        args = {**rest, TWIN_DIFF_INPUT: diff, **{k: w.astype(_WEIGHT_DTYPES[k]) for k, w in weights.items()}}
        y = _forward(args)
    with _jax.named_scope("loss_head"):
        err = _jnp.square(y.astype(_jnp.float32) - loss_target)
        return 0.5 * _jnp.sum(_jnp.mean(err, axis=-1)) if err.ndim else 0.5 * err


def _adamw(w, g, m, v):
    m = ADAM_B1 * m + (1.0 - ADAM_B1) * g
    v = ADAM_B2 * v + (1.0 - ADAM_B2) * _jnp.square(g)
    m_hat = m / (1.0 - ADAM_B1 ** ADAM_STEP)
    v_hat = v / (1.0 - ADAM_B2 ** ADAM_STEP)
    delta = -ADAM_LR * (m_hat / (_jnp.sqrt(v_hat) + ADAM_EPS) + ADAM_WD * w)
    return delta, m, v


def reference(x, c, ctx, c_ctx, ada_w, ada_b, norm_g, mlp_w1, mlp_w2, final_g, ssd_w_in, ssd_conv_w, ssd_conv_b, ssd_dt_bias, ssd_a_log, ssd_d, ssd_norm_g, ssd_w_out, ret_w_in, ret_log_decay, ret_w_out, hgrn_w_in, hgrn_lb_logits, hgrn_norm_g, hgrn_w_out, gdn_w_in, gdn_conv_w, gdn_dt_bias, gdn_a_log, gdn_norm_g, gdn_w_out, loss_target, m_c_ctx, m_ada_w, m_ada_b, m_norm_g, m_mlp_w1, m_mlp_w2, m_final_g, m_ssd_w_in, m_ssd_conv_w, m_ssd_conv_b, m_ssd_dt_bias, m_ssd_a_log, m_ssd_d, m_ssd_norm_g, m_ssd_w_out, m_ret_w_in, m_ret_log_decay, m_ret_w_out, m_hgrn_w_in, m_hgrn_lb_logits, m_hgrn_norm_g, m_hgrn_w_out, m_gdn_w_in, m_gdn_conv_w, m_gdn_dt_bias, m_gdn_a_log, m_gdn_norm_g, m_gdn_w_out, v_c_ctx, v_ada_w, v_ada_b, v_norm_g, v_mlp_w1, v_mlp_w2, v_final_g, v_ssd_w_in, v_ssd_conv_w, v_ssd_conv_b, v_ssd_dt_bias, v_ssd_a_log, v_ssd_d, v_ssd_norm_g, v_ssd_w_out, v_ret_w_in, v_ret_log_decay, v_ret_w_out, v_hgrn_w_in, v_hgrn_lb_logits, v_hgrn_norm_g, v_hgrn_w_out, v_gdn_w_in, v_gdn_conv_w, v_gdn_dt_bias, v_gdn_a_log, v_gdn_norm_g, v_gdn_w_out):
    given = dict(x=x, c=c, ctx=ctx, c_ctx=c_ctx, ada_w=ada_w, ada_b=ada_b, norm_g=norm_g, mlp_w1=mlp_w1, mlp_w2=mlp_w2, final_g=final_g, ssd_w_in=ssd_w_in, ssd_conv_w=ssd_conv_w, ssd_conv_b=ssd_conv_b, ssd_dt_bias=ssd_dt_bias, ssd_a_log=ssd_a_log, ssd_d=ssd_d, ssd_norm_g=ssd_norm_g, ssd_w_out=ssd_w_out, ret_w_in=ret_w_in, ret_log_decay=ret_log_decay, ret_w_out=ret_w_out, hgrn_w_in=hgrn_w_in, hgrn_lb_logits=hgrn_lb_logits, hgrn_norm_g=hgrn_norm_g, hgrn_w_out=hgrn_w_out, gdn_w_in=gdn_w_in, gdn_conv_w=gdn_conv_w, gdn_dt_bias=gdn_dt_bias, gdn_a_log=gdn_a_log, gdn_norm_g=gdn_norm_g, gdn_w_out=gdn_w_out, loss_target=loss_target, m_c_ctx=m_c_ctx, m_ada_w=m_ada_w, m_ada_b=m_ada_b, m_norm_g=m_norm_g, m_mlp_w1=m_mlp_w1, m_mlp_w2=m_mlp_w2, m_final_g=m_final_g, m_ssd_w_in=m_ssd_w_in, m_ssd_conv_w=m_ssd_conv_w, m_ssd_conv_b=m_ssd_conv_b, m_ssd_dt_bias=m_ssd_dt_bias, m_ssd_a_log=m_ssd_a_log, m_ssd_d=m_ssd_d, m_ssd_norm_g=m_ssd_norm_g, m_ssd_w_out=m_ssd_w_out, m_ret_w_in=m_ret_w_in, m_ret_log_decay=m_ret_log_decay, m_ret_w_out=m_ret_w_out, m_hgrn_w_in=m_hgrn_w_in, m_hgrn_lb_logits=m_hgrn_lb_logits, m_hgrn_norm_g=m_hgrn_norm_g, m_hgrn_w_out=m_hgrn_w_out, m_gdn_w_in=m_gdn_w_in, m_gdn_conv_w=m_gdn_conv_w, m_gdn_dt_bias=m_gdn_dt_bias, m_gdn_a_log=m_gdn_a_log, m_gdn_norm_g=m_gdn_norm_g, m_gdn_w_out=m_gdn_w_out, v_c_ctx=v_c_ctx, v_ada_w=v_ada_w, v_ada_b=v_ada_b, v_norm_g=v_norm_g, v_mlp_w1=v_mlp_w1, v_mlp_w2=v_mlp_w2, v_final_g=v_final_g, v_ssd_w_in=v_ssd_w_in, v_ssd_conv_w=v_ssd_conv_w, v_ssd_conv_b=v_ssd_conv_b, v_ssd_dt_bias=v_ssd_dt_bias, v_ssd_a_log=v_ssd_a_log, v_ssd_d=v_ssd_d, v_ssd_norm_g=v_ssd_norm_g, v_ssd_w_out=v_ssd_w_out, v_ret_w_in=v_ret_w_in, v_ret_log_decay=v_ret_log_decay, v_ret_w_out=v_ret_w_out, v_hgrn_w_in=v_hgrn_w_in, v_hgrn_lb_logits=v_hgrn_lb_logits, v_hgrn_norm_g=v_hgrn_norm_g, v_hgrn_w_out=v_hgrn_w_out, v_gdn_w_in=v_gdn_w_in, v_gdn_conv_w=v_gdn_conv_w, v_gdn_dt_bias=v_gdn_dt_bias, v_gdn_a_log=v_gdn_a_log, v_gdn_norm_g=v_gdn_norm_g, v_gdn_w_out=v_gdn_w_out)
    weights = {n: given[n] for n in TWIN_WEIGHTS}
    shared = {n: given[n] for n in SHARED_INPUTS}
    per_example = {n: given[n] for n in ['x', 'c', 'ctx']}
    grad_fn = _jax.value_and_grad(_loss, argnums=(0, 1))

    def one_microbatch(ex, loss_target):
        ex = dict(ex)
        diff = ex.pop(TWIN_DIFF_INPUT)
        return grad_fn(weights, diff, {**shared, **ex}, loss_target)

    if N_MICROBATCH == 1:
        loss, (grad_w, grad_x) = one_microbatch(per_example, given["loss_target"])
    else:
        def body(carry, xs):
            loss_sum, grad_sum = carry
            l_k, (gw_k, gx_k) = one_microbatch(xs[0], xs[1])
            with _jax.named_scope("update"):
                return (loss_sum + l_k, _jax.tree.map(_jnp.add, grad_sum, gw_k)), gx_k

        init = (_jnp.zeros((), _jnp.float32), _jax.tree.map(_jnp.zeros_like, weights))
        (loss, grad_w), grad_x = _jax.lax.scan(body, init, (per_example, given["loss_target"]))
    with _jax.named_scope("update"):
        delta_w, new_m, new_v = {}, {}, {}
        for n in TWIN_WEIGHTS:
            delta_w[n], new_m[n], new_v[n] = _adamw(weights[n], grad_w[n], given["m_" + n], given["v_" + n])
    return (loss, grad_x, *[grad_w[n] for n in TWIN_WEIGHTS], *[delta_w[n] for n in TWIN_WEIGHTS],
            *[new_m[n] for n in TWIN_WEIGHTS], *[new_v[n] for n in TWIN_WEIGHTS])
```

```python
import functools
import math

import jax
import jax.numpy as jnp
from jax import lax
from jax.experimental import pallas as pl
from jax.experimental.pallas import tpu as pltpu

f32 = jnp.float32
bf16 = jnp.bfloat16
HP = lax.Precision.HIGHEST
MESH = pl.DeviceIdType.MESH

D_MODEL = 2048
DEPTH = 4
GRID_W = 64
CHUNK = 64
SUB = 16
NORM_EPS = 1e-6
ROPE_BASE = 10000.0
N_CHIPS = 4
N_DEV = 8
LANES = 128
PACK_W = 2048
SMALL_W = 1024
VMEM_LIMIT = 48 * 1024 * 1024

SSD_D_INNER, SSD_HEADS, SSD_GROUPS, SSD_STATE, SSD_HEAD_DIM = 4096, 64, 8, 128, 64
SSD_CONV_CH = SSD_D_INNER + 2 * SSD_GROUPS * SSD_STATE
SSD_MAIN = SSD_D_INNER + SSD_CONV_CH
RET_HEADS, RET_QK, RET_V, RET_DV = 8, 256, 512, 4096
HGRN_HEADS, HGRN_EXPAND = 16, 128
GDN_K_HEADS, GDN_V_HEADS, GDN_HEAD = 16, 32, 128
GDN_DK, GDN_DV = 2048, 4096
GDN_CONV_CH = 2 * GDN_DK + GDN_DV
GDN_MAIN = GDN_CONV_CH + GDN_DV

ADAM_LR, ADAM_B1, ADAM_B2, ADAM_EPS, ADAM_WD, ADAM_STEP = 0.001, 0.9, 0.999, 1e-08, 0.01, 10

BIG = ("mlp_w1", "mlp_w2", "ssd_w_in", "ssd_w_out", "ret_w_in", "ret_w_out", "hgrn_w_in", "hgrn_w_out",
       "gdn_w_in", "gdn_w_out")
COL_SHARDED = ("mlp_w1", "ssd_w_in", "ret_w_in", "hgrn_w_in", "gdn_w_in")
SMALL = (("c_ctx", None), ("ada_b", None), ("norm_g", 2), ("final_g", None), ("ssd_conv_w", 2), ("ssd_conv_b", None),
         ("ssd_dt_bias", None), ("ssd_a_log", None), ("ssd_d", None), ("ssd_norm_g", None), ("ret_log_decay", None),
         ("hgrn_lb_logits", None), ("hgrn_norm_g", 1), ("gdn_conv_w", 2), ("gdn_dt_bias", None), ("gdn_a_log", None),
         ("gdn_norm_g", None))
WEIGHTS = ("c_ctx", "ada_w", "ada_b", "norm_g", "mlp_w1", "mlp_w2", "final_g", "ssd_w_in", "ssd_conv_w", "ssd_conv_b",
           "ssd_dt_bias", "ssd_a_log", "ssd_d", "ssd_norm_g", "ssd_w_out", "ret_w_in", "ret_log_decay", "ret_w_out",
           "hgrn_w_in", "hgrn_lb_logits", "hgrn_norm_g", "hgrn_w_out", "gdn_w_in", "gdn_conv_w", "gdn_dt_bias",
           "gdn_a_log", "gdn_norm_g", "gdn_w_out")


def _dot(a, b, prec=None):
    return lax.dot_general(a, b, (((1,), (0,)), ((), ())), precision=prec, preferred_element_type=f32)


def _dot_nt(a, b, prec=None):
    return lax.dot_general(a, b, (((1,), (1,)), ((), ())), precision=prec, preferred_element_type=f32)


def _dot_tn(a, b, prec=None):
    return lax.dot_general(a, b, (((0,), (0,)), ((), ())), precision=prec, preferred_element_type=f32)


def _iota(shape, d):
    return lax.broadcasted_iota(jnp.int32, shape, d)


def _log2(n):
    assert n & (n - 1) == 0
    return n.bit_length() - 1


def _pick(n, cands):
    for t in cands:
        if n % t == 0:
            return t
    return n


def _mm(a, b, mode, out_dtype=f32):
    if mode == "nn":
        (m, k), n = a.shape, b.shape[1]
    elif mode == "nt":
        (m, k), n = a.shape, b.shape[0]
    else:
        (k, m), n = a.shape, b.shape[1]
    tm = _pick(m, (1024, 768, 512, 256, 128, 64, 32, 16, 8))
    tn = _pick(n, (1024, 512, 256, 128))
    tk = _pick(k, (768, 512, 256, 128, 64, 32, 16, 8))
    nk = k // tk

    def body(a_ref, b_ref, o_ref, acc_ref):
        kk = pl.program_id(2)

        @pl.when(kk == 0)
        def _():
            acc_ref[...] = jnp.zeros_like(acc_ref)

        x, y = a_ref[...].astype(bf16), b_ref[...].astype(bf16)
        if mode == "nn":
            acc_ref[...] += _dot(x, y)
        elif mode == "nt":
            acc_ref[...] += _dot_nt(x, y)
        else:
            acc_ref[...] += _dot_tn(x, y)

        @pl.when(kk == nk - 1)
        def _():
            o_ref[...] = acc_ref[...].astype(out_dtype)

    if mode == "tn":
        a_spec = pl.BlockSpec((tk, tm), lambda i, j, kk: (kk, i))
    else:
        a_spec = pl.BlockSpec((tm, tk), lambda i, j, kk: (i, kk))
    if mode == "nt":
        b_spec = pl.BlockSpec((tn, tk), lambda i, j, kk: (j, kk))
    else:
        b_spec = pl.BlockSpec((tk, tn), lambda i, j, kk: (kk, j))
    return pl.pallas_call(
        body,
        grid=(m // tm, n // tn, nk),
        in_specs=[a_spec, b_spec],
        out_specs=pl.BlockSpec((tm, tn), lambda i, j, kk: (i, j)),
        out_shape=jax.ShapeDtypeStruct((m, n), out_dtype),
        scratch_shapes=[pltpu.VMEM((tm, tn), f32)],
        compiler_params=pltpu.CompilerParams(dimension_semantics=("parallel", "parallel", "arbitrary"),
                                             vmem_limit_bytes=VMEM_LIMIT),
        name=f"mm_{mode}_{m}x{k}x{n}_{jnp.dtype(out_dtype).name}",
    )(a, b)


@jax.custom_vjp
def pmm(a, w):
    return _mm(a, w, "nn")


def _pmm_fwd(a, w):
    return _mm(a, w, "nn"), (a, w)


def _pmm_bwd(res, g):
    a, w = res
    return _mm(g, w, "nt"), _mm(a, g, "tn", bf16)


pmm.defvjp(_pmm_fwd, _pmm_bwd)


def _sd_chunk(q, k, v, laq, lap, s, *, r, p):
    qc = CHUNK
    n = q.shape[1]
    tril = (_iota((qc, qc), 0) >= _iota((qc, qc), 1)).astype(f32)
    ones = jnp.ones((qc, qc), f32)
    rowx = _iota((qc, r * qc), 0)
    colx = _iota((qc, r * qc), 1) & (qc - 1)
    cumcol = _dot(tril, laq, HP)
    cumrow = _dot(ones, laq * (rowx <= colx).astype(f32), HP)
    decay = jnp.where(rowx >= colx, jnp.exp(jnp.minimum(cumcol - cumrow, 0.0)), 0.0)
    qb, kb, vb = q.astype(bf16), k.astype(bf16), v.astype(bf16)
    ktile = jnp.concatenate([kb] * r, axis=0) if r > 1 else kb
    attn = (_dot_nt(qb, ktile) * decay).astype(bf16)
    if r > 1:
        vt = jnp.concatenate([vb] * r, axis=0)
        same_head = (_iota((r * qc, r * p), 0) >> _log2(qc)) == (_iota((r * qc, r * p), 1) >> _log2(p))
        vt = jnp.where(same_head, vt, jnp.zeros_like(vt))
    else:
        vt = vb
    cum_p = _dot(tril, lap, HP)
    tot_q = _dot(ones, lap, HP)
    tot_n = _dot(jnp.ones((n, qc), f32), lap, HP)
    y = _dot(attn, vt) + jnp.exp(cum_p) * _dot(qb, s.astype(bf16))
    s_new = jnp.exp(tot_n) * s + _dot_tn(kb, (v * jnp.exp(tot_q - cum_p)).astype(bf16))
    return y, s_new


def _vd_sub(q, k, v, lf, st):
    c = SUB
    cc = c * c
    kd, vd = q.shape[1], v.shape[1]
    sh = _log2(c)
    rt = _iota((cc, c), 0) >> sh
    rs = _iota((cc, c), 0) & (c - 1)
    j = _iota((cc, c), 1)
    rep_t = rt == j
    rep_s = rs == j
    between = ((j > rs) & (j <= rt)).astype(f32)
    seg = _dot(between, lf, HP)
    valid = (_iota((cc, kd), 0) >> sh) >= (_iota((cc, kd), 0) & (c - 1))
    e = jnp.where(valid, jnp.exp(seg), 0.0)
    qb, kb, vb = q.astype(bf16), k.astype(bf16), v.astype(bf16)
    z = _dot(rep_t.astype(bf16), qb) * _dot(rep_s.astype(bf16), kb) * e
    w = jnp.sum(z, axis=1, keepdims=True) * rep_s.astype(f32)
    attn = _dot_tn(rep_t.astype(f32), w, HP)
    tril = (_iota((c, c), 0) >= _iota((c, c), 1)).astype(f32)
    cum = _dot(tril, lf, HP)
    tot = _dot(jnp.ones((c, c), f32), lf, HP)
    tot_v = _dot(jnp.ones((vd, c), f32), lf, HP)
    y = _dot(attn.astype(bf16), vb) + _dot_nt((q * jnp.exp(cum)).astype(bf16), st.astype(bf16))
    st_new = st * jnp.exp(tot_v) + _dot_tn(vb, (k * jnp.exp(tot - cum)).astype(bf16))
    return y, st_new


def _vd_chunk(q, k, v, lf, st):
    ys = []
    for i in range(CHUNK // SUB):
        rows = slice(i * SUB, (i + 1) * SUB)
        y_i, st = _vd_sub(q[rows], k[rows], v[rows], lf[rows], st)
        ys.append(y_i)
    return jnp.concatenate(ys, axis=0), st


def _dl_chunk(q, k, v, bq, laq, bk, lak, s):
    qc = CHUNK
    kd = q.shape[1]
    row, col = _iota((qc, qc), 0), _iota((qc, qc), 1)
    tril = (row >= col).astype(f32)
    ones = jnp.ones((qc, qc), f32)
    cumcol = _dot(tril, laq, HP)
    cumrow = _dot(ones, laq * (row <= col).astype(f32), HP)
    eseg = jnp.exp(jnp.minimum(cumcol - cumrow, 0.0))
    d_incl = jnp.where(row >= col, eseg, 0.0)
    d_strict = jnp.where(row > col, eseg, 0.0)
    qb, kb = q.astype(bf16), k.astype(bf16)
    kk = _dot_nt(kb, kb)
    qk = _dot_nt(qb, kb)
    cum_k = _dot(tril, lak, HP)
    tot = _dot(ones, lak, HP)
    tot_k = _dot(jnp.ones((kd, qc), f32), lak, HP)
    ecum = jnp.exp(cum_k)
    m = -(bq * kk * d_strict)
    tinv = (row == col).astype(f32) + m
    mp = m
    for _ in range(_log2(qc) - 1):
        mp = _dot(mp, mp, HP)
        tinv = tinv + _dot(tinv, mp, HP)
    u = _dot(tinv, v * bk, HP)
    w = _dot(tinv, k * bk * ecum, HP)
    sb = s.astype(bf16)
    v_new = u - _dot(w.astype(bf16), sb)
    y = _dot((qk * d_incl).astype(bf16), v_new.astype(bf16)) + ecum * _dot(qb, sb)
    s_new = jnp.exp(tot_k) * s + _dot_tn(kb, (v_new * jnp.exp(tot - cum_k)).astype(bf16))
    return y, s_new


def _scan_call(chunk, ins, maps, blocks, gshapes, gmaps, gblocks, state_shape, y_shape, y_block, y_map, units, nc, name,
               dy=None, sprev=None):
    n = len(ins)
    params = pltpu.CompilerParams(dimension_semantics=("parallel", "arbitrary"), vmem_limit_bytes=VMEM_LIMIT)
    state_block = (None, None) + state_shape
    if dy is None:
        def fwd_body(*refs):
            in_refs, y_ref, sp_ref, s_scr = refs[:n], refs[n], refs[n + 1], refs[n + 2]

            @pl.when(pl.program_id(1) == 0)
            def _():
                s_scr[...] = jnp.zeros_like(s_scr)

            s = s_scr[...]
            sp_ref[...] = s
            y, s_new = chunk(*[r[...] for r in in_refs], s)
            y_ref[...] = y
            s_scr[...] = s_new

        return pl.pallas_call(
            fwd_body, grid=(units, nc),
            in_specs=[pl.BlockSpec(b, m) for b, m in zip(blocks, maps)],
            out_specs=[pl.BlockSpec(y_block, y_map), pl.BlockSpec(state_block, lambda u, c: (u, c, 0, 0))],
            out_shape=[jax.ShapeDtypeStruct(y_shape, f32), jax.ShapeDtypeStruct((units, nc) + state_shape, f32)],
            scratch_shapes=[pltpu.VMEM(state_shape, f32)],
            compiler_params=params, name=name,
        )(*ins)

    def bwd_body(*refs):
        in_refs, sp_ref, dy_ref = refs[:n], refs[n], refs[n + 1]
        out_refs, ds_scr = refs[n + 2:2 * n + 2], refs[2 * n + 2]

        @pl.when(pl.program_id(1) == 0)
        def _():
            ds_scr[...] = jnp.zeros_like(ds_scr)

        _, vjp = jax.vjp(chunk, *[r[...] for r in in_refs], sp_ref[...])
        grads = vjp((dy_ref[...], ds_scr[...]))
        for r, g in zip(out_refs, grads[:n]):
            r[...] = g
        ds_scr[...] = grads[n]

    def rev(m):
        return lambda u, c: m(u, nc - 1 - c)

    return pl.pallas_call(
        bwd_body, grid=(units, nc),
        in_specs=[pl.BlockSpec(b, rev(m)) for b, m in zip(blocks, maps)]
        + [pl.BlockSpec(state_block, lambda u, c: (u, nc - 1 - c, 0, 0)), pl.BlockSpec(y_block, rev(y_map))],
        out_specs=[pl.BlockSpec(b, rev(m)) for b, m in zip(gblocks, gmaps)],
        out_shape=[jax.ShapeDtypeStruct(s, f32) for s in gshapes],
        scratch_shapes=[pltpu.VMEM(state_shape, f32)],
        compiler_params=params, name=name,
    )(*ins, sprev, dy)


def _sd_call(g, r, p, ins, dy=None, sprev=None):
    q, _, v, laq, _ = ins
    t = q.shape[1]
    n = q.shape[2] // g
    im = lambda u, c: (u // g, c, u % g)
    imx = lambda u, c: (u // g, u % g, c, 0)
    bq, bv, bx = (None, CHUNK, n), (None, CHUNK, r * p), (None, None, CHUNK, r * CHUNK)
    maps, blocks = [im, im, im, imx, im], [bq, bq, bv, bx, bv]
    return _scan_call(functools.partial(_sd_chunk, r=r, p=p), list(ins), maps, blocks, [a.shape for a in ins], maps, blocks,
                      (n, r * p), v.shape, bv, im, 2 * g, t // CHUNK,
                      f"sd_{'fwd' if dy is None else 'bwd'}_{n}_{r}_{p}_{t}", dy, sprev)


def _make_scan(call):
    @jax.custom_vjp
    def scan(*ins):
        return call(ins)[0]

    def fwd(*ins):
        y, sprev = call(ins)
        return y, (ins, sprev)

    def bwd(res, dy):
        ins, sprev = res
        return tuple(call(ins, dy, sprev))

    scan.defvjp(fwd, bwd)
    return scan


_ssd_scan = _make_scan(functools.partial(_sd_call, SSD_GROUPS, SSD_HEADS // SSD_GROUPS, SSD_HEAD_DIM))
_ret_scan = _make_scan(functools.partial(_sd_call, RET_HEADS, 1, RET_V))


def _vd_call(ins, dy=None, sprev=None):
    q = ins[0]
    t, h = q.shape[1], q.shape[2] // HGRN_EXPAND
    im = lambda u, c: (u // h, c, u % h)
    blk = (None, CHUNK, HGRN_EXPAND)
    return _scan_call(_vd_chunk, list(ins), [im] * 4, [blk] * 4, [q.shape] * 4, [im] * 4, [blk] * 4,
                      (HGRN_EXPAND, HGRN_EXPAND), q.shape, blk, im, 2 * h, t // CHUNK,
                      f"vd_{'fwd' if dy is None else 'bwd'}_{t}", dy, sprev)


_hgrn_scan = _make_scan(_vd_call)


def _dl_call(ins, dy=None, sprev=None):
    q, _, v, bq, _, _, _ = ins
    t = q.shape[1]
    hv = v.shape[2] // GDN_HEAD
    rr = hv // (q.shape[2] // GDN_HEAD)
    imq = lambda u, c: (u // hv, c, (u % hv) // rr)
    imv = lambda u, c: (u // hv, c, u % hv)
    imx = lambda u, c: (u // hv, u % hv, c, 0)
    blk, blx = (None, CHUNK, GDN_HEAD), (None, None, CHUNK, CHUNK)
    maps, blocks = [imq, imq, imv, imx, imx, imv, imv], [blk, blk, blk, blx, blx, blk, blk]
    gmaps = [imv, imv, imv, imx, imx, imv, imv]
    gshapes = [v.shape, v.shape, v.shape, bq.shape, bq.shape, v.shape, v.shape]
    return _scan_call(_dl_chunk, list(ins), maps, blocks, gshapes, gmaps, blocks, (GDN_HEAD, GDN_HEAD), v.shape, blk, imv,
                      2 * hv, t // CHUNK, f"dl_{'fwd' if dy is None else 'bwd'}_{t}", dy, sprev)


@jax.custom_vjp
def _gdn_scan(*ins):
    return _dl_call(ins)[0]


def _gdn_scan_fwd(*ins):
    y, sprev = _dl_call(ins)
    return y, (ins, sprev)


def _gdn_scan_bwd(res, dy):
    ins, sprev = res
    dq, dk, dv, dbq, dlaq, dbk, dlak = _dl_call(ins, dy, sprev)
    d2, t, _ = dq.shape
    rr = GDN_V_HEADS // GDN_K_HEADS
    fold = lambda a: a.reshape(d2, t, GDN_K_HEADS, rr, GDN_HEAD).sum(3).reshape(d2, t, GDN_DK)
    return fold(dq), fold(dk), dv, dbq, dlaq, dbk, dlak


_gdn_scan.defvjp(_gdn_scan_fwd, _gdn_scan_bwd)


def _place():
    x, y, c = lax.axis_index("x"), lax.axis_index("y"), lax.axis_index("c")
    chips = [(1 - x, y), (x, 1 - y), (1 - x, 1 - y)]
    return x, y, c, chips


def _small_all_gather(block):
    m_per, n = block.shape

    def body(x_ref, out_ref, send_sems, recv_sems, local_sem):
        x, y, c, chips = _place()
        me, sibling = (x, y, c), (x, y, 1 - c)

        def rows(px, py, pc):
            return out_ref.at[pl.ds((4 * px + 2 * py + pc) * m_per, m_per), :]

        def copy(k, blk, to, src=None):
            return pltpu.make_async_remote_copy(
                src_ref=rows(*blk) if src is None else src, dst_ref=rows(*blk),
                send_sem=send_sems.at[k], recv_sem=recv_sems.at[k], device_id=to, device_id_type=MESH)

        mine = pltpu.make_async_copy(x_ref, rows(*me), local_sem)
        mine.start()
        first = [copy(0, me, sibling, src=x_ref)]
        first += [copy(1 + j, me, (*chip, c), src=x_ref) for j, chip in enumerate(chips)]
        for cp in first:
            cp.start()
        passed = [copy(4 + j, (*chip, c), sibling) for j, chip in enumerate(chips)]
        for j, chip in enumerate(chips):
            copy(1 + j, (*chip, c), me).wait_recv()
            passed[j].start()
        copy(0, sibling, me).wait_recv()
        for j, chip in enumerate(chips):
            copy(4 + j, (*chip, 1 - c), me).wait_recv()
        for cp in first + passed:
            cp.wait_send()
        mine.wait()

    return pl.pallas_call(
        body,
        out_shape=jax.ShapeDtypeStruct((N_DEV * m_per, n), block.dtype),
        in_specs=[pl.BlockSpec(memory_space=pltpu.VMEM)],
        out_specs=pl.BlockSpec(memory_space=pltpu.VMEM),
        scratch_shapes=[pltpu.SemaphoreType.DMA((7,)), pltpu.SemaphoreType.DMA((7,)), pltpu.SemaphoreType.DMA],
        compiler_params=pltpu.CompilerParams(vmem_limit_bytes=VMEM_LIMIT),
        name=f"small_all_gather_{m_per}x{n}",
    )(block)


def _any_spec():
    return pl.BlockSpec(memory_space=pl.ANY)


def _weight_all_gather(pack):
    rows, width = pack.shape
    half = rows // 2

    def body(w_ref, out_ref, send_sems, recv_sems, local_sem):
        x, y, c, chips = _place()
        sibling = (x, y, 1 - c)

        def part(px, py, pc):
            return out_ref.at[2 * px + py, pl.ds(pl.multiple_of(pc * half, 16), half), :]

        def copy(k, blk, to, src=None):
            return pltpu.make_async_remote_copy(
                src_ref=part(*blk) if src is None else src, dst_ref=part(*blk),
                send_sem=send_sems.at[k], recv_sem=recv_sems.at[k], device_id=to, device_id_type=MESH)

        mine = pltpu.make_async_copy(w_ref, out_ref.at[2 * x + y], local_sem)
        mine.start()
        my_half = w_ref.at[pl.ds(pl.multiple_of(c * half, 16), half), :]
        first = [copy(j, (x, y, c), (*chip, c), src=my_half) for j, chip in enumerate(chips)]
        for cp in first:
            cp.start()
        passed = [copy(3 + j, (*chip, c), sibling) for j, chip in enumerate(chips)]
        for j, chip in enumerate(chips):
            copy(j, (*chip, c), (x, y, c)).wait_recv()
            passed[j].start()
        for j, chip in enumerate(chips):
            copy(3 + j, (*chip, 1 - c), (x, y, c)).wait_recv()
        for cp in first + passed:
            cp.wait_send()
        mine.wait()

    return pl.pallas_call(
        body,
        out_shape=jax.ShapeDtypeStruct((N_CHIPS, rows, width), pack.dtype),
        in_specs=[_any_spec()], out_specs=_any_spec(),
        scratch_shapes=[pltpu.SemaphoreType.DMA((6,)), pltpu.SemaphoreType.DMA((6,)), pltpu.SemaphoreType.DMA],
        name="weight_all_gather",
    )(pack)


def _sibling_swap(send):
    def body(s_ref, out_ref, send_sem, recv_sem):
        x, y, c, _ = _place()
        cp = pltpu.make_async_remote_copy(src_ref=s_ref, dst_ref=out_ref, send_sem=send_sem, recv_sem=recv_sem,
                                          device_id=(x, y, 1 - c), device_id_type=MESH)
        cp.start()
        cp.wait()

    return pl.pallas_call(
        body, out_shape=jax.ShapeDtypeStruct(send.shape, send.dtype),
        in_specs=[_any_spec()], out_specs=_any_spec(),
        scratch_shapes=[pltpu.SemaphoreType.DMA, pltpu.SemaphoreType.DMA],
        name=f"sibling_swap_{'x'.join(map(str, send.shape))}_{jnp.dtype(send.dtype).name}",
    )(send)


def _chip_exchange(parts):
    _, rows, width = parts.shape

    def body(p_ref, out_ref, send_sems, recv_sems):
        x, y, c, chips = _place()
        copies = [pltpu.make_async_remote_copy(
            src_ref=p_ref.at[2 * px + py], dst_ref=out_ref.at[j], send_sem=send_sems.at[j], recv_sem=recv_sems.at[j],
            device_id=(px, py, c), device_id_type=MESH) for j, (px, py) in enumerate(chips)]
        for cp in copies:
            cp.start()
        for cp in copies:
            cp.wait()

    return pl.pallas_call(
        body, out_shape=jax.ShapeDtypeStruct((3, rows, width), parts.dtype),
        in_specs=[_any_spec()], out_specs=_any_spec(),
        scratch_shapes=[pltpu.SemaphoreType.DMA((3,)), pltpu.SemaphoreType.DMA((3,))],
        name="chip_exchange",
    )(parts)


def _row_tile(rows, width, n_arrays):
    budget = VMEM_LIMIT // (4 * 2 * n_arrays * width * 2)
    return _pick(rows, tuple(t for t in (1024, 512, 256, 128, 64, 32, 16, 8) if t <= max(budget, 8)))


def _add_n(arrays, out_dtype):
    rows, width = arrays[0].shape
    n = len(arrays)
    tr = _row_tile(rows, width, n + 1)

    def body(*refs):
        acc = refs[0][...].astype(f32)
        for r in refs[1:n]:
            acc = acc + r[...].astype(f32)
        refs[n][...] = acc.astype(out_dtype)

    spec = pl.BlockSpec((tr, width), lambda i: (i, 0))
    return pl.pallas_call(
        body, grid=(rows // tr,), in_specs=[spec] * n, out_specs=spec,
        out_shape=jax.ShapeDtypeStruct((rows, width), out_dtype),
        compiler_params=pltpu.CompilerParams(dimension_semantics=("parallel",), vmem_limit_bytes=VMEM_LIMIT),
        name=f"add{n}_{rows}x{width}_{jnp.dtype(out_dtype).name}",
    )(*arrays)


def _adamw(w, g, m, v):
    rows, width = w.shape
    tr = _row_tile(rows, width, 7)
    c1 = 1.0 / (1.0 - ADAM_B1 ** ADAM_STEP)
    c2 = 1.0 / (1.0 - ADAM_B2 ** ADAM_STEP)

    def body(w_ref, g_ref, m_ref, v_ref, d_ref, mo_ref, vo_ref):
        gg = g_ref[...]
        m_new = ADAM_B1 * m_ref[...] + (1.0 - ADAM_B1) * gg
        v_new = ADAM_B2 * v_ref[...] + (1.0 - ADAM_B2) * (gg * gg)
        d_ref[...] = -ADAM_LR * ((m_new * c1) / (jnp.sqrt(v_new * c2) + ADAM_EPS) + ADAM_WD * w_ref[...])
        mo_ref[...] = m_new
        vo_ref[...] = v_new

    spec = pl.BlockSpec((tr, width), lambda i: (i, 0))
    sds = jax.ShapeDtypeStruct((rows, width), f32)
    return pl.pallas_call(
        body, grid=(rows // tr,), in_specs=[spec] * 4, out_specs=[spec] * 3, out_shape=[sds] * 3,
        compiler_params=pltpu.CompilerParams(dimension_semantics=("parallel",), vmem_limit_bytes=VMEM_LIMIT),
        name=f"adamw_{rows}x{width}",
    )(w, g, m, v)


def _rmsnorm(x, g):
    return x * lax.rsqrt(jnp.mean(x * x, axis=-1, keepdims=True) + NORM_EPS) * g


def _rev(a, lc):
    return jnp.concatenate([jnp.flip(a[:lc], 0), jnp.flip(a[lc:], 0)], axis=0)


def _both(a, lc):
    return jnp.stack([a, _rev(a, lc)])


def _both2(a_f, a_b, lc):
    return jnp.stack([a_f, _rev(a_b, lc)])


def _merge(y, lc):
    return y[0] + _rev(y[1], lc)


def _conv(u, w, lc):
    t = u.shape[0]
    pos = jnp.arange(t)[:, None]
    zero = jnp.zeros((1, u.shape[1]), u.dtype)
    prev = jnp.where((pos == 0) | (pos == lc), 0.0, jnp.concatenate([zero, u[:-1]], axis=0))
    nxt = jnp.where((pos == lc - 1) | (pos == t - 1), 0.0, jnp.concatenate([u[1:], zero], axis=0))
    return w[0] * prev + w[1] * u + w[2] * nxt


def _lanes_q(a):
    return jnp.broadcast_to(a.T[:, :, None], (a.shape[1], a.shape[0], CHUNK))


def _ssd_mixer(h, lc, start, w_main, w_dt, conv_w, conv_b, dt_bias, a_log, d_skip, norm_g, w_out):
    t = h.shape[0]
    r = SSD_HEADS // SSD_GROUPS
    u = pmm(h, w_main)
    dt = pmm(h, w_dt)
    z, xbc = u[:, :SSD_D_INNER], u[:, SSD_D_INNER:]
    xbc = jax.nn.silu(_conv(xbc, conv_w, lc) + conv_b)
    xs = xbc[:, :SSD_D_INNER]
    bm = xbc[:, SSD_D_INNER:SSD_D_INNER + SSD_GROUPS * SSD_STATE]
    cm = xbc[:, SSD_D_INNER + SSD_GROUPS * SSD_STATE:]
    dt = jax.nn.softplus(dt.reshape(t, 2, SSD_HEADS) + dt_bias)
    log_a = -jnp.exp(a_log) * dt
    per_lane = lambda a: jnp.repeat(a, SSD_HEAD_DIM, axis=1)
    v = _both2(xs * per_lane(dt[:, 0]), xs * per_lane(dt[:, 1]), lc)
    lap = _both2(per_lane(log_a[:, 0]), per_lane(log_a[:, 1]), lc)
    laq = lap.reshape(2, t, SSD_GROUPS, r * CHUNK).transpose(0, 2, 1, 3)
    y = _merge(_ssd_scan(_both(cm, lc), _both(bm, lc), v, laq, lap), lc)
    y = y + jnp.repeat(d_skip, SSD_HEAD_DIM) * xs
    y = y[start:] * jax.nn.silu(z[start:])
    n = t - start
    y = _rmsnorm(y.reshape(n, SSD_GROUPS, -1), norm_g.reshape(SSD_GROUPS, -1)).reshape(n, SSD_D_INNER)
    return pmm(y, w_out)


def _rope(a, rows):
    pos = jnp.arange(rows * GRID_W)
    row = (pos // GRID_W).astype(f32)
    col = (pos % GRID_W).astype(f32)
    half = a.shape[-1] // 2
    inv_freq = ROPE_BASE ** (-jnp.arange(0, half, 2, dtype=f32) / half)

    def rot(u, p):
        ang = p[:, None] * inv_freq
        cos, sin = jnp.cos(ang)[:, None, :], jnp.sin(ang)[:, None, :]
        u1, u2 = jnp.split(u, 2, axis=-1)
        return jnp.concatenate([u1 * cos - u2 * sin, u2 * cos + u1 * sin], axis=-1)

    return jnp.concatenate([rot(a[..., :half], row), rot(a[..., half:], col)], axis=-1)


def _ret_mixer(h, lc, start, w_in, log_decay, w_out):
    t = h.shape[0]
    u = pmm(h, w_in)
    q = u[:, :D_MODEL].reshape(t, RET_HEADS, RET_QK)
    k = u[:, D_MODEL:2 * D_MODEL].reshape(t, RET_HEADS, RET_QK) * RET_QK ** -0.5
    v = u[:, 2 * D_MODEL:2 * D_MODEL + RET_DV]
    gate = u[:, 2 * D_MODEL + RET_DV:]
    rows = (t - lc) // GRID_W
    q = jnp.concatenate([q[:lc], _rope(q[lc:], rows)], axis=0).reshape(t, D_MODEL)
    k = jnp.concatenate([k[:lc], _rope(k[lc:], rows)], axis=0).reshape(t, D_MODEL)
    laq = jnp.broadcast_to(log_decay[:, :, None, None], (2, RET_HEADS, t, CHUNK))
    lap = jnp.broadcast_to(jnp.repeat(log_decay, RET_V, axis=1)[:, None, :], (2, t, RET_DV))
    y = _merge(_ret_scan(_both(q, lc), _both(k, lc), _both(v, lc), laq, lap), lc)
    n = t - start
    y = y[start:].reshape(n, RET_HEADS, RET_V)
    mu = jnp.mean(y, axis=-1, keepdims=True)
    var = jnp.mean(jnp.square(y - mu), axis=-1, keepdims=True)
    y = ((y - mu) * lax.rsqrt(var + NORM_EPS)).reshape(n, RET_DV) * jax.nn.silu(gate[start:])
    return pmm(y, w_out)


def _hgrn_mixer(h, lc, start, w_in, lb, norm_g, w_out):
    t = h.shape[0]
    u = pmm(h, w_in)
    q, f_f, f_b, inp, gate = (u[:, i * D_MODEL:(i + 1) * D_MODEL] for i in range(5))

    def gates(f):
        log_f = jnp.logaddexp(jnp.log(lb), jnp.log1p(-lb) + jax.nn.log_sigmoid(f))
        return log_f, (1 - lb) * jax.nn.sigmoid(-f)

    lf_f, k_f = gates(f_f)
    lf_b, k_b = gates(f_b)
    y = _merge(_hgrn_scan(_both(q, lc), _both2(k_f, k_b, lc), _both(inp, lc), _both2(lf_f, lf_b, lc)), lc)
    n = t - start
    y = _rmsnorm(y[start:].reshape(n, HGRN_HEADS, HGRN_EXPAND), norm_g.reshape(HGRN_HEADS, HGRN_EXPAND))
    y = y.reshape(n, D_MODEL) * jax.nn.silu(gate[start:])
    return pmm(y, w_out)


def _l2norm(a):
    return a * lax.rsqrt(jnp.sum(a * a, axis=-1, keepdims=True) + 1e-6)


def _gdn_mixer(h, lc, start, w_main, w_gate, conv_w, dt_bias, a_log, norm_g, w_out):
    t = h.shape[0]
    u = pmm(h, w_main)
    ba = pmm(h, w_gate)
    qkv = jax.nn.silu(_conv(u[:, :GDN_CONV_CH], conv_w, lc))
    z = u[:, GDN_CONV_CH:]
    q = (_l2norm(qkv[:, :GDN_DK].reshape(t, GDN_K_HEADS, GDN_HEAD)) * GDN_HEAD ** -0.5).reshape(t, GDN_DK)
    k = _l2norm(qkv[:, GDN_DK:2 * GDN_DK].reshape(t, GDN_K_HEADS, GDN_HEAD)).reshape(t, GDN_DK)
    v = qkv[:, 2 * GDN_DK:]
    beta = jax.nn.sigmoid(ba[:, :2 * GDN_V_HEADS].reshape(t, 2, GDN_V_HEADS))
    log_a = -jnp.exp(a_log) * jax.nn.softplus(ba[:, 2 * GDN_V_HEADS:].reshape(t, 2, GDN_V_HEADS) + dt_bias)
    per_lane = lambda a: jnp.repeat(a, GDN_HEAD, axis=1)
    bq = jnp.stack([_lanes_q(beta[:, 0]), _lanes_q(_rev(beta[:, 1], lc))])
    laq = jnp.stack([_lanes_q(log_a[:, 0]), _lanes_q(_rev(log_a[:, 1], lc))])
    bk = _both2(per_lane(beta[:, 0]), per_lane(beta[:, 1]), lc)
    lak = _both2(per_lane(log_a[:, 0]), per_lane(log_a[:, 1]), lc)
    y = _merge(_gdn_scan(_both(q, lc), _both(k, lc), _both(v, lc), bq, laq, bk, lak), lc)
    n = t - start
    y = _rmsnorm(y[start:].reshape(n, GDN_V_HEADS, GDN_HEAD), norm_g) * jax.nn.silu(
        z[start:].reshape(n, GDN_V_HEADS, GDN_HEAD))
    return pmm(y.reshape(n, GDN_DV), w_out)


def _local_loss(x, mod, big, small, ctx, target):
    lc = ctx.shape[0]
    tok = jnp.concatenate([ctx, x], axis=0)
    t = tok.shape[0]
    is_ctx = (jnp.arange(t) < lc)[:, None]
    p = jax.nn.softmax(small["hgrn_lb_logits"], axis=0)
    for i in range(DEPTH):
        last = i == DEPTH - 1
        start = lc if last else 0
        sh1, sc1, g1, sh2, sc2, g2 = (jnp.where(is_ctx, m[1], m[0]) for m in jnp.split(mod[i], 6, axis=-1))
        h = _rmsnorm(tok, small["norm_g"][i, 0]) * (1 + sc1) + sh1
        if i == 0:
            y = _ssd_mixer(h, lc, start, big["ssd_w_main"], big["ssd_w_dt"], small["ssd_conv_w"][0], small["ssd_conv_b"][0],
                           small["ssd_dt_bias"][0], small["ssd_a_log"][0], small["ssd_d"][0], small["ssd_norm_g"][0],
                           big["ssd_w_out"])
        elif i == 1:
            y = _ret_mixer(h, lc, start, big["ret_w_in"], small["ret_log_decay"][0], big["ret_w_out"])
        elif i == 2:
            lb = jnp.cumsum(p, axis=0)[i] - p[0]
            y = _hgrn_mixer(h, lc, start, big["hgrn_w_in"], lb, small["hgrn_norm_g"][0], big["hgrn_w_out"])
        else:
            y = _gdn_mixer(h, lc, start, big["gdn_w_main"], big["gdn_w_gate"], small["gdn_conv_w"][0],
                           small["gdn_dt_bias"][0], small["gdn_a_log"][0], small["gdn_norm_g"][0], big["gdn_w_out"])
        if last:
            tok, sc2, sh2, g1, g2 = tok[lc:], sc2[lc:], sh2[lc:], g1[lc:], g2[lc:]
        tok = tok + g1 * y
        h = _rmsnorm(tok, small["norm_g"][i, 1]) * (1 + sc2) + sh2
        hid = jnp.square(jax.nn.relu(pmm(h, big["mlp_w1"][i])))
        tok = tok + g2 * pmm(hid, big["mlp_w2"][i])
    out = _rmsnorm(tok, small["final_g"])
    return 0.5 * jnp.sum(jnp.mean(jnp.square(out - target), axis=-1))


def _rows_of(a, width):
    return a.reshape(-1, width)


def _pad_rows(flat, width, mult=8):
    rows = -(-flat.shape[0] // width)
    rows = -(-rows // mult) * mult
    return jnp.pad(flat, (0, rows * width - flat.shape[0])).reshape(rows, width)


def _unpack_weights(gathered, shard_shapes):
    full, r0 = {}, 0
    for name in BIG:
        shp = shard_shapes[name]
        rows = math.prod(shp) // PACK_W
        blk = gathered[:, r0:r0 + rows].reshape((N_CHIPS,) + shp)
        r0 += rows
        if name in COL_SHARDED:
            full[name] = jnp.moveaxis(blk, 0, 2).reshape(shp[0], shp[1], N_CHIPS * shp[2])
        else:
            full[name] = jnp.moveaxis(blk, 0, 1).reshape(shp[0], N_CHIPS * shp[1], shp[2])
    return full


def _pack_grads(grads, shard_shapes):
    parts = []
    for name in BIG:
        shp = shard_shapes[name]
        g = grads[name]
        if name in COL_SHARDED:
            blk = jnp.moveaxis(g.reshape(shp[0], shp[1], N_CHIPS, shp[2]), 2, 0)
        else:
            blk = jnp.moveaxis(g.reshape(shp[0], N_CHIPS, shp[1], shp[2]), 1, 0)
        parts.append(blk.reshape(N_CHIPS, -1, PACK_W))
    return jnp.concatenate(parts, axis=1)


def _silu_grad(a):
    s = jax.nn.sigmoid(a)
    return s * (1 + a * (1 - s))


def kernel(x, c, ctx, c_ctx, ada_w, ada_b, norm_g, mlp_w1, mlp_w2, final_g, ssd_w_in, ssd_conv_w, ssd_conv_b, ssd_dt_bias, ssd_a_log, ssd_d, ssd_norm_g, ssd_w_out, ret_w_in, ret_log_decay, ret_w_out, hgrn_w_in, hgrn_lb_logits, hgrn_norm_g, hgrn_w_out, gdn_w_in, gdn_conv_w, gdn_dt_bias, gdn_a_log, gdn_norm_g, gdn_w_out, loss_target, m_c_ctx, m_ada_w, m_ada_b, m_norm_g, m_mlp_w1, m_mlp_w2, m_final_g, m_ssd_w_in, m_ssd_conv_w, m_ssd_conv_b, m_ssd_dt_bias, m_ssd_a_log, m_ssd_d, m_ssd_norm_g, m_ssd_w_out, m_ret_w_in, m_ret_log_decay, m_ret_w_out, m_hgrn_w_in, m_hgrn_lb_logits, m_hgrn_norm_g, m_hgrn_w_out, m_gdn_w_in, m_gdn_conv_w, m_gdn_dt_bias, m_gdn_a_log, m_gdn_norm_g, m_gdn_w_out, v_c_ctx, v_ada_w, v_ada_b, v_norm_g, v_mlp_w1, v_mlp_w2, v_final_g, v_ssd_w_in, v_ssd_conv_w, v_ssd_conv_b, v_ssd_dt_bias, v_ssd_a_log, v_ssd_d, v_ssd_norm_g, v_ssd_w_out, v_ret_w_in, v_ret_log_decay, v_ret_w_out, v_hgrn_w_in, v_hgrn_lb_logits, v_hgrn_norm_g, v_hgrn_w_out, v_gdn_w_in, v_gdn_conv_w, v_gdn_dt_bias, v_gdn_a_log, v_gdn_norm_g, v_gdn_w_out):
    env = dict(locals())
    w_loc = {n: env[n] for n in WEIGHTS}
    m_loc = {n: env["m_" + n] for n in WEIGHTS}
    v_loc = {n: env["v_" + n] for n in WEIGHTS}
    chip = 2 * lax.axis_index("x") + lax.axis_index("y")
    core = lax.axis_index("c")
    dev = 2 * chip + core
    d = D_MODEL

    sharded_small = [n for n, ax in SMALL if ax is not None]
    flat1 = jnp.concatenate([c.reshape(-1)] + [w_loc[n].reshape(-1) for n in sharded_small])
    g1 = _small_all_gather(_pad_rows(flat1, SMALL_W)).reshape(N_DEV, -1)
    c_all = g1[:, :d]
    small, off = {n: w_loc[n] for n, ax in SMALL if ax is None}, d
    for n in sharded_small:
        ax = dict(SMALL)[n]
        size = w_loc[n].size
        pieces = [g1[2 * k, off:off + size].reshape(w_loc[n].shape) for k in range(N_CHIPS)]
        small[n] = jnp.concatenate(pieces, axis=ax)
        off += size

    n_sh = ada_w.shape[2]
    cond_in = jnp.concatenate([c_all, c_ctx[None]], axis=0)
    cond = jnp.pad(jax.nn.silu(cond_in), ((0, 16 - N_DEV - 1), (0, 0)))
    ada_b_sh = lax.dynamic_slice_in_dim(ada_b, chip * n_sh, n_sh, axis=1)
    mod_sh = jnp.stack([_mm(cond, ada_w[i], "nn") + ada_b_sh[i] for i in range(DEPTH)])
    g2 = _small_all_gather(mod_sh.reshape(-1, SMALL_W)).reshape(N_DEV, DEPTH, 16, n_sh)
    mod_all = jnp.concatenate([g2[2 * k] for k in range(N_CHIPS)], axis=-1)
    mod_loc = jnp.stack([lax.dynamic_index_in_dim(mod_all, dev, axis=1, keepdims=False), mod_all[:, N_DEV]], axis=1)

    shard_shapes = {n: w_loc[n].shape for n in BIG}
    pack = jnp.concatenate([_rows_of(w_loc[n].astype(bf16), PACK_W) for n in BIG], axis=0)
    full = _unpack_weights(_weight_all_gather(pack), shard_shapes)
    big = {"mlp_w1": tuple(full["mlp_w1"][i] for i in range(DEPTH)),
           "mlp_w2": tuple(full["mlp_w2"][i] for i in range(DEPTH)),
           "ssd_w_main": full["ssd_w_in"][0, :, :SSD_MAIN], "ssd_w_dt": full["ssd_w_in"][0, :, SSD_MAIN:],
           "ssd_w_out": full["ssd_w_out"][0], "ret_w_in": full["ret_w_in"][0], "ret_w_out": full["ret_w_out"][0],
           "hgrn_w_in": full["hgrn_w_in"][0], "hgrn_w_out": full["hgrn_w_out"][0],
           "gdn_w_main": full["gdn_w_in"][0, :, :GDN_MAIN], "gdn_w_gate": full["gdn_w_in"][0, :, GDN_MAIN:],
           "gdn_w_out": full["gdn_w_out"][0]}

    small_diff = {n: small[n] for n, _ in SMALL if n not in ("c_ctx", "ada_b")}
    loss_loc, (gx, gmod, gbig, gsmall) = jax.value_and_grad(_local_loss, argnums=(0, 1, 2, 3))(
        x[0], mod_loc, big, small_diff, ctx[0], loss_target[0])
    loss = lax.psum(loss_loc, ("x", "y", "c"))

    small_names = [n for n, _ in SMALL if n not in ("c_ctx", "ada_b")]
    flat3 = jnp.concatenate([gmod.reshape(-1)] + [gsmall[n].reshape(-1) for n in small_names])
    g3 = _small_all_gather(_pad_rows(flat3, SMALL_W)).reshape(N_DEV, -1)
    gmod_all = g3[:, :gmod.size].reshape(N_DEV, DEPTH, 2, 6 * d)
    ctx_row = gmod_all[0, :, 1]
    for b in range(1, N_DEV):
        ctx_row = ctx_row + gmod_all[b, :, 1]
    dmod = jnp.concatenate([jnp.moveaxis(gmod_all[:, :, 0], 0, 1), ctx_row[:, None],
                            jnp.zeros((DEPTH, 16 - N_DEV - 1, 6 * d), f32)], axis=1)
    grad_small, off = {}, gmod.size
    for n in small_names:
        size = small[n].size
        tot = g3[0, off:off + size]
        for b in range(1, N_DEV):
            tot = tot + g3[b, off:off + size]
        grad_small[n] = tot.reshape(small[n].shape)
        off += size
    grad_small["ada_b"] = jnp.sum(dmod, axis=1)
    dmod_sh = lax.dynamic_slice_in_dim(dmod, chip * n_sh, n_sh, axis=2)
    grad_ada_w = jnp.stack([_mm(cond, dmod_sh[i], "tn") for i in range(DEPTH)])
    dcond = _mm(dmod_sh[0], ada_w[0], "nt")
    for i in range(1, DEPTH):
        dcond = dcond + _mm(dmod_sh[i], ada_w[i], "nt")
    g4 = _small_all_gather(_pad_rows(dcond[N_DEV], SMALL_W)).reshape(N_DEV, -1)[:, :d]
    dcond_ctx = g4[0] + g4[2] + g4[4] + g4[6]
    grad_small["c_ctx"] = dcond_ctx * _silu_grad(c_ctx)
    for n, ax in SMALL:
        if ax is not None:
            width = w_loc[n].shape[ax]
            grad_small[n] = lax.dynamic_slice_in_dim(grad_small[n], chip * width, width, axis=ax)

    gfull = {"mlp_w1": jnp.stack(gbig["mlp_w1"]), "mlp_w2": jnp.stack(gbig["mlp_w2"]),
             "ssd_w_in": jnp.concatenate([gbig["ssd_w_main"], gbig["ssd_w_dt"]], axis=1)[None],
             "ssd_w_out": gbig["ssd_w_out"][None], "ret_w_in": gbig["ret_w_in"][None], "ret_w_out": gbig["ret_w_out"][None],
             "hgrn_w_in": gbig["hgrn_w_in"][None], "hgrn_w_out": gbig["hgrn_w_out"][None],
             "gdn_w_in": jnp.concatenate([gbig["gdn_w_main"], gbig["gdn_w_gate"]], axis=1)[None],
             "gdn_w_out": gbig["gdn_w_out"][None]}
    gp = _pack_grads(gfull, shard_shapes)
    rows = gp.shape[1]
    half = rows // 2
    gp = gp.reshape(N_CHIPS, 2, half, PACK_W)
    mine = lax.dynamic_index_in_dim(gp, core, axis=1, keepdims=False)
    theirs = lax.dynamic_index_in_dim(gp, 1 - core, axis=1, keepdims=False)
    got = _sibling_swap(theirs)
    pair = _add_n([mine.reshape(-1, PACK_W), got.reshape(-1, PACK_W)], bf16).reshape(N_CHIPS, half, PACK_W)
    landed = _chip_exchange(pair)
    own = lax.dynamic_index_in_dim(pair, chip, axis=0, keepdims=False)
    red_half = _add_n([own, landed[0], landed[1], landed[2]], f32)
    other_half = _sibling_swap(red_half)
    red = jnp.where(core == 0, jnp.concatenate([red_half, other_half], axis=0),
                    jnp.concatenate([other_half, red_half], axis=0))
    grad_big, r0 = {}, 0
    for n in BIG:
        nrows = w_loc[n].size // PACK_W
        grad_big[n] = red[r0:r0 + nrows].reshape(w_loc[n].shape)
        r0 += nrows

    grads = dict(grad_big)
    grads.update(grad_small)
    grads["ada_w"] = grad_ada_w
    delta, new_m, new_v = {}, {}, {}
    for n in BIG + ("ada_w",):
        width = w_loc[n].shape[-1]
        outs = _adamw(*[a.reshape(-1, width) for a in (w_loc[n], grads[n], m_loc[n], v_loc[n])])
        delta[n], new_m[n], new_v[n] = (o.reshape(w_loc[n].shape) for o in outs)
    names = [n for n, _ in SMALL]
    packs = [_pad_rows(jnp.concatenate([t[n].reshape(-1) for n in names]), SMALL_W) for t in (w_loc, grads, m_loc, v_loc)]
    outs = _adamw(*packs)
    off = 0
    for n in names:
        size = w_loc[n].size
        delta[n], new_m[n], new_v[n] = (o.reshape(-1)[off:off + size].reshape(w_loc[n].shape) for o in outs)
        off += size

    return (loss, gx[None], *[grads[n] for n in WEIGHTS], *[delta[n] for n in WEIGHTS],
            *[new_m[n] for n in WEIGHTS], *[new_v[n] for n in WEIGHTS])
```

```python
import functools
import math

import jax
import jax.numpy as jnp
from jax import lax
from jax.experimental import pallas as pl
from jax.experimental.pallas import tpu as pltpu

f32 = jnp.float32
bf16 = jnp.bfloat16
HP = lax.Precision.HIGHEST
MESH = pl.DeviceIdType.MESH

D_MODEL = 2048
DEPTH = 4
GRID_W = 64
CHUNK = 64
SUB = 16
HEADS_PER_STEP = 4
NORM_EPS = 1e-6
ROPE_BASE = 10000.0
N_CHIPS = 4
N_DEV = 8
LANES = 128
PACK_W = 2048
SMALL_W = 1024
VMEM_LIMIT = 48 * 1024 * 1024

SSD_D_INNER, SSD_HEADS, SSD_GROUPS, SSD_STATE, SSD_HEAD_DIM = 4096, 64, 8, 128, 64
SSD_CONV_CH = SSD_D_INNER + 2 * SSD_GROUPS * SSD_STATE
SSD_MAIN = SSD_D_INNER + SSD_CONV_CH
RET_HEADS, RET_QK, RET_V, RET_DV = 8, 256, 512, 4096
HGRN_HEADS, HGRN_EXPAND = 16, 128
GDN_K_HEADS, GDN_V_HEADS, GDN_HEAD = 16, 32, 128
GDN_DK, GDN_DV = 2048, 4096
GDN_CONV_CH = 2 * GDN_DK + GDN_DV
GDN_MAIN = GDN_CONV_CH + GDN_DV

ADAM_LR, ADAM_B1, ADAM_B2, ADAM_EPS, ADAM_WD, ADAM_STEP = 0.001, 0.9, 0.999, 1e-08, 0.01, 10

BIG = ("mlp_w1", "mlp_w2", "ssd_w_in", "ssd_w_out", "ret_w_in", "ret_w_out", "hgrn_w_in", "hgrn_w_out",
       "gdn_w_in", "gdn_w_out")
COL_SHARDED = ("mlp_w1", "ssd_w_in", "ret_w_in", "hgrn_w_in", "gdn_w_in")
SMALL = (("c_ctx", None), ("ada_b", None), ("norm_g", 2), ("final_g", None), ("ssd_conv_w", 2), ("ssd_conv_b", None),
         ("ssd_dt_bias", None), ("ssd_a_log", None), ("ssd_d", None), ("ssd_norm_g", None), ("ret_log_decay", None),
         ("hgrn_lb_logits", None), ("hgrn_norm_g", 1), ("gdn_conv_w", 2), ("gdn_dt_bias", None), ("gdn_a_log", None),
         ("gdn_norm_g", None))
WEIGHTS = ("c_ctx", "ada_w", "ada_b", "norm_g", "mlp_w1", "mlp_w2", "final_g", "ssd_w_in", "ssd_conv_w", "ssd_conv_b",
           "ssd_dt_bias", "ssd_a_log", "ssd_d", "ssd_norm_g", "ssd_w_out", "ret_w_in", "ret_log_decay", "ret_w_out",
           "hgrn_w_in", "hgrn_lb_logits", "hgrn_norm_g", "hgrn_w_out", "gdn_w_in", "gdn_conv_w", "gdn_dt_bias",
           "gdn_a_log", "gdn_norm_g", "gdn_w_out")


def _dot(a, b, prec=None):
    return lax.dot_general(a, b, (((1,), (0,)), ((), ())), precision=prec, preferred_element_type=f32)


def _dot_nt(a, b, prec=None):
    return lax.dot_general(a, b, (((1,), (1,)), ((), ())), precision=prec, preferred_element_type=f32)


def _dot_tn(a, b, prec=None):
    return lax.dot_general(a, b, (((0,), (0,)), ((), ())), precision=prec, preferred_element_type=f32)


def _iota(shape, d):
    return lax.broadcasted_iota(jnp.int32, shape, d)


def _log2(n):
    assert n & (n - 1) == 0
    return n.bit_length() - 1


def _pick(n, cands):
    for t in cands:
        if n % t == 0:
            return t
    return n


def _mm(a, b, mode, out_dtype=f32):
    if mode == "nn":
        (m, k), n = a.shape, b.shape[1]
    elif mode == "nt":
        (m, k), n = a.shape, b.shape[0]
    else:
        (k, m), n = a.shape, b.shape[1]
    tm = _pick(m, (1024, 768, 512, 256, 128, 64, 32, 16, 8))
    tn = _pick(n, (1024, 512, 256, 128))
    tk = _pick(k, (768, 512, 256, 128, 64, 32, 16, 8))
    nk = k // tk

    def body(a_ref, b_ref, o_ref, acc_ref):
        kk = pl.program_id(2)

        @pl.when(kk == 0)
        def _():
            acc_ref[...] = jnp.zeros_like(acc_ref)

        x, y = a_ref[...].astype(bf16), b_ref[...].astype(bf16)
        if mode == "nn":
            acc_ref[...] += _dot(x, y)
        elif mode == "nt":
            acc_ref[...] += _dot_nt(x, y)
        else:
            acc_ref[...] += _dot_tn(x, y)

        @pl.when(kk == nk - 1)
        def _():
            o_ref[...] = acc_ref[...].astype(out_dtype)

    if mode == "tn":
        a_spec = pl.BlockSpec((tk, tm), lambda i, j, kk: (kk, i))
    else:
        a_spec = pl.BlockSpec((tm, tk), lambda i, j, kk: (i, kk))
    if mode == "nt":
        b_spec = pl.BlockSpec((tn, tk), lambda i, j, kk: (j, kk))
    else:
        b_spec = pl.BlockSpec((tk, tn), lambda i, j, kk: (kk, j))
    return pl.pallas_call(
        body,
        grid=(m // tm, n // tn, nk),
        in_specs=[a_spec, b_spec],
        out_specs=pl.BlockSpec((tm, tn), lambda i, j, kk: (i, j)),
        out_shape=jax.ShapeDtypeStruct((m, n), out_dtype),
        scratch_shapes=[pltpu.VMEM((tm, tn), f32)],
        compiler_params=pltpu.CompilerParams(dimension_semantics=("parallel", "parallel", "arbitrary"),
                                             vmem_limit_bytes=VMEM_LIMIT),
        name=f"mm_{mode}_{m}x{k}x{n}_{jnp.dtype(out_dtype).name}",
    )(a, b)


@jax.custom_vjp
def pmm(a, w):
    return _mm(a, w, "nn")


def _pmm_fwd(a, w):
    return _mm(a, w, "nn"), (a, w)


def _pmm_bwd(res, g):
    a, w = res
    return _mm(g, w, "nt"), _mm(a, g, "tn", bf16)


pmm.defvjp(_pmm_fwd, _pmm_bwd)


def _eff(i, n, rev):
    return (n - 1 - i) if rev else i


def _sd_chunk(q, k, v, laq, lap, s, *, r, p, rev):
    qc = CHUNK
    n = q.shape[1]
    tril = (_eff(_iota((qc, qc), 0), qc, rev) >= _eff(_iota((qc, qc), 1), qc, rev)).astype(f32)
    ones = jnp.ones((qc, qc), f32)
    rowx = _eff(_iota((qc, r * qc), 0), qc, rev)
    colx = _eff(_iota((qc, r * qc), 1) & (qc - 1), qc, rev)
    cumcol = _dot(tril, laq, HP)
    cumrow = _dot(ones, laq * (rowx <= colx).astype(f32), HP)
    decay = jnp.where(rowx >= colx, jnp.exp(jnp.minimum(cumcol - cumrow, 0.0)), 0.0)
    qb, kb, vb = q.astype(bf16), k.astype(bf16), v.astype(bf16)
    ktile = jnp.concatenate([kb] * r, axis=0) if r > 1 else kb
    attn = (_dot_nt(qb, ktile) * decay).astype(bf16)
    if r > 1:
        vt = jnp.concatenate([vb] * r, axis=0)
        same_head = (_iota((r * qc, r * p), 0) >> _log2(qc)) == (_iota((r * qc, r * p), 1) >> _log2(p))
        vt = jnp.where(same_head, vt, jnp.zeros_like(vt))
    else:
        vt = vb
    cum_p = _dot(tril, lap, HP)
    tot_q = _dot(ones, lap, HP)
    tot_n = _dot(jnp.ones((n, qc), f32), lap, HP)
    y = _dot(attn, vt) + jnp.exp(cum_p) * _dot(qb, s.astype(bf16))
    s_new = jnp.exp(tot_n) * s + _dot_tn(kb, (v * jnp.exp(tot_q - cum_p)).astype(bf16))
    return y, s_new


def _vd_sub(q, k, v, lf, st, rev):
    c = SUB
    cc = c * c
    kd, vd = q.shape[1], v.shape[1]
    sh = _log2(c)
    rt = _eff(_iota((cc, c), 0) >> sh, c, rev)
    rs = _eff(_iota((cc, c), 0) & (c - 1), c, rev)
    j = _eff(_iota((cc, c), 1), c, rev)
    rep_t = rt == j
    rep_s = rs == j
    between = ((j > rs) & (j <= rt)).astype(f32)
    seg = _dot(between, lf, HP)
    valid = _eff(_iota((cc, kd), 0) >> sh, c, rev) >= _eff(_iota((cc, kd), 0) & (c - 1), c, rev)
    e = jnp.where(valid, jnp.exp(seg), 0.0)
    qb, kb, vb = q.astype(bf16), k.astype(bf16), v.astype(bf16)
    z = _dot(rep_t.astype(bf16), qb) * _dot(rep_s.astype(bf16), kb) * e
    w = jnp.sum(z, axis=1, keepdims=True) * rep_s.astype(f32)
    attn = _dot_tn(rep_t.astype(f32), w, HP)
    tril = (_eff(_iota((c, c), 0), c, rev) >= _eff(_iota((c, c), 1), c, rev)).astype(f32)
    cum = _dot(tril, lf, HP)
    tot = _dot(jnp.ones((c, c), f32), lf, HP)
    tot_v = _dot(jnp.ones((vd, c), f32), lf, HP)
    y = _dot(attn.astype(bf16), vb) + _dot_nt((q * jnp.exp(cum)).astype(bf16), st.astype(bf16))
    st_new = st * jnp.exp(tot_v) + _dot_tn(vb, (k * jnp.exp(tot - cum)).astype(bf16))
    return y, st_new


def _vd_chunk(q, k, v, lf, st, *, rev, hb):
    n_sub = CHUNK // SUB
    order = range(n_sub - 1, -1, -1) if rev else range(n_sub)
    ys, sts = [], []
    for h in range(hb):
        lanes = slice(h * HGRN_EXPAND, (h + 1) * HGRN_EXPAND)
        st_h, parts = st[h], [None] * n_sub
        for i in order:
            rows = slice(i * SUB, (i + 1) * SUB)
            parts[i], st_h = _vd_sub(q[rows, lanes], k[rows, lanes], v[rows, lanes], lf[rows, lanes], st_h, rev)
        ys.append(jnp.concatenate(parts, axis=0))
        sts.append(st_h)
    return jnp.concatenate(ys, axis=1), jnp.stack(sts)


def _dl_chunk(q, k, v, bq, laq, bk, lak, s, *, rev, hb, rr):
    qc = CHUNK
    kd = GDN_HEAD
    row, col = _eff(_iota((qc, qc), 0), qc, rev), _eff(_iota((qc, qc), 1), qc, rev)
    tril = (row >= col).astype(f32)
    ones = jnp.ones((qc, qc), f32)
    ones_k = jnp.ones((kd, qc), f32)
    eye = (row == col).astype(f32)
    ys, ss = [], []
    for g in range(hb // rr):
        glanes = slice(g * kd, (g + 1) * kd)
        kf = k[:, glanes]
        qb, kb = q[:, glanes].astype(bf16), kf.astype(bf16)
        kk = _dot_nt(kb, kb)
        qk = _dot_nt(qb, kb)
        for j in range(g * rr, (g + 1) * rr):
            lanes = slice(j * kd, (j + 1) * kd)
            cumcol = _dot(tril, laq[j], HP)
            cumrow = _dot(ones, laq[j] * (row <= col).astype(f32), HP)
            eseg = jnp.exp(jnp.minimum(cumcol - cumrow, 0.0))
            d_incl = jnp.where(row >= col, eseg, 0.0)
            d_strict = jnp.where(row > col, eseg, 0.0)
            cum_k = _dot(tril, lak[:, lanes], HP)
            tot = _dot(ones, lak[:, lanes], HP)
            tot_k = _dot(ones_k, lak[:, lanes], HP)
            ecum = jnp.exp(cum_k)
            m = -(bq[j] * kk * d_strict)
            tinv = eye + m
            mp = m
            for _ in range(_log2(qc) - 1):
                mp = _dot(mp, mp, HP)
                tinv = tinv + _dot(tinv, mp, HP)
            u = _dot(tinv, v[:, lanes] * bk[:, lanes], HP)
            w = _dot(tinv, kf * bk[:, lanes] * ecum, HP)
            sb = s[j].astype(bf16)
            v_new = u - _dot(w.astype(bf16), sb)
            ys.append(_dot((qk * d_incl).astype(bf16), v_new.astype(bf16)) + ecum * _dot(qb, sb))
            ss.append(jnp.exp(tot_k) * s[j] + _dot_tn(kb, (v_new * jnp.exp(tot - cum_k)).astype(bf16)))
    return jnp.concatenate(ys, axis=1), jnp.stack(ss)


def _scan_call(chunk, ins, maps, blocks, state_shape, y_shape, y_block, y_map, units, nc, ncx, rev, name,
               dy=None, sprev=None):
    n = len(ins)
    params = pltpu.CompilerParams(dimension_semantics=("parallel", "arbitrary"), vmem_limit_bytes=VMEM_LIMIT)
    state_block = (None, None) + state_shape
    zeros = (0,) * len(state_shape)

    def chunk_at(pos):
        return jnp.where(pos < ncx, ncx - 1 - pos, nc + ncx - 1 - pos) if rev else pos

    def at(m, pos_of):
        return lambda u, c: m(u, chunk_at(pos_of(c)))

    if dy is None:
        def fwd_body(*refs):
            in_refs, y_ref, sp_ref, s_scr = refs[:n], refs[n], refs[n + 1], refs[n + 2]

            @pl.when(pl.program_id(1) == 0)
            def _():
                s_scr[...] = jnp.zeros_like(s_scr)

            s = s_scr[...]
            sp_ref[...] = s
            y, s_new = chunk(*[r[...] for r in in_refs], s)
            y_ref[...] = y
            s_scr[...] = s_new

        same = lambda c: c
        return pl.pallas_call(
            fwd_body, grid=(units, nc),
            in_specs=[pl.BlockSpec(b, at(m, same)) for b, m in zip(blocks, maps)],
            out_specs=[pl.BlockSpec(y_block, at(y_map, same)), pl.BlockSpec(state_block, lambda u, c: (u, c) + zeros)],
            out_shape=[jax.ShapeDtypeStruct(y_shape, f32), jax.ShapeDtypeStruct((units, nc) + state_shape, f32)],
            scratch_shapes=[pltpu.VMEM(state_shape, f32)],
            compiler_params=params, name=name,
        )(*ins)

    def bwd_body(*refs):
        in_refs, sp_ref, dy_ref = refs[:n], refs[n], refs[n + 1]
        out_refs, ds_scr = refs[n + 2:2 * n + 2], refs[2 * n + 2]

        @pl.when(pl.program_id(1) == 0)
        def _():
            ds_scr[...] = jnp.zeros_like(ds_scr)

        _, vjp = jax.vjp(chunk, *[r[...] for r in in_refs], sp_ref[...])
        grads = vjp((dy_ref[...], ds_scr[...]))
        for r, g in zip(out_refs, grads[:n]):
            r[...] = g
        ds_scr[...] = grads[n]

    back = lambda c: nc - 1 - c
    return pl.pallas_call(
        bwd_body, grid=(units, nc),
        in_specs=[pl.BlockSpec(b, at(m, back)) for b, m in zip(blocks, maps)]
        + [pl.BlockSpec(state_block, lambda u, c: (u, nc - 1 - c) + zeros), pl.BlockSpec(y_block, at(y_map, back))],
        out_specs=[pl.BlockSpec(b, at(m, back)) for b, m in zip(blocks, maps)],
        out_shape=[jax.ShapeDtypeStruct(a.shape, f32) for a in ins],
        scratch_shapes=[pltpu.VMEM(state_shape, f32)],
        compiler_params=params, name=name,
    )(*ins, sprev, dy)


def _scan_name(kind, rev, t, dy):
    return f"{kind}_{'bwd' if dy is not None else 'fwd'}_{'rev' if rev else 'fore'}_{t}"


def _sd_call(g, r, p, rev, ncx, ins, dy=None, sprev=None):
    q, _, v, _, _ = ins
    t = q.shape[0]
    n = q.shape[1] // g
    im = lambda u, c: (c, u)
    imx = lambda u, c: (u, c, 0)
    bq, bv, bx = (CHUNK, n), (CHUNK, r * p), (None, CHUNK, r * CHUNK)
    return _scan_call(functools.partial(_sd_chunk, r=r, p=p, rev=rev), list(ins), [im, im, im, imx, im], [bq, bq, bv, bx, bv],
                      (n, r * p), v.shape, bv, im, g, t // CHUNK, ncx, rev, _scan_name(f"sd{n}x{r}x{p}", rev, t, dy), dy, sprev)


def _vd_call(rev, ncx, ins, dy=None, sprev=None):
    q = ins[0]
    t, h = q.shape[0], q.shape[1] // HGRN_EXPAND
    hb = HEADS_PER_STEP
    im = lambda u, c: (c, u)
    blk = (CHUNK, hb * HGRN_EXPAND)
    return _scan_call(functools.partial(_vd_chunk, rev=rev, hb=hb), list(ins), [im] * 4, [blk] * 4,
                      (hb, HGRN_EXPAND, HGRN_EXPAND), q.shape, blk, im, h // hb, t // CHUNK, ncx, rev,
                      _scan_name("vd", rev, t, dy), dy, sprev)


def _dl_call(rev, ncx, ins, dy=None, sprev=None):
    q, _, v, _, _, _, _ = ins
    t = q.shape[0]
    hv = v.shape[1] // GDN_HEAD
    rr = hv // (q.shape[1] // GDN_HEAD)
    hb = HEADS_PER_STEP
    im = lambda u, c: (c, u)
    imx = lambda u, c: (u, c, 0)
    bqk, bv, bx = (CHUNK, hb // rr * GDN_HEAD), (CHUNK, hb * GDN_HEAD), (hb, CHUNK, CHUNK)
    return _scan_call(functools.partial(_dl_chunk, rev=rev, hb=hb, rr=rr), list(ins), [im, im, im, imx, imx, im, im],
                      [bqk, bqk, bv, bx, bx, bv, bv], (hb, GDN_HEAD, GDN_HEAD), v.shape, bv, im, hv // hb, t // CHUNK, ncx, rev,
                      _scan_name("dl", rev, t, dy), dy, sprev)


_SCAN_CALLS = {"ssd": functools.partial(_sd_call, SSD_GROUPS, SSD_HEADS // SSD_GROUPS, SSD_HEAD_DIM),
               "ret": functools.partial(_sd_call, RET_HEADS, 1, RET_V), "hgrn": _vd_call, "gdn": _dl_call}
_SCANS = {}


def _scan(kind, rev, lc):
    key = (kind, rev, lc)
    if key not in _SCANS:
        call = functools.partial(_SCAN_CALLS[kind], rev, lc // CHUNK)

        @jax.custom_vjp
        def scan(*ins):
            return call(ins)[0]

        def fwd(*ins):
            y, sprev = call(ins)
            return y, (ins, sprev)

        def bwd(res, dy):
            ins, sprev = res
            return tuple(call(ins, dy, sprev))

        scan.defvjp(fwd, bwd)
        _SCANS[key] = scan
    return _SCANS[key]


def _place():
    x, y, c = lax.axis_index("x"), lax.axis_index("y"), lax.axis_index("c")
    chips = [(1 - x, y), (x, 1 - y), (1 - x, 1 - y)]
    return x, y, c, chips


def _small_all_gather(block):
    m_per, n = block.shape

    def body(x_ref, out_ref, send_sems, recv_sems, local_sem):
        x, y, c, chips = _place()
        me, sibling = (x, y, c), (x, y, 1 - c)

        def rows(px, py, pc):
            return out_ref.at[pl.ds((4 * px + 2 * py + pc) * m_per, m_per), :]

        def copy(k, blk, to, src=None):
            return pltpu.make_async_remote_copy(
                src_ref=rows(*blk) if src is None else src, dst_ref=rows(*blk),
                send_sem=send_sems.at[k], recv_sem=recv_sems.at[k], device_id=to, device_id_type=MESH)

        mine = pltpu.make_async_copy(x_ref, rows(*me), local_sem)
        mine.start()
        first = [copy(0, me, sibling, src=x_ref)]
        first += [copy(1 + j, me, (*chip, c), src=x_ref) for j, chip in enumerate(chips)]
        for cp in first:
            cp.start()
        passed = [copy(4 + j, (*chip, c), sibling) for j, chip in enumerate(chips)]
        for j, chip in enumerate(chips):
            copy(1 + j, (*chip, c), me).wait_recv()
            passed[j].start()
        copy(0, sibling, me).wait_recv()
        for j, chip in enumerate(chips):
            copy(4 + j, (*chip, 1 - c), me).wait_recv()
        for cp in first + passed:
            cp.wait_send()
        mine.wait()

    return pl.pallas_call(
        body,
        out_shape=jax.ShapeDtypeStruct((N_DEV * m_per, n), block.dtype),
        in_specs=[pl.BlockSpec(memory_space=pltpu.VMEM)],
        out_specs=pl.BlockSpec(memory_space=pltpu.VMEM),
        scratch_shapes=[pltpu.SemaphoreType.DMA((7,)), pltpu.SemaphoreType.DMA((7,)), pltpu.SemaphoreType.DMA],
        compiler_params=pltpu.CompilerParams(vmem_limit_bytes=VMEM_LIMIT),
        name=f"small_all_gather_{m_per}x{n}",
    )(block)


def _any_spec():
    return pl.BlockSpec(memory_space=pl.ANY)


def _weight_all_gather(pack):
    rows, width = pack.shape
    half = rows // 2

    def body(w_ref, out_ref, send_sems, recv_sems, local_sem):
        x, y, c, chips = _place()
        sibling = (x, y, 1 - c)

        def part(px, py, pc):
            return out_ref.at[2 * px + py, pl.ds(pl.multiple_of(pc * half, 16), half), :]

        def copy(k, blk, to, src=None):
            return pltpu.make_async_remote_copy(
                src_ref=part(*blk) if src is None else src, dst_ref=part(*blk),
                send_sem=send_sems.at[k], recv_sem=recv_sems.at[k], device_id=to, device_id_type=MESH)

        mine = pltpu.make_async_copy(w_ref, out_ref.at[2 * x + y], local_sem)
        mine.start()
        my_half = w_ref.at[pl.ds(pl.multiple_of(c * half, 16), half), :]
        first = [copy(j, (x, y, c), (*chip, c), src=my_half) for j, chip in enumerate(chips)]
        for cp in first:
            cp.start()
        passed = [copy(3 + j, (*chip, c), sibling) for j, chip in enumerate(chips)]
        for j, chip in enumerate(chips):
            copy(j, (*chip, c), (x, y, c)).wait_recv()
            passed[j].start()
        for j, chip in enumerate(chips):
            copy(3 + j, (*chip, 1 - c), (x, y, c)).wait_recv()
        for cp in first + passed:
            cp.wait_send()
        mine.wait()

    return pl.pallas_call(
        body,
        out_shape=jax.ShapeDtypeStruct((N_CHIPS, rows, width), pack.dtype),
        in_specs=[_any_spec()], out_specs=_any_spec(),
        scratch_shapes=[pltpu.SemaphoreType.DMA((6,)), pltpu.SemaphoreType.DMA((6,)), pltpu.SemaphoreType.DMA],
        name="weight_all_gather",
    )(pack)


def _sibling_swap(send):
    def body(s_ref, out_ref, send_sem, recv_sem):
        x, y, c, _ = _place()
        cp = pltpu.make_async_remote_copy(src_ref=s_ref, dst_ref=out_ref, send_sem=send_sem, recv_sem=recv_sem,
                                          device_id=(x, y, 1 - c), device_id_type=MESH)
        cp.start()
        cp.wait()

    return pl.pallas_call(
        body, out_shape=jax.ShapeDtypeStruct(send.shape, send.dtype),
        in_specs=[_any_spec()], out_specs=_any_spec(),
        scratch_shapes=[pltpu.SemaphoreType.DMA, pltpu.SemaphoreType.DMA],
        name=f"sibling_swap_{'x'.join(map(str, send.shape))}_{jnp.dtype(send.dtype).name}",
    )(send)


def _chip_exchange(parts):
    _, rows, width = parts.shape

    def body(p_ref, out_ref, send_sems, recv_sems):
        x, y, c, chips = _place()
        copies = [pltpu.make_async_remote_copy(
            src_ref=p_ref.at[2 * px + py], dst_ref=out_ref.at[j], send_sem=send_sems.at[j], recv_sem=recv_sems.at[j],
            device_id=(px, py, c), device_id_type=MESH) for j, (px, py) in enumerate(chips)]
        for cp in copies:
            cp.start()
        for cp in copies:
            cp.wait()

    return pl.pallas_call(
        body, out_shape=jax.ShapeDtypeStruct((3, rows, width), parts.dtype),
        in_specs=[_any_spec()], out_specs=_any_spec(),
        scratch_shapes=[pltpu.SemaphoreType.DMA((3,)), pltpu.SemaphoreType.DMA((3,))],
        name="chip_exchange",
    )(parts)


def _row_tile(rows, width, n_arrays):
    budget = VMEM_LIMIT // (4 * 2 * n_arrays * width * 2)
    return _pick(rows, tuple(t for t in (1024, 512, 256, 128, 64, 32, 16, 8) if t <= max(budget, 8)))


def _add_n(arrays, out_dtype):
    rows, width = arrays[0].shape
    n = len(arrays)
    tr = _row_tile(rows, width, n + 1)

    def body(*refs):
        acc = refs[0][...].astype(f32)
        for r in refs[1:n]:
            acc = acc + r[...].astype(f32)
        refs[n][...] = acc.astype(out_dtype)

    spec = pl.BlockSpec((tr, width), lambda i: (i, 0))
    return pl.pallas_call(
        body, grid=(rows // tr,), in_specs=[spec] * n, out_specs=spec,
        out_shape=jax.ShapeDtypeStruct((rows, width), out_dtype),
        compiler_params=pltpu.CompilerParams(dimension_semantics=("parallel",), vmem_limit_bytes=VMEM_LIMIT),
        name=f"add{n}_{rows}x{width}_{jnp.dtype(out_dtype).name}",
    )(*arrays)


def _adamw(w, g, m, v):
    rows, width = w.shape
    tr = _row_tile(rows, width, 7)
    c1 = 1.0 / (1.0 - ADAM_B1 ** ADAM_STEP)
    c2 = 1.0 / (1.0 - ADAM_B2 ** ADAM_STEP)

    def body(w_ref, g_ref, m_ref, v_ref, d_ref, mo_ref, vo_ref):
        gg = g_ref[...]
        m_new = ADAM_B1 * m_ref[...] + (1.0 - ADAM_B1) * gg
        v_new = ADAM_B2 * v_ref[...] + (1.0 - ADAM_B2) * (gg * gg)
        d_ref[...] = -ADAM_LR * ((m_new * c1) / (jnp.sqrt(v_new * c2) + ADAM_EPS) + ADAM_WD * w_ref[...])
        mo_ref[...] = m_new
        vo_ref[...] = v_new

    spec = pl.BlockSpec((tr, width), lambda i: (i, 0))
    sds = jax.ShapeDtypeStruct((rows, width), f32)
    return pl.pallas_call(
        body, grid=(rows // tr,), in_specs=[spec] * 4, out_specs=[spec] * 3, out_shape=[sds] * 3,
        compiler_params=pltpu.CompilerParams(dimension_semantics=("parallel",), vmem_limit_bytes=VMEM_LIMIT),
        name=f"adamw_{rows}x{width}",
    )(w, g, m, v)


def _rmsnorm(x, g):
    return x * lax.rsqrt(jnp.mean(x * x, axis=-1, keepdims=True) + NORM_EPS) * g


def _conv(u, w, lc):
    t = u.shape[0]
    pos = jnp.arange(t)[:, None]
    zero = jnp.zeros((1, u.shape[1]), u.dtype)
    prev = jnp.where((pos == 0) | (pos == lc), 0.0, jnp.concatenate([zero, u[:-1]], axis=0))
    nxt = jnp.where((pos == lc - 1) | (pos == t - 1), 0.0, jnp.concatenate([u[1:], zero], axis=0))
    return w[0] * prev + w[1] * u + w[2] * nxt


def _lanes_q(a):
    return jnp.broadcast_to(a.T[:, :, None], (a.shape[1], a.shape[0], CHUNK))


def _ssd_mixer(h, lc, start, w_main, w_dt, conv_w, conv_b, dt_bias, a_log, d_skip, norm_g, w_out):
    t = h.shape[0]
    r = SSD_HEADS // SSD_GROUPS
    u = pmm(h, w_main)
    dt = pmm(h, w_dt)
    z, xbc = u[:, :SSD_D_INNER], u[:, SSD_D_INNER:]
    xbc = jax.nn.silu(_conv(xbc, conv_w, lc) + conv_b)
    xs = xbc[:, :SSD_D_INNER]
    bm = xbc[:, SSD_D_INNER:SSD_D_INNER + SSD_GROUPS * SSD_STATE]
    cm = xbc[:, SSD_D_INNER + SSD_GROUPS * SSD_STATE:]
    dt = jax.nn.softplus(dt.reshape(t, 2, SSD_HEADS) + dt_bias)
    log_a = -jnp.exp(a_log) * dt
    per_lane = lambda a: jnp.repeat(a, SSD_HEAD_DIM, axis=1)
    y = jnp.repeat(d_skip, SSD_HEAD_DIM) * xs
    for d in (0, 1):
        lap = per_lane(log_a[:, d])
        laq = lap.reshape(t, SSD_GROUPS, r * CHUNK).transpose(1, 0, 2)
        y = y + _scan("ssd", d == 1, lc)(cm, bm, xs * per_lane(dt[:, d]), laq, lap)
    y = y[start:] * jax.nn.silu(z[start:])
    n = t - start
    y = _rmsnorm(y.reshape(n, SSD_GROUPS, -1), norm_g.reshape(SSD_GROUPS, -1)).reshape(n, SSD_D_INNER)
    return pmm(y, w_out)


def _rope(a, rows):
    pos = jnp.arange(rows * GRID_W)
    row = (pos // GRID_W).astype(f32)
    col = (pos % GRID_W).astype(f32)
    half = a.shape[-1] // 2
    inv_freq = ROPE_BASE ** (-jnp.arange(0, half, 2, dtype=f32) / half)

    def rot(u, p):
        ang = p[:, None] * inv_freq
        cos, sin = jnp.cos(ang)[:, None, :], jnp.sin(ang)[:, None, :]
        u1, u2 = jnp.split(u, 2, axis=-1)
        return jnp.concatenate([u1 * cos - u2 * sin, u2 * cos + u1 * sin], axis=-1)

    return jnp.concatenate([rot(a[..., :half], row), rot(a[..., half:], col)], axis=-1)


def _ret_mixer(h, lc, start, w_in, log_decay, w_out):
    t = h.shape[0]
    u = pmm(h, w_in)
    q = u[:, :D_MODEL].reshape(t, RET_HEADS, RET_QK)
    k = u[:, D_MODEL:2 * D_MODEL].reshape(t, RET_HEADS, RET_QK) * RET_QK ** -0.5
    v = u[:, 2 * D_MODEL:2 * D_MODEL + RET_DV]
    gate = u[:, 2 * D_MODEL + RET_DV:]
    rows = (t - lc) // GRID_W
    q = jnp.concatenate([q[:lc], _rope(q[lc:], rows)], axis=0).reshape(t, D_MODEL)
    k = jnp.concatenate([k[:lc], _rope(k[lc:], rows)], axis=0).reshape(t, D_MODEL)
    y = 0.0
    for d in (0, 1):
        laq = jnp.broadcast_to(log_decay[d][:, None, None], (RET_HEADS, t, CHUNK))
        lap = jnp.broadcast_to(jnp.repeat(log_decay[d], RET_V)[None, :], (t, RET_DV))
        y = y + _scan("ret", d == 1, lc)(q, k, v, laq, lap)
    n = t - start
    y = y[start:].reshape(n, RET_HEADS, RET_V)
    mu = jnp.mean(y, axis=-1, keepdims=True)
    var = jnp.mean(jnp.square(y - mu), axis=-1, keepdims=True)
    y = ((y - mu) * lax.rsqrt(var + NORM_EPS)).reshape(n, RET_DV) * jax.nn.silu(gate[start:])
    return pmm(y, w_out)


def _hgrn_mixer(h, lc, start, w_in, lb, norm_g, w_out):
    t = h.shape[0]
    u = pmm(h, w_in)
    q, f_f, f_b, inp, gate = (u[:, i * D_MODEL:(i + 1) * D_MODEL] for i in range(5))

    def gates(f):
        log_f = jnp.logaddexp(jnp.log(lb), jnp.log1p(-lb) + jax.nn.log_sigmoid(f))
        return log_f, (1 - lb) * jax.nn.sigmoid(-f)

    lf_f, k_f = gates(f_f)
    lf_b, k_b = gates(f_b)
    y = _scan("hgrn", False, lc)(q, k_f, inp, lf_f) + _scan("hgrn", True, lc)(q, k_b, inp, lf_b)
    n = t - start
    y = _rmsnorm(y[start:].reshape(n, HGRN_HEADS, HGRN_EXPAND), norm_g.reshape(HGRN_HEADS, HGRN_EXPAND))
    y = y.reshape(n, D_MODEL) * jax.nn.silu(gate[start:])
    return pmm(y, w_out)


def _l2norm(a):
    return a * lax.rsqrt(jnp.sum(a * a, axis=-1, keepdims=True) + 1e-6)


def _gdn_mixer(h, lc, start, w_main, w_gate, conv_w, dt_bias, a_log, norm_g, w_out):
    t = h.shape[0]
    u = pmm(h, w_main)
    ba = pmm(h, w_gate)
    qkv = jax.nn.silu(_conv(u[:, :GDN_CONV_CH], conv_w, lc))
    z = u[:, GDN_CONV_CH:]
    q = (_l2norm(qkv[:, :GDN_DK].reshape(t, GDN_K_HEADS, GDN_HEAD)) * GDN_HEAD ** -0.5).reshape(t, GDN_DK)
    k = _l2norm(qkv[:, GDN_DK:2 * GDN_DK].reshape(t, GDN_K_HEADS, GDN_HEAD)).reshape(t, GDN_DK)
    v = qkv[:, 2 * GDN_DK:]
    beta = jax.nn.sigmoid(ba[:, :2 * GDN_V_HEADS].reshape(t, 2, GDN_V_HEADS))
    log_a = -jnp.exp(a_log) * jax.nn.softplus(ba[:, 2 * GDN_V_HEADS:].reshape(t, 2, GDN_V_HEADS) + dt_bias)
    per_lane = lambda a: jnp.repeat(a, GDN_HEAD, axis=1)
    y = 0.0
    for d in (0, 1):
        y = y + _scan("gdn", d == 1, lc)(q, k, v, _lanes_q(beta[:, d]), _lanes_q(log_a[:, d]),
                                         per_lane(beta[:, d]), per_lane(log_a[:, d]))
    n = t - start
    y = _rmsnorm(y[start:].reshape(n, GDN_V_HEADS, GDN_HEAD), norm_g) * jax.nn.silu(
        z[start:].reshape(n, GDN_V_HEADS, GDN_HEAD))
    return pmm(y.reshape(n, GDN_DV), w_out)


def _local_loss(x, mod, big, small, ctx, target):
    lc = ctx.shape[0]
    tok = jnp.concatenate([ctx, x], axis=0)
    t = tok.shape[0]
    is_ctx = (jnp.arange(t) < lc)[:, None]
    p = jax.nn.softmax(small["hgrn_lb_logits"], axis=0)
    for i in range(DEPTH):
        last = i == DEPTH - 1
        start = lc if last else 0
        sh1, sc1, g1, sh2, sc2, g2 = (jnp.where(is_ctx, m[1], m[0]) for m in jnp.split(mod[i], 6, axis=-1))
        h = _rmsnorm(tok, small["norm_g"][i, 0]) * (1 + sc1) + sh1
        if i == 0:
            y = _ssd_mixer(h, lc, start, big["ssd_w_main"], big["ssd_w_dt"], small["ssd_conv_w"][0], small["ssd_conv_b"][0],
                           small["ssd_dt_bias"][0], small["ssd_a_log"][0], small["ssd_d"][0], small["ssd_norm_g"][0],
                           big["ssd_w_out"])
        elif i == 1:
            y = _ret_mixer(h, lc, start, big["ret_w_in"], small["ret_log_decay"][0], big["ret_w_out"])
        elif i == 2:
            lb = jnp.cumsum(p, axis=0)[i] - p[0]
            y = _hgrn_mixer(h, lc, start, big["hgrn_w_in"], lb, small["hgrn_norm_g"][0], big["hgrn_w_out"])
        else:
            y = _gdn_mixer(h, lc, start, big["gdn_w_main"], big["gdn_w_gate"], small["gdn_conv_w"][0],
                           small["gdn_dt_bias"][0], small["gdn_a_log"][0], small["gdn_norm_g"][0], big["gdn_w_out"])
        if last:
            tok, sc2, sh2, g1, g2 = tok[lc:], sc2[lc:], sh2[lc:], g1[lc:], g2[lc:]
        tok = tok + g1 * y
        h = _rmsnorm(tok, small["norm_g"][i, 1]) * (1 + sc2) + sh2
        hid = jnp.square(jax.nn.relu(pmm(h, big["mlp_w1"][i])))
        tok = tok + g2 * pmm(hid, big["mlp_w2"][i])
    out = _rmsnorm(tok, small["final_g"])
    return 0.5 * jnp.sum(jnp.mean(jnp.square(out - target), axis=-1))


def _rows_of(a, width):
    return a.reshape(-1, width)


def _pad_rows(flat, width, mult=8):
    rows = -(-flat.shape[0] // width)
    rows = -(-rows // mult) * mult
    return jnp.pad(flat, (0, rows * width - flat.shape[0])).reshape(rows, width)


def _unpack_weights(gathered, shard_shapes):
    full, r0 = {}, 0
    for name in BIG:
        shp = shard_shapes[name]
        rows = math.prod(shp) // PACK_W
        blk = gathered[:, r0:r0 + rows].reshape((N_CHIPS,) + shp)
        r0 += rows
        if name in COL_SHARDED:
            full[name] = jnp.moveaxis(blk, 0, 2).reshape(shp[0], shp[1], N_CHIPS * shp[2])
        else:
            full[name] = jnp.moveaxis(blk, 0, 1).reshape(shp[0], N_CHIPS * shp[1], shp[2])
    return full


def _pack_grads(grads, shard_shapes):
    parts = []
    for name in BIG:
        shp = shard_shapes[name]
        g = grads[name]
        if name in COL_SHARDED:
            blk = jnp.moveaxis(g.reshape(shp[0], shp[1], N_CHIPS, shp[2]), 2, 0)
        else:
            blk = jnp.moveaxis(g.reshape(shp[0], N_CHIPS, shp[1], shp[2]), 1, 0)
        parts.append(blk.reshape(N_CHIPS, -1, PACK_W))
    return jnp.concatenate(parts, axis=1)


def _silu_grad(a):
    s = jax.nn.sigmoid(a)
    return s * (1 + a * (1 - s))


def kernel(x, c, ctx, c_ctx, ada_w, ada_b, norm_g, mlp_w1, mlp_w2, final_g, ssd_w_in, ssd_conv_w, ssd_conv_b, ssd_dt_bias, ssd_a_log, ssd_d, ssd_norm_g, ssd_w_out, ret_w_in, ret_log_decay, ret_w_out, hgrn_w_in, hgrn_lb_logits, hgrn_norm_g, hgrn_w_out, gdn_w_in, gdn_conv_w, gdn_dt_bias, gdn_a_log, gdn_norm_g, gdn_w_out, loss_target, m_c_ctx, m_ada_w, m_ada_b, m_norm_g, m_mlp_w1, m_mlp_w2, m_final_g, m_ssd_w_in, m_ssd_conv_w, m_ssd_conv_b, m_ssd_dt_bias, m_ssd_a_log, m_ssd_d, m_ssd_norm_g, m_ssd_w_out, m_ret_w_in, m_ret_log_decay, m_ret_w_out, m_hgrn_w_in, m_hgrn_lb_logits, m_hgrn_norm_g, m_hgrn_w_out, m_gdn_w_in, m_gdn_conv_w, m_gdn_dt_bias, m_gdn_a_log, m_gdn_norm_g, m_gdn_w_out, v_c_ctx, v_ada_w, v_ada_b, v_norm_g, v_mlp_w1, v_mlp_w2, v_final_g, v_ssd_w_in, v_ssd_conv_w, v_ssd_conv_b, v_ssd_dt_bias, v_ssd_a_log, v_ssd_d, v_ssd_norm_g, v_ssd_w_out, v_ret_w_in, v_ret_log_decay, v_ret_w_out, v_hgrn_w_in, v_hgrn_lb_logits, v_hgrn_norm_g, v_hgrn_w_out, v_gdn_w_in, v_gdn_conv_w, v_gdn_dt_bias, v_gdn_a_log, v_gdn_norm_g, v_gdn_w_out):
    env = dict(locals())
    w_loc = {n: env[n] for n in WEIGHTS}
    m_loc = {n: env["m_" + n] for n in WEIGHTS}
    v_loc = {n: env["v_" + n] for n in WEIGHTS}
    chip = 2 * lax.axis_index("x") + lax.axis_index("y")
    core = lax.axis_index("c")
    dev = 2 * chip + core
    d = D_MODEL

    sharded_small = [n for n, ax in SMALL if ax is not None]
    flat1 = jnp.concatenate([c.reshape(-1)] + [w_loc[n].reshape(-1) for n in sharded_small])
    g1 = _small_all_gather(_pad_rows(flat1, SMALL_W)).reshape(N_DEV, -1)
    c_all = g1[:, :d]
    small, off = {n: w_loc[n] for n, ax in SMALL if ax is None}, d
    for n in sharded_small:
        ax = dict(SMALL)[n]
        size = w_loc[n].size
        pieces = [g1[2 * k, off:off + size].reshape(w_loc[n].shape) for k in range(N_CHIPS)]
        small[n] = jnp.concatenate(pieces, axis=ax)
        off += size

    n_sh = ada_w.shape[2]
    cond_in = jnp.concatenate([c_all, c_ctx[None]], axis=0)
    cond = jnp.pad(jax.nn.silu(cond_in), ((0, 16 - N_DEV - 1), (0, 0)))
    ada_b_sh = lax.dynamic_slice_in_dim(ada_b, chip * n_sh, n_sh, axis=1)
    mod_sh = jnp.stack([_mm(cond, ada_w[i], "nn") + ada_b_sh[i] for i in range(DEPTH)])
    g2 = _small_all_gather(mod_sh.reshape(-1, SMALL_W)).reshape(N_DEV, DEPTH, 16, n_sh)
    mod_all = jnp.concatenate([g2[2 * k] for k in range(N_CHIPS)], axis=-1)
    mod_loc = jnp.stack([lax.dynamic_index_in_dim(mod_all, dev, axis=1, keepdims=False), mod_all[:, N_DEV]], axis=1)

    shard_shapes = {n: w_loc[n].shape for n in BIG}
    pack = jnp.concatenate([_rows_of(w_loc[n].astype(bf16), PACK_W) for n in BIG], axis=0)
    full = _unpack_weights(_weight_all_gather(pack), shard_shapes)
    big = {"mlp_w1": tuple(full["mlp_w1"][i] for i in range(DEPTH)),
           "mlp_w2": tuple(full["mlp_w2"][i] for i in range(DEPTH)),
           "ssd_w_main": full["ssd_w_in"][0, :, :SSD_MAIN], "ssd_w_dt": full["ssd_w_in"][0, :, SSD_MAIN:],
           "ssd_w_out": full["ssd_w_out"][0], "ret_w_in": full["ret_w_in"][0], "ret_w_out": full["ret_w_out"][0],
           "hgrn_w_in": full["hgrn_w_in"][0], "hgrn_w_out": full["hgrn_w_out"][0],
           "gdn_w_main": full["gdn_w_in"][0, :, :GDN_MAIN], "gdn_w_gate": full["gdn_w_in"][0, :, GDN_MAIN:],
           "gdn_w_out": full["gdn_w_out"][0]}

    small_diff = {n: small[n] for n, _ in SMALL if n not in ("c_ctx", "ada_b")}
    loss_loc, (gx, gmod, gbig, gsmall) = jax.value_and_grad(_local_loss, argnums=(0, 1, 2, 3))(
        x[0], mod_loc, big, small_diff, ctx[0], loss_target[0])
    loss = lax.psum(loss_loc, ("x", "y", "c"))

    small_names = [n for n, _ in SMALL if n not in ("c_ctx", "ada_b")]
    flat3 = jnp.concatenate([gmod.reshape(-1)] + [gsmall[n].reshape(-1) for n in small_names])
    g3 = _small_all_gather(_pad_rows(flat3, SMALL_W)).reshape(N_DEV, -1)
    gmod_all = g3[:, :gmod.size].reshape(N_DEV, DEPTH, 2, 6 * d)
    ctx_row = gmod_all[0, :, 1]
    for b in range(1, N_DEV):
        ctx_row = ctx_row + gmod_all[b, :, 1]
    dmod = jnp.concatenate([jnp.moveaxis(gmod_all[:, :, 0], 0, 1), ctx_row[:, None],
                            jnp.zeros((DEPTH, 16 - N_DEV - 1, 6 * d), f32)], axis=1)
    grad_small, off = {}, gmod.size
    for n in small_names:
        size = small[n].size
        tot = g3[0, off:off + size]
        for b in range(1, N_DEV):
            tot = tot + g3[b, off:off + size]
        grad_small[n] = tot.reshape(small[n].shape)
        off += size
    grad_small["ada_b"] = jnp.sum(dmod, axis=1)
    dmod_sh = lax.dynamic_slice_in_dim(dmod, chip * n_sh, n_sh, axis=2)
    grad_ada_w = jnp.stack([_mm(cond, dmod_sh[i], "tn") for i in range(DEPTH)])
    dcond = _mm(dmod_sh[0], ada_w[0], "nt")
    for i in range(1, DEPTH):
        dcond = dcond + _mm(dmod_sh[i], ada_w[i], "nt")
    g4 = _small_all_gather(_pad_rows(dcond[N_DEV], SMALL_W)).reshape(N_DEV, -1)[:, :d]
    dcond_ctx = g4[0] + g4[2] + g4[4] + g4[6]
    grad_small["c_ctx"] = dcond_ctx * _silu_grad(c_ctx)
    for n, ax in SMALL:
        if ax is not None:
            width = w_loc[n].shape[ax]
            grad_small[n] = lax.dynamic_slice_in_dim(grad_small[n], chip * width, width, axis=ax)

    gfull = {"mlp_w1": jnp.stack(gbig["mlp_w1"]), "mlp_w2": jnp.stack(gbig["mlp_w2"]),
             "ssd_w_in": jnp.concatenate([gbig["ssd_w_main"], gbig["ssd_w_dt"]], axis=1)[None],
             "ssd_w_out": gbig["ssd_w_out"][None], "ret_w_in": gbig["ret_w_in"][None], "ret_w_out": gbig["ret_w_out"][None],
             "hgrn_w_in": gbig["hgrn_w_in"][None], "hgrn_w_out": gbig["hgrn_w_out"][None],
             "gdn_w_in": jnp.concatenate([gbig["gdn_w_main"], gbig["gdn_w_gate"]], axis=1)[None],
             "gdn_w_out": gbig["gdn_w_out"][None]}
    gp = _pack_grads(gfull, shard_shapes)
    rows = gp.shape[1]
    half = rows // 2
    gp = gp.reshape(N_CHIPS, 2, half, PACK_W)
    mine = lax.dynamic_index_in_dim(gp, core, axis=1, keepdims=False)
    theirs = lax.dynamic_index_in_dim(gp, 1 - core, axis=1, keepdims=False)
    got = _sibling_swap(theirs)
    pair = _add_n([mine.reshape(-1, PACK_W), got.reshape(-1, PACK_W)], bf16).reshape(N_CHIPS, half, PACK_W)
    landed = _chip_exchange(pair)
    own = lax.dynamic_index_in_dim(pair, chip, axis=0, keepdims=False)
    red_half = _add_n([own, landed[0], landed[1], landed[2]], f32)
    other_half = _sibling_swap(red_half)
    red = jnp.where(core == 0, jnp.concatenate([red_half, other_half], axis=0),
                    jnp.concatenate([other_half, red_half], axis=0))
    grad_big, r0 = {}, 0
    for n in BIG:
        nrows = w_loc[n].size // PACK_W
        grad_big[n] = red[r0:r0 + nrows].reshape(w_loc[n].shape)
        r0 += nrows

    grads = dict(grad_big)
    grads.update(grad_small)
    grads["ada_w"] = grad_ada_w
    delta, new_m, new_v = {}, {}, {}
    for n in BIG + ("ada_w",):
        width = w_loc[n].shape[-1]
        outs = _adamw(*[a.reshape(-1, width) for a in (w_loc[n], grads[n], m_loc[n], v_loc[n])])
        delta[n], new_m[n], new_v[n] = (o.reshape(w_loc[n].shape) for o in outs)
    names = [n for n, _ in SMALL]
    packs = [_pad_rows(jnp.concatenate([t[n].reshape(-1) for n in names]), SMALL_W) for t in (w_loc, grads, m_loc, v_loc)]
    outs = _adamw(*packs)
    off = 0
    for n in names:
        size = w_loc[n].size
        delta[n], new_m[n], new_v[n] = (o.reshape(-1)[off:off + size].reshape(w_loc[n].shape) for o in outs)
        off += size

    return (loss, gx[None], *[grads[n] for n in WEIGHTS], *[delta[n] for n in WEIGHTS],
            *[new_m[n] for n in WEIGHTS], *[new_v[n] for n in WEIGHTS])
```

```python
import functools
import math

import jax
import jax.numpy as jnp
from jax import lax
from jax.experimental import pallas as pl
from jax.experimental.pallas import tpu as pltpu

f32 = jnp.float32
bf16 = jnp.bfloat16
MESH = pl.DeviceIdType.MESH

D_MODEL = 2048
DEPTH = 4
GRID_W = 64
CHUNK = 64
SUB = 16
HEADS_PER_STEP = 4
NORM_EPS = 1e-6
ROPE_BASE = 10000.0
N_CHIPS = 4
N_DEV = 8
PACK_W = 2048
SMALL_W = 1024
VMEM_LIMIT = 48 * 1024 * 1024

SSD_D_INNER, SSD_HEADS, SSD_GROUPS, SSD_STATE, SSD_HEAD_DIM = 4096, 64, 8, 128, 64
SSD_CONV_CH = SSD_D_INNER + 2 * SSD_GROUPS * SSD_STATE
SSD_MAIN = SSD_D_INNER + SSD_CONV_CH
RET_HEADS, RET_QK, RET_V, RET_DV = 8, 256, 512, 4096
HGRN_HEADS, HGRN_EXPAND = 16, 128
GDN_K_HEADS, GDN_V_HEADS, GDN_HEAD = 16, 32, 128
GDN_DK, GDN_DV = 2048, 4096
GDN_CONV_CH = 2 * GDN_DK + GDN_DV
GDN_MAIN = GDN_CONV_CH + GDN_DV

ADAM_LR, ADAM_B1, ADAM_B2, ADAM_EPS, ADAM_WD, ADAM_STEP = 0.001, 0.9, 0.999, 1e-08, 0.01, 10

BIG = ("mlp_w1", "mlp_w2", "ssd_w_in", "ssd_w_out", "ret_w_in", "ret_w_out", "hgrn_w_in", "hgrn_w_out",
       "gdn_w_in", "gdn_w_out")
COL_SHARDED = ("mlp_w1", "ssd_w_in", "ret_w_in", "hgrn_w_in", "gdn_w_in")
SMALL = (("c_ctx", None), ("ada_b", None), ("norm_g", 2), ("final_g", None), ("ssd_conv_w", 2), ("ssd_conv_b", None),
         ("ssd_dt_bias", None), ("ssd_a_log", None), ("ssd_d", None), ("ssd_norm_g", None), ("ret_log_decay", None),
         ("hgrn_lb_logits", None), ("hgrn_norm_g", 1), ("gdn_conv_w", 2), ("gdn_dt_bias", None), ("gdn_a_log", None),
         ("gdn_norm_g", None))
WEIGHTS = ("c_ctx", "ada_w", "ada_b", "norm_g", "mlp_w1", "mlp_w2", "final_g", "ssd_w_in", "ssd_conv_w", "ssd_conv_b",
           "ssd_dt_bias", "ssd_a_log", "ssd_d", "ssd_norm_g", "ssd_w_out", "ret_w_in", "ret_log_decay", "ret_w_out",
           "hgrn_w_in", "hgrn_lb_logits", "hgrn_norm_g", "hgrn_w_out", "gdn_w_in", "gdn_conv_w", "gdn_dt_bias",
           "gdn_a_log", "gdn_norm_g", "gdn_w_out")


def _dot(a, b, prec=None):
    return lax.dot_general(a, b, (((1,), (0,)), ((), ())), precision=prec, preferred_element_type=f32)


def _dot_nt(a, b, prec=None):
    return lax.dot_general(a, b, (((1,), (1,)), ((), ())), precision=prec, preferred_element_type=f32)


def _dot_tn(a, b, prec=None):
    return lax.dot_general(a, b, (((0,), (0,)), ((), ())), precision=prec, preferred_element_type=f32)


def _split2(x):
    hi = x.astype(bf16)
    return hi, (x - hi.astype(f32)).astype(bf16)


def _split3(x):
    hi = x.astype(bf16)
    rest = x - hi.astype(f32)
    mid = rest.astype(bf16)
    return hi, mid, (rest - mid.astype(f32)).astype(bf16)


@jax.custom_vjp
def _dot_sel(mask, x):
    hi, mid, lo = _split3(x)
    return _dot(mask, hi) + _dot(mask, mid) + _dot(mask, lo)


def _dot_sel_fwd(mask, x):
    return _dot_sel(mask, x), mask


def _dot_sel_bwd(mask, g):
    hi, mid, lo = _split3(g)
    return jnp.zeros_like(mask), _dot_tn(mask, hi) + _dot_tn(mask, mid) + _dot_tn(mask, lo)


_dot_sel.defvjp(_dot_sel_fwd, _dot_sel_bwd)


def _dot3_with(dot, a, b):
    a_hi, a_lo = _split2(a)
    b_hi, b_lo = _split2(b)
    return dot(a_hi, b_hi) + dot(a_hi, b_lo) + dot(a_lo, b_hi)


@jax.custom_vjp
def _dot3(a, b):
    return _dot3_with(_dot, a, b)


def _dot3_fwd(a, b):
    return _dot3(a, b), (a, b)


def _dot3_bwd(res, g):
    a, b = res
    return _dot3_with(_dot_nt, g, b), _dot3_with(_dot_tn, a, g)


_dot3.defvjp(_dot3_fwd, _dot3_bwd)


def _iota(shape, d):
    return lax.broadcasted_iota(jnp.int32, shape, d)


def _log2(n):
    assert n & (n - 1) == 0
    return n.bit_length() - 1


def _pick(n, cands):
    for t in cands:
        if n % t == 0:
            return t
    return n


def _mm(a, b, mode, out_dtype=f32):
    if mode == "nn":
        (m, k), n = a.shape, b.shape[1]
    elif mode == "nt":
        (m, k), n = a.shape, b.shape[0]
    else:
        (k, m), n = a.shape, b.shape[1]
    tm = _pick(m, (1024, 768, 512, 256, 128, 64, 32, 16, 8))
    tn = _pick(n, (1024, 512, 256, 128))
    tk = _pick(k, (768, 512, 256, 128, 64, 32, 16, 8))
    nk = k // tk

    def body(a_ref, b_ref, o_ref, acc_ref):
        kk = pl.program_id(2)

        @pl.when(kk == 0)
        def _():
            acc_ref[...] = jnp.zeros_like(acc_ref)

        x, y = a_ref[...].astype(bf16), b_ref[...].astype(bf16)
        if mode == "nn":
            acc_ref[...] += _dot(x, y)
        elif mode == "nt":
            acc_ref[...] += _dot_nt(x, y)
        else:
            acc_ref[...] += _dot_tn(x, y)

        @pl.when(kk == nk - 1)
        def _():
            o_ref[...] = acc_ref[...].astype(out_dtype)

    if mode == "tn":
        a_spec = pl.BlockSpec((tk, tm), lambda i, j, kk: (kk, i))
    else:
        a_spec = pl.BlockSpec((tm, tk), lambda i, j, kk: (i, kk))
    if mode == "nt":
        b_spec = pl.BlockSpec((tn, tk), lambda i, j, kk: (j, kk))
    else:
        b_spec = pl.BlockSpec((tk, tn), lambda i, j, kk: (kk, j))
    return pl.pallas_call(
        body,
        grid=(m // tm, n // tn, nk),
        in_specs=[a_spec, b_spec],
        out_specs=pl.BlockSpec((tm, tn), lambda i, j, kk: (i, j)),
        out_shape=jax.ShapeDtypeStruct((m, n), out_dtype),
        scratch_shapes=[pltpu.VMEM((tm, tn), f32)],
        compiler_params=pltpu.CompilerParams(dimension_semantics=("parallel", "parallel", "arbitrary"),
                                             vmem_limit_bytes=VMEM_LIMIT),
        name=f"mm_{mode}_{m}x{k}x{n}_{jnp.dtype(out_dtype).name}",
    )(a, b)


@jax.custom_vjp
def pmm(a, w):
    return _mm(a, w, "nn")


def _pmm_fwd(a, w):
    return _mm(a, w, "nn"), (a, w)


def _pmm_bwd(res, g):
    a, w = res
    return _mm(g, w, "nt"), _mm(a, g, "tn", bf16)


pmm.defvjp(_pmm_fwd, _pmm_bwd)


def _eff(i, n, rev):
    return (n - 1 - i) if rev else i


def _sd_chunk(q, k, v, laq, lap, s, *, r, p, rev):
    qc = CHUNK
    n = q.shape[1]
    tril = (_eff(_iota((qc, qc), 0), qc, rev) >= _eff(_iota((qc, qc), 1), qc, rev)).astype(bf16)
    ones = jnp.ones((qc, qc), bf16)
    rowx = _eff(_iota((qc, r * qc), 0), qc, rev)
    colx = _eff(_iota((qc, r * qc), 1) & (qc - 1), qc, rev)
    cumcol = _dot_sel(tril, laq)
    cumrow = _dot_sel(ones, laq * (rowx <= colx).astype(f32))
    decay = jnp.where(rowx >= colx, jnp.exp(jnp.minimum(cumcol - cumrow, 0.0)), 0.0)
    qb, kb, vb = q.astype(bf16), k.astype(bf16), v.astype(bf16)
    ktile = jnp.concatenate([kb] * r, axis=0) if r > 1 else kb
    attn = (_dot_nt(qb, ktile) * decay).astype(bf16)
    if r > 1:
        vt = jnp.concatenate([vb] * r, axis=0)
        same_head = (_iota((r * qc, r * p), 0) >> _log2(qc)) == (_iota((r * qc, r * p), 1) >> _log2(p))
        vt = jnp.where(same_head, vt, jnp.zeros_like(vt))
    else:
        vt = vb
    cum_p = _dot_sel(tril, lap)
    tot_q = _dot_sel(ones, lap)
    tot_n = _dot_sel(jnp.ones((n, qc), bf16), lap)
    y = _dot(attn, vt) + jnp.exp(cum_p) * _dot(qb, s.astype(bf16))
    s_new = jnp.exp(tot_n) * s + _dot_tn(kb, (v * jnp.exp(tot_q - cum_p)).astype(bf16))
    return y, s_new


def _vd_chunk(q, k, v, lf, st, *, rev, hb):
    c = SUB
    cc = c * c
    kd = vd = HGRN_EXPAND
    sh = _log2(c)
    n_sub = CHUNK // c
    rt = _eff(_iota((cc, c), 0) >> sh, c, rev)
    rs = _eff(_iota((cc, c), 0) & (c - 1), c, rev)
    j = _eff(_iota((cc, c), 1), c, rev)
    rep_t = (rt == j).astype(bf16)
    rep_s = rs == j
    rep_s_b, rep_s_f = rep_s.astype(bf16), rep_s.astype(f32)
    rep_t_tr = (_eff(_iota((c, cc), 1) >> sh, c, rev) == _eff(_iota((c, cc), 0), c, rev)).astype(bf16)
    between = ((j > rs) & (j <= rt)).astype(bf16)
    valid = _eff(_iota((cc, kd), 0) >> sh, c, rev) >= _eff(_iota((cc, kd), 0) & (c - 1), c, rev)
    tril = (_eff(_iota((c, c), 0), c, rev) >= _eff(_iota((c, c), 1), c, rev)).astype(bf16)
    ones_c, ones_v = jnp.ones((c, c), bf16), jnp.ones((vd, c), bf16)
    cells = [(h, i) for h in range(hb) for i in range(n_sub)]

    def cut(a, h, i):
        return a[i * c:(i + 1) * c, h * kd:(h + 1) * kd]

    qs = {hi: cut(q, *hi) for hi in cells}
    ks = {hi: cut(k, *hi) for hi in cells}
    lfs = {hi: cut(lf, *hi) for hi in cells}
    vb = {hi: cut(v, *hi).astype(bf16) for hi in cells}
    seg = {hi: _dot_sel(between, lfs[hi]) for hi in cells}
    e = {hi: jnp.where(valid, jnp.exp(seg[hi]), 0.0) for hi in cells}
    z = {hi: _dot(rep_t, qs[hi].astype(bf16)) * _dot(rep_s_b, ks[hi].astype(bf16)) * e[hi] for hi in cells}
    w = {hi: jnp.sum(z[hi], axis=1, keepdims=True) * rep_s_f for hi in cells}
    attn = {hi: _dot_sel(rep_t_tr, w[hi]) for hi in cells}
    cum = {hi: _dot_sel(tril, lfs[hi]) for hi in cells}
    tot = {hi: _dot_sel(ones_c, lfs[hi]) for hi in cells}
    tot_v = {hi: jnp.exp(_dot_sel(ones_v, lfs[hi])) for hi in cells}
    y_in = {hi: _dot(attn[hi].astype(bf16), vb[hi]) for hi in cells}
    q_e = {hi: (qs[hi] * jnp.exp(cum[hi])).astype(bf16) for hi in cells}
    k_e = {hi: (ks[hi] * jnp.exp(tot[hi] - cum[hi])).astype(bf16) for hi in cells}
    sts = [st[h] for h in range(hb)]
    ys = {}
    for i in (range(n_sub - 1, -1, -1) if rev else range(n_sub)):
        for h in range(hb):
            ys[(h, i)] = y_in[(h, i)] + _dot_nt(q_e[(h, i)], sts[h].astype(bf16))
            sts[h] = sts[h] * tot_v[(h, i)] + _dot_tn(vb[(h, i)], k_e[(h, i)])
    y = jnp.concatenate([jnp.concatenate([ys[(h, i)] for i in range(n_sub)], axis=0) for h in range(hb)], axis=1)
    return y, jnp.concatenate([a[None] for a in sts], axis=0)


def _dl_chunk(q, k, v, bq, laq, bk, lak, s, *, rev, hb, rr):
    qc = CHUNK
    kd = GDN_HEAD
    row, col = _eff(_iota((qc, qc), 0), qc, rev), _eff(_iota((qc, qc), 1), qc, rev)
    tril = (row >= col).astype(bf16)
    triu = (row <= col).astype(f32)
    ones = jnp.ones((qc, qc), bf16)
    ones_k = jnp.ones((kd, qc), bf16)
    eye = (row == col).astype(f32)
    heads = range(hb)
    lanes = [slice(j * kd, (j + 1) * kd) for j in heads]
    kf = [k[:, lanes[j // rr]] for j in heads]
    qb = [q[:, lanes[j // rr]].astype(bf16) for j in heads]
    kb = [a.astype(bf16) for a in kf]
    kk = [_dot_nt(kb[j], kb[j]) for j in heads]
    qk = [_dot_nt(qb[j], kb[j]) for j in heads]
    eseg = [jnp.exp(jnp.minimum(_dot_sel(tril, laq[j]) - _dot_sel(ones, laq[j] * triu), 0.0)) for j in heads]
    cum_k = [_dot_sel(tril, lak[:, lanes[j]]) for j in heads]
    tot = [_dot_sel(ones, lak[:, lanes[j]]) for j in heads]
    tot_k = [_dot_sel(ones_k, lak[:, lanes[j]]) for j in heads]
    ecum = [jnp.exp(a) for a in cum_k]
    mp = [-(bq[j] * kk[j] * jnp.where(row > col, eseg[j], 0.0)) for j in heads]
    tinv = [eye + a for a in mp]
    for _ in range(_log2(qc) - 1):
        mp = [_dot3(a, a) for a in mp]
        tinv = [t + _dot3(t, a) for t, a in zip(tinv, mp)]
    u = [_dot3(tinv[j], v[:, lanes[j]] * bk[:, lanes[j]]) for j in heads]
    w = [_dot3(tinv[j], kf[j] * bk[:, lanes[j]] * ecum[j]) for j in heads]
    sb = [s[j].astype(bf16) for j in heads]
    v_new = [u[j] - _dot(w[j].astype(bf16), sb[j]) for j in heads]
    ys = [_dot((qk[j] * jnp.where(row >= col, eseg[j], 0.0)).astype(bf16), v_new[j].astype(bf16)) + ecum[j] * _dot(qb[j], sb[j])
          for j in heads]
    ss = [jnp.exp(tot_k[j]) * s[j] + _dot_tn(kb[j], (v_new[j] * jnp.exp(tot[j] - cum_k[j])).astype(bf16)) for j in heads]
    return jnp.concatenate(ys, axis=1), jnp.concatenate([a[None] for a in ss], axis=0)


def _scan_call(chunk, ins, maps, blocks, state_shape, y_shape, y_block, y_map, units, nc, ncx, rev, name,
               dy=None, sprev=None):
    n = len(ins)
    params = pltpu.CompilerParams(dimension_semantics=("parallel", "arbitrary"), vmem_limit_bytes=VMEM_LIMIT)
    state_block = (None, None) + state_shape
    zeros = (0,) * len(state_shape)

    def chunk_at(pos):
        return jnp.where(pos < ncx, ncx - 1 - pos, nc + ncx - 1 - pos) if rev else pos

    def at(m, pos_of):
        return lambda u, c: m(u, chunk_at(pos_of(c)))

    if dy is None:
        def fwd_body(*refs):
            in_refs, y_ref, sp_ref, s_scr = refs[:n], refs[n], refs[n + 1], refs[n + 2]

            @pl.when(pl.program_id(1) == 0)
            def _():
                s_scr[...] = jnp.zeros_like(s_scr)

            s = s_scr[...]
            sp_ref[...] = s
            y, s_new = chunk(*[r[...] for r in in_refs], s)
            y_ref[...] = y
            s_scr[...] = s_new

        same = lambda c: c
        return pl.pallas_call(
            fwd_body, grid=(units, nc),
            in_specs=[pl.BlockSpec(b, at(m, same)) for b, m in zip(blocks, maps)],
            out_specs=[pl.BlockSpec(y_block, at(y_map, same)), pl.BlockSpec(state_block, lambda u, c: (u, c) + zeros)],
            out_shape=[jax.ShapeDtypeStruct(y_shape, f32), jax.ShapeDtypeStruct((units, nc) + state_shape, f32)],
            scratch_shapes=[pltpu.VMEM(state_shape, f32)],
            compiler_params=params, name=name,
        )(*ins)

    def bwd_body(*refs):
        in_refs, sp_ref, dy_ref = refs[:n], refs[n], refs[n + 1]
        out_refs, ds_scr = refs[n + 2:2 * n + 2], refs[2 * n + 2]

        @pl.when(pl.program_id(1) == 0)
        def _():
            ds_scr[...] = jnp.zeros_like(ds_scr)

        _, vjp = jax.vjp(chunk, *[r[...] for r in in_refs], sp_ref[...])
        grads = vjp((dy_ref[...], ds_scr[...]))
        for r, g in zip(out_refs, grads[:n]):
            r[...] = g
        ds_scr[...] = grads[n]

    back = lambda c: nc - 1 - c
    return pl.pallas_call(
        bwd_body, grid=(units, nc),
        in_specs=[pl.BlockSpec(b, at(m, back)) for b, m in zip(blocks, maps)]
        + [pl.BlockSpec(state_block, lambda u, c: (u, nc - 1 - c) + zeros), pl.BlockSpec(y_block, at(y_map, back))],
        out_specs=[pl.BlockSpec(b, at(m, back)) for b, m in zip(blocks, maps)],
        out_shape=[jax.ShapeDtypeStruct(a.shape, f32) for a in ins],
        scratch_shapes=[pltpu.VMEM(state_shape, f32)],
        compiler_params=params, name=name,
    )(*ins, sprev, dy)


def _scan_name(kind, rev, t, dy):
    return f"{kind}_{'bwd' if dy is not None else 'fwd'}_{'rev' if rev else 'fore'}_{t}"


def _sd_call(g, r, p, rev, ncx, ins, dy=None, sprev=None):
    q, _, v, _, _ = ins
    t = q.shape[0]
    n = q.shape[1] // g
    im = lambda u, c: (c, u)
    imx = lambda u, c: (u, c, 0)
    bq, bv, bx = (CHUNK, n), (CHUNK, r * p), (None, CHUNK, r * CHUNK)
    return _scan_call(functools.partial(_sd_chunk, r=r, p=p, rev=rev), list(ins), [im, im, im, imx, im], [bq, bq, bv, bx, bv],
                      (n, r * p), v.shape, bv, im, g, t // CHUNK, ncx, rev, _scan_name(f"sd{n}x{r}x{p}", rev, t, dy), dy, sprev)


def _vd_call(rev, ncx, ins, dy=None, sprev=None):
    q = ins[0]
    t, h = q.shape[0], q.shape[1] // HGRN_EXPAND
    hb = HEADS_PER_STEP
    im = lambda u, c: (c, u)
    blk = (CHUNK, hb * HGRN_EXPAND)
    return _scan_call(functools.partial(_vd_chunk, rev=rev, hb=hb), list(ins), [im] * 4, [blk] * 4,
                      (hb, HGRN_EXPAND, HGRN_EXPAND), q.shape, blk, im, h // hb, t // CHUNK, ncx, rev,
                      _scan_name("vd", rev, t, dy), dy, sprev)


def _dl_call(rev, ncx, ins, dy=None, sprev=None):
    q, _, v, _, _, _, _ = ins
    t = q.shape[0]
    hv = v.shape[1] // GDN_HEAD
    rr = hv // (q.shape[1] // GDN_HEAD)
    hb = HEADS_PER_STEP
    im = lambda u, c: (c, u)
    imx = lambda u, c: (u, c, 0)
    bqk, bv, bx = (CHUNK, hb // rr * GDN_HEAD), (CHUNK, hb * GDN_HEAD), (hb, CHUNK, CHUNK)
    return _scan_call(functools.partial(_dl_chunk, rev=rev, hb=hb, rr=rr), list(ins), [im, im, im, imx, imx, im, im],
                      [bqk, bqk, bv, bx, bx, bv, bv], (hb, GDN_HEAD, GDN_HEAD), v.shape, bv, im, hv // hb, t // CHUNK, ncx, rev,
                      _scan_name("dl", rev, t, dy), dy, sprev)


_SCAN_CALLS = {"ssd": functools.partial(_sd_call, SSD_GROUPS, SSD_HEADS // SSD_GROUPS, SSD_HEAD_DIM),
               "ret": functools.partial(_sd_call, RET_HEADS, 1, RET_V), "hgrn": _vd_call, "gdn": _dl_call}
_SCANS = {}


def _scan(kind, rev, lc):
    key = (kind, rev, lc)
    if key not in _SCANS:
        call = functools.partial(_SCAN_CALLS[kind], rev, lc // CHUNK)

        @jax.custom_vjp
        def scan(*ins):
            return call(ins)[0]

        def fwd(*ins):
            y, sprev = call(ins)
            return y, (ins, sprev)

        def bwd(res, dy):
            ins, sprev = res
            return tuple(call(ins, dy, sprev))

        scan.defvjp(fwd, bwd)
        _SCANS[key] = scan
    return _SCANS[key]


def _place():
    x, y, c = lax.axis_index("x"), lax.axis_index("y"), lax.axis_index("c")
    chips = [(1 - x, y), (x, 1 - y), (1 - x, 1 - y)]
    return x, y, c, chips


def _small_all_gather(block):
    m_per, n = block.shape

    def body(x_ref, out_ref, send_sems, recv_sems, local_sem):
        x, y, c, chips = _place()
        me, sibling = (x, y, c), (x, y, 1 - c)

        def rows(px, py, pc):
            return out_ref.at[pl.ds((4 * px + 2 * py + pc) * m_per, m_per), :]

        def copy(k, blk, to, src=None):
            return pltpu.make_async_remote_copy(
                src_ref=rows(*blk) if src is None else src, dst_ref=rows(*blk),
                send_sem=send_sems.at[k], recv_sem=recv_sems.at[k], device_id=to, device_id_type=MESH)

        mine = pltpu.make_async_copy(x_ref, rows(*me), local_sem)
        mine.start()
        first = [copy(0, me, sibling, src=x_ref)]
        first += [copy(1 + j, me, (*chip, c), src=x_ref) for j, chip in enumerate(chips)]
        for cp in first:
            cp.start()
        passed = [copy(4 + j, (*chip, c), sibling) for j, chip in enumerate(chips)]
        for j, chip in enumerate(chips):
            copy(1 + j, (*chip, c), me).wait_recv()
            passed[j].start()
        copy(0, sibling, me).wait_recv()
        for j, chip in enumerate(chips):
            copy(4 + j, (*chip, 1 - c), me).wait_recv()
        for cp in first + passed:
            cp.wait_send()
        mine.wait()

    return pl.pallas_call(
        body,
        out_shape=jax.ShapeDtypeStruct((N_DEV * m_per, n), block.dtype),
        in_specs=[pl.BlockSpec(memory_space=pltpu.VMEM)],
        out_specs=pl.BlockSpec(memory_space=pltpu.VMEM),
        scratch_shapes=[pltpu.SemaphoreType.DMA((7,)), pltpu.SemaphoreType.DMA((7,)), pltpu.SemaphoreType.DMA],
        compiler_params=pltpu.CompilerParams(vmem_limit_bytes=VMEM_LIMIT),
        name=f"small_all_gather_{m_per}x{n}",
    )(block)


def _any_spec():
    return pl.BlockSpec(memory_space=pl.ANY)


def _weight_all_gather(pack):
    rows, width = pack.shape
    half = rows // 2

    def body(w_ref, out_ref, send_sems, recv_sems, local_sem):
        x, y, c, chips = _place()
        sibling = (x, y, 1 - c)

        def part(px, py, pc):
            return out_ref.at[2 * px + py, pl.ds(pl.multiple_of(pc * half, 16), half), :]

        def copy(k, blk, to, src=None):
            return pltpu.make_async_remote_copy(
                src_ref=part(*blk) if src is None else src, dst_ref=part(*blk),
                send_sem=send_sems.at[k], recv_sem=recv_sems.at[k], device_id=to, device_id_type=MESH)

        mine = pltpu.make_async_copy(w_ref, out_ref.at[2 * x + y], local_sem)
        mine.start()
        my_half = w_ref.at[pl.ds(pl.multiple_of(c * half, 16), half), :]
        first = [copy(j, (x, y, c), (*chip, c), src=my_half) for j, chip in enumerate(chips)]
        for cp in first:
            cp.start()
        passed = [copy(3 + j, (*chip, c), sibling) for j, chip in enumerate(chips)]
        for j, chip in enumerate(chips):
            copy(j, (*chip, c), (x, y, c)).wait_recv()
            passed[j].start()
        for j, chip in enumerate(chips):
            copy(3 + j, (*chip, 1 - c), (x, y, c)).wait_recv()
        for cp in first + passed:
            cp.wait_send()
        mine.wait()

    return pl.pallas_call(
        body,
        out_shape=jax.ShapeDtypeStruct((N_CHIPS, rows, width), pack.dtype),
        in_specs=[_any_spec()], out_specs=_any_spec(),
        scratch_shapes=[pltpu.SemaphoreType.DMA((6,)), pltpu.SemaphoreType.DMA((6,)), pltpu.SemaphoreType.DMA],
        name="weight_all_gather",
    )(pack)


def _sibling_swap(send):
    def body(s_ref, out_ref, send_sem, recv_sem):
        x, y, c, _ = _place()
        cp = pltpu.make_async_remote_copy(src_ref=s_ref, dst_ref=out_ref, send_sem=send_sem, recv_sem=recv_sem,
                                          device_id=(x, y, 1 - c), device_id_type=MESH)
        cp.start()
        cp.wait()

    return pl.pallas_call(
        body, out_shape=jax.ShapeDtypeStruct(send.shape, send.dtype),
        in_specs=[_any_spec()], out_specs=_any_spec(),
        scratch_shapes=[pltpu.SemaphoreType.DMA, pltpu.SemaphoreType.DMA],
        name=f"sibling_swap_{'x'.join(map(str, send.shape))}_{jnp.dtype(send.dtype).name}",
    )(send)


def _chip_exchange(parts):
    _, rows, width = parts.shape

    def body(p_ref, out_ref, send_sems, recv_sems):
        x, y, c, chips = _place()
        copies = [pltpu.make_async_remote_copy(
            src_ref=p_ref.at[2 * px + py], dst_ref=out_ref.at[j], send_sem=send_sems.at[j], recv_sem=recv_sems.at[j],
            device_id=(px, py, c), device_id_type=MESH) for j, (px, py) in enumerate(chips)]
        for cp in copies:
            cp.start()
        for cp in copies:
            cp.wait()

    return pl.pallas_call(
        body, out_shape=jax.ShapeDtypeStruct((3, rows, width), parts.dtype),
        in_specs=[_any_spec()], out_specs=_any_spec(),
        scratch_shapes=[pltpu.SemaphoreType.DMA((3,)), pltpu.SemaphoreType.DMA((3,))],
        name="chip_exchange",
    )(parts)


def _row_tile(rows, width, n_arrays):
    budget = VMEM_LIMIT // (4 * 2 * n_arrays * width * 2)
    return _pick(rows, tuple(t for t in (1024, 512, 256, 128, 64, 32, 16, 8) if t <= max(budget, 8)))


def _add_n(arrays, out_dtype):
    rows, width = arrays[0].shape
    n = len(arrays)
    tr = _row_tile(rows, width, n + 1)

    def body(*refs):
        acc = refs[0][...].astype(f32)
        for r in refs[1:n]:
            acc = acc + r[...].astype(f32)
        refs[n][...] = acc.astype(out_dtype)

    spec = pl.BlockSpec((tr, width), lambda i: (i, 0))
    return pl.pallas_call(
        body, grid=(rows // tr,), in_specs=[spec] * n, out_specs=spec,
        out_shape=jax.ShapeDtypeStruct((rows, width), out_dtype),
        compiler_params=pltpu.CompilerParams(dimension_semantics=("parallel",), vmem_limit_bytes=VMEM_LIMIT),
        name=f"add{n}_{rows}x{width}_{jnp.dtype(out_dtype).name}",
    )(*arrays)


def _adamw(w, g, m, v):
    rows, width = w.shape
    tr = _row_tile(rows, width, 7)
    c1 = 1.0 / (1.0 - ADAM_B1 ** ADAM_STEP)
    c2 = 1.0 / (1.0 - ADAM_B2 ** ADAM_STEP)

    def body(w_ref, g_ref, m_ref, v_ref, d_ref, mo_ref, vo_ref):
        gg = g_ref[...]
        m_new = ADAM_B1 * m_ref[...] + (1.0 - ADAM_B1) * gg
        v_new = ADAM_B2 * v_ref[...] + (1.0 - ADAM_B2) * (gg * gg)
        d_ref[...] = -ADAM_LR * ((m_new * c1) / (jnp.sqrt(v_new * c2) + ADAM_EPS) + ADAM_WD * w_ref[...])
        mo_ref[...] = m_new
        vo_ref[...] = v_new

    spec = pl.BlockSpec((tr, width), lambda i: (i, 0))
    sds = jax.ShapeDtypeStruct((rows, width), f32)
    return pl.pallas_call(
        body, grid=(rows // tr,), in_specs=[spec] * 4, out_specs=[spec] * 3, out_shape=[sds] * 3,
        compiler_params=pltpu.CompilerParams(dimension_semantics=("parallel",), vmem_limit_bytes=VMEM_LIMIT),
        name=f"adamw_{rows}x{width}",
    )(w, g, m, v)


def _rmsnorm(x, g):
    return x * lax.rsqrt(jnp.mean(x * x, axis=-1, keepdims=True) + NORM_EPS) * g


def _conv(u, w, lc):
    t = u.shape[0]
    pos = jnp.arange(t)[:, None]
    zero = jnp.zeros((1, u.shape[1]), u.dtype)
    prev = jnp.where((pos == 0) | (pos == lc), 0.0, jnp.concatenate([zero, u[:-1]], axis=0))
    nxt = jnp.where((pos == lc - 1) | (pos == t - 1), 0.0, jnp.concatenate([u[1:], zero], axis=0))
    return w[0] * prev + w[1] * u + w[2] * nxt


def _lanes_q(a):
    return jnp.broadcast_to(a.T[:, :, None], (a.shape[1], a.shape[0], CHUNK))


def _ssd_mixer(h, lc, start, w_main, w_dt, conv_w, conv_b, dt_bias, a_log, d_skip, norm_g, w_out):
    t = h.shape[0]
    r = SSD_HEADS // SSD_GROUPS
    u = pmm(h, w_main)
    dt = pmm(h, w_dt)
    z, xbc = u[:, :SSD_D_INNER], u[:, SSD_D_INNER:]
    xbc = jax.nn.silu(_conv(xbc, conv_w, lc) + conv_b)
    xs = xbc[:, :SSD_D_INNER]
    bm = xbc[:, SSD_D_INNER:SSD_D_INNER + SSD_GROUPS * SSD_STATE]
    cm = xbc[:, SSD_D_INNER + SSD_GROUPS * SSD_STATE:]
    dt = jax.nn.softplus(dt.reshape(t, 2, SSD_HEADS) + dt_bias)
    log_a = -jnp.exp(a_log) * dt
    per_lane = lambda a: jnp.repeat(a, SSD_HEAD_DIM, axis=1)
    y = jnp.repeat(d_skip, SSD_HEAD_DIM) * xs
    for d in (0, 1):
        lap = per_lane(log_a[:, d])
        laq = lap.reshape(t, SSD_GROUPS, r * CHUNK).transpose(1, 0, 2)
        y = y + _scan("ssd", d == 1, lc)(cm, bm, xs * per_lane(dt[:, d]), laq, lap)
    y = y[start:] * jax.nn.silu(z[start:])
    n = t - start
    y = _rmsnorm(y.reshape(n, SSD_GROUPS, -1), norm_g.reshape(SSD_GROUPS, -1)).reshape(n, SSD_D_INNER)
    return pmm(y, w_out)


def _rope(a, rows):
    pos = jnp.arange(rows * GRID_W)
    row = (pos // GRID_W).astype(f32)
    col = (pos % GRID_W).astype(f32)
    half = a.shape[-1] // 2
    inv_freq = ROPE_BASE ** (-jnp.arange(0, half, 2, dtype=f32) / half)

    def rot(u, p):
        ang = p[:, None] * inv_freq
        cos, sin = jnp.cos(ang)[:, None, :], jnp.sin(ang)[:, None, :]
        u1, u2 = jnp.split(u, 2, axis=-1)
        return jnp.concatenate([u1 * cos - u2 * sin, u2 * cos + u1 * sin], axis=-1)

    return jnp.concatenate([rot(a[..., :half], row), rot(a[..., half:], col)], axis=-1)


def _ret_mixer(h, lc, start, w_in, log_decay, w_out):
    t = h.shape[0]
    u = pmm(h, w_in)
    q = u[:, :D_MODEL].reshape(t, RET_HEADS, RET_QK)
    k = u[:, D_MODEL:2 * D_MODEL].reshape(t, RET_HEADS, RET_QK) * RET_QK ** -0.5
    v = u[:, 2 * D_MODEL:2 * D_MODEL + RET_DV]
    gate = u[:, 2 * D_MODEL + RET_DV:]
    rows = (t - lc) // GRID_W
    q = jnp.concatenate([q[:lc], _rope(q[lc:], rows)], axis=0).reshape(t, D_MODEL)
    k = jnp.concatenate([k[:lc], _rope(k[lc:], rows)], axis=0).reshape(t, D_MODEL)
    y = 0.0
    for d in (0, 1):
        laq = jnp.broadcast_to(log_decay[d][:, None, None], (RET_HEADS, t, CHUNK))
        lap = jnp.broadcast_to(jnp.repeat(log_decay[d], RET_V)[None, :], (t, RET_DV))
        y = y + _scan("ret", d == 1, lc)(q, k, v, laq, lap)
    n = t - start
    y = y[start:].reshape(n, RET_HEADS, RET_V)
    mu = jnp.mean(y, axis=-1, keepdims=True)
    var = jnp.mean(jnp.square(y - mu), axis=-1, keepdims=True)
    y = ((y - mu) * lax.rsqrt(var + NORM_EPS)).reshape(n, RET_DV) * jax.nn.silu(gate[start:])
    return pmm(y, w_out)


def _hgrn_mixer(h, lc, start, w_in, lb, norm_g, w_out):
    t = h.shape[0]
    u = pmm(h, w_in)
    q, f_f, f_b, inp, gate = (u[:, i * D_MODEL:(i + 1) * D_MODEL] for i in range(5))

    def gates(f):
        log_f = jnp.logaddexp(jnp.log(lb), jnp.log1p(-lb) + jax.nn.log_sigmoid(f))
        return log_f, (1 - lb) * jax.nn.sigmoid(-f)

    lf_f, k_f = gates(f_f)
    lf_b, k_b = gates(f_b)
    y = _scan("hgrn", False, lc)(q, k_f, inp, lf_f) + _scan("hgrn", True, lc)(q, k_b, inp, lf_b)
    n = t - start
    y = _rmsnorm(y[start:].reshape(n, HGRN_HEADS, HGRN_EXPAND), norm_g.reshape(HGRN_HEADS, HGRN_EXPAND))
    y = y.reshape(n, D_MODEL) * jax.nn.silu(gate[start:])
    return pmm(y, w_out)


def _l2norm(a):
    return a * lax.rsqrt(jnp.sum(a * a, axis=-1, keepdims=True) + 1e-6)


def _gdn_mixer(h, lc, start, w_main, w_gate, conv_w, dt_bias, a_log, norm_g, w_out):
    t = h.shape[0]
    u = pmm(h, w_main)
    ba = pmm(h, w_gate)
    qkv = jax.nn.silu(_conv(u[:, :GDN_CONV_CH], conv_w, lc))
    z = u[:, GDN_CONV_CH:]
    q = (_l2norm(qkv[:, :GDN_DK].reshape(t, GDN_K_HEADS, GDN_HEAD)) * GDN_HEAD ** -0.5).reshape(t, GDN_DK)
    k = _l2norm(qkv[:, GDN_DK:2 * GDN_DK].reshape(t, GDN_K_HEADS, GDN_HEAD)).reshape(t, GDN_DK)
    v = qkv[:, 2 * GDN_DK:]
    beta = jax.nn.sigmoid(ba[:, :2 * GDN_V_HEADS].reshape(t, 2, GDN_V_HEADS))
    log_a = -jnp.exp(a_log) * jax.nn.softplus(ba[:, 2 * GDN_V_HEADS:].reshape(t, 2, GDN_V_HEADS) + dt_bias)
    per_lane = lambda a: jnp.repeat(a, GDN_HEAD, axis=1)
    y = 0.0
    for d in (0, 1):
        y = y + _scan("gdn", d == 1, lc)(q, k, v, _lanes_q(beta[:, d]), _lanes_q(log_a[:, d]),
                                         per_lane(beta[:, d]), per_lane(log_a[:, d]))
    n = t - start
    y = _rmsnorm(y[start:].reshape(n, GDN_V_HEADS, GDN_HEAD), norm_g) * jax.nn.silu(
        z[start:].reshape(n, GDN_V_HEADS, GDN_HEAD))
    return pmm(y.reshape(n, GDN_DV), w_out)


def _local_loss(x, mod, big, small, ctx, target):
    lc = ctx.shape[0]
    tok = jnp.concatenate([ctx, x], axis=0)
    t = tok.shape[0]
    is_ctx = (jnp.arange(t) < lc)[:, None]
    p = jax.nn.softmax(small["hgrn_lb_logits"], axis=0)
    for i in range(DEPTH):
        last = i == DEPTH - 1
        start = lc if last else 0
        sh1, sc1, g1, sh2, sc2, g2 = (jnp.where(is_ctx, m[1], m[0]) for m in jnp.split(mod[i], 6, axis=-1))
        h = _rmsnorm(tok, small["norm_g"][i, 0]) * (1 + sc1) + sh1
        if i == 0:
            y = _ssd_mixer(h, lc, start, big["ssd_w_main"], big["ssd_w_dt"], small["ssd_conv_w"][0], small["ssd_conv_b"][0],
                           small["ssd_dt_bias"][0], small["ssd_a_log"][0], small["ssd_d"][0], small["ssd_norm_g"][0],
                           big["ssd_w_out"])
        elif i == 1:
            y = _ret_mixer(h, lc, start, big["ret_w_in"], small["ret_log_decay"][0], big["ret_w_out"])
        elif i == 2:
            lb = jnp.cumsum(p, axis=0)[i] - p[0]
            y = _hgrn_mixer(h, lc, start, big["hgrn_w_in"], lb, small["hgrn_norm_g"][0], big["hgrn_w_out"])
        else:
            y = _gdn_mixer(h, lc, start, big["gdn_w_main"], big["gdn_w_gate"], small["gdn_conv_w"][0],
                           small["gdn_dt_bias"][0], small["gdn_a_log"][0], small["gdn_norm_g"][0], big["gdn_w_out"])
        if last:
            tok, sc2, sh2, g1, g2 = tok[lc:], sc2[lc:], sh2[lc:], g1[lc:], g2[lc:]
        tok = tok + g1 * y
        h = _rmsnorm(tok, small["norm_g"][i, 1]) * (1 + sc2) + sh2
        hid = jnp.square(jax.nn.relu(pmm(h, big["mlp_w1"][i])))
        tok = tok + g2 * pmm(hid, big["mlp_w2"][i])
    out = _rmsnorm(tok, small["final_g"])
    return 0.5 * jnp.sum(jnp.mean(jnp.square(out - target), axis=-1))


def _rows_of(a, width):
    return a.reshape(-1, width)


def _pad_rows(flat, width, mult=8):
    rows = -(-flat.shape[0] // width)
    rows = -(-rows // mult) * mult
    return jnp.pad(flat, (0, rows * width - flat.shape[0])).reshape(rows, width)


def _unpack_weights(gathered, shard_shapes):
    full, r0 = {}, 0
    for name in BIG:
        shp = shard_shapes[name]
        rows = math.prod(shp) // PACK_W
        blk = gathered[:, r0:r0 + rows].reshape((N_CHIPS,) + shp)
        r0 += rows
        if name in COL_SHARDED:
            full[name] = jnp.moveaxis(blk, 0, 2).reshape(shp[0], shp[1], N_CHIPS * shp[2])
        else:
            full[name] = jnp.moveaxis(blk, 0, 1).reshape(shp[0], N_CHIPS * shp[1], shp[2])
    return full


def _pack_grads(grads, shard_shapes):
    parts = []
    for name in BIG:
        shp = shard_shapes[name]
        g = grads[name]
        if name in COL_SHARDED:
            blk = jnp.moveaxis(g.reshape(shp[0], shp[1], N_CHIPS, shp[2]), 2, 0)
        else:
            blk = jnp.moveaxis(g.reshape(shp[0], N_CHIPS, shp[1], shp[2]), 1, 0)
        parts.append(blk.reshape(N_CHIPS, -1, PACK_W))
    return jnp.concatenate(parts, axis=1)


def _silu_grad(a):
    s = jax.nn.sigmoid(a)
    return s * (1 + a * (1 - s))


def kernel(x, c, ctx, c_ctx, ada_w, ada_b, norm_g, mlp_w1, mlp_w2, final_g, ssd_w_in, ssd_conv_w, ssd_conv_b, ssd_dt_bias, ssd_a_log, ssd_d, ssd_norm_g, ssd_w_out, ret_w_in, ret_log_decay, ret_w_out, hgrn_w_in, hgrn_lb_logits, hgrn_norm_g, hgrn_w_out, gdn_w_in, gdn_conv_w, gdn_dt_bias, gdn_a_log, gdn_norm_g, gdn_w_out, loss_target, m_c_ctx, m_ada_w, m_ada_b, m_norm_g, m_mlp_w1, m_mlp_w2, m_final_g, m_ssd_w_in, m_ssd_conv_w, m_ssd_conv_b, m_ssd_dt_bias, m_ssd_a_log, m_ssd_d, m_ssd_norm_g, m_ssd_w_out, m_ret_w_in, m_ret_log_decay, m_ret_w_out, m_hgrn_w_in, m_hgrn_lb_logits, m_hgrn_norm_g, m_hgrn_w_out, m_gdn_w_in, m_gdn_conv_w, m_gdn_dt_bias, m_gdn_a_log, m_gdn_norm_g, m_gdn_w_out, v_c_ctx, v_ada_w, v_ada_b, v_norm_g, v_mlp_w1, v_mlp_w2, v_final_g, v_ssd_w_in, v_ssd_conv_w, v_ssd_conv_b, v_ssd_dt_bias, v_ssd_a_log, v_ssd_d, v_ssd_norm_g, v_ssd_w_out, v_ret_w_in, v_ret_log_decay, v_ret_w_out, v_hgrn_w_in, v_hgrn_lb_logits, v_hgrn_norm_g, v_hgrn_w_out, v_gdn_w_in, v_gdn_conv_w, v_gdn_dt_bias, v_gdn_a_log, v_gdn_norm_g, v_gdn_w_out):
    env = dict(locals())
    w_loc = {n: env[n] for n in WEIGHTS}
    m_loc = {n: env["m_" + n] for n in WEIGHTS}
    v_loc = {n: env["v_" + n] for n in WEIGHTS}
    chip = 2 * lax.axis_index("x") + lax.axis_index("y")
    core = lax.axis_index("c")
    dev = 2 * chip + core
    d = D_MODEL

    sharded_small = [n for n, ax in SMALL if ax is not None]
    flat1 = jnp.concatenate([c.reshape(-1)] + [w_loc[n].reshape(-1) for n in sharded_small])
    g1 = _small_all_gather(_pad_rows(flat1, SMALL_W)).reshape(N_DEV, -1)
    c_all = g1[:, :d]
    small, off = {n: w_loc[n] for n, ax in SMALL if ax is None}, d
    for n in sharded_small:
        ax = dict(SMALL)[n]
        size = w_loc[n].size
        pieces = [g1[2 * k, off:off + size].reshape(w_loc[n].shape) for k in range(N_CHIPS)]
        small[n] = jnp.concatenate(pieces, axis=ax)
        off += size

    n_sh = ada_w.shape[2]
    cond_in = jnp.concatenate([c_all, c_ctx[None]], axis=0)
    cond = jnp.pad(jax.nn.silu(cond_in), ((0, 16 - N_DEV - 1), (0, 0)))
    ada_b_sh = lax.dynamic_slice_in_dim(ada_b, chip * n_sh, n_sh, axis=1)
    mod_sh = jnp.stack([_mm(cond, ada_w[i], "nn") + ada_b_sh[i] for i in range(DEPTH)])
    g2 = _small_all_gather(mod_sh.reshape(-1, SMALL_W)).reshape(N_DEV, DEPTH, 16, n_sh)
    mod_all = jnp.concatenate([g2[2 * k] for k in range(N_CHIPS)], axis=-1)
    mod_loc = jnp.stack([lax.dynamic_index_in_dim(mod_all, dev, axis=1, keepdims=False), mod_all[:, N_DEV]], axis=1)

    shard_shapes = {n: w_loc[n].shape for n in BIG}
    pack = jnp.concatenate([_rows_of(w_loc[n].astype(bf16), PACK_W) for n in BIG], axis=0)
    full = _unpack_weights(_weight_all_gather(pack), shard_shapes)
    big = {"mlp_w1": tuple(full["mlp_w1"][i] for i in range(DEPTH)),
           "mlp_w2": tuple(full["mlp_w2"][i] for i in range(DEPTH)),
           "ssd_w_main": full["ssd_w_in"][0, :, :SSD_MAIN], "ssd_w_dt": full["ssd_w_in"][0, :, SSD_MAIN:],
           "ssd_w_out": full["ssd_w_out"][0], "ret_w_in": full["ret_w_in"][0], "ret_w_out": full["ret_w_out"][0],
           "hgrn_w_in": full["hgrn_w_in"][0], "hgrn_w_out": full["hgrn_w_out"][0],
           "gdn_w_main": full["gdn_w_in"][0, :, :GDN_MAIN], "gdn_w_gate": full["gdn_w_in"][0, :, GDN_MAIN:],
           "gdn_w_out": full["gdn_w_out"][0]}

    small_diff = {n: small[n] for n, _ in SMALL if n not in ("c_ctx", "ada_b")}
    loss_loc, (gx, gmod, gbig, gsmall) = jax.value_and_grad(_local_loss, argnums=(0, 1, 2, 3))(
        x[0], mod_loc, big, small_diff, ctx[0], loss_target[0])
    loss = lax.psum(loss_loc, ("x", "y", "c"))

    small_names = [n for n, _ in SMALL if n not in ("c_ctx", "ada_b")]
    flat3 = jnp.concatenate([gmod.reshape(-1)] + [gsmall[n].reshape(-1) for n in small_names])
    g3 = _small_all_gather(_pad_rows(flat3, SMALL_W)).reshape(N_DEV, -1)
    gmod_all = g3[:, :gmod.size].reshape(N_DEV, DEPTH, 2, 6 * d)
    ctx_row = gmod_all[0, :, 1]
    for b in range(1, N_DEV):
        ctx_row = ctx_row + gmod_all[b, :, 1]
    dmod = jnp.concatenate([jnp.moveaxis(gmod_all[:, :, 0], 0, 1), ctx_row[:, None],
                            jnp.zeros((DEPTH, 16 - N_DEV - 1, 6 * d), f32)], axis=1)
    grad_small, off = {}, gmod.size
    for n in small_names:
        size = small[n].size
        tot = g3[0, off:off + size]
        for b in range(1, N_DEV):
            tot = tot + g3[b, off:off + size]
        grad_small[n] = tot.reshape(small[n].shape)
        off += size
    grad_small["ada_b"] = jnp.sum(dmod, axis=1)
    dmod_sh = lax.dynamic_slice_in_dim(dmod, chip * n_sh, n_sh, axis=2)
    grad_ada_w = jnp.stack([_mm(cond, dmod_sh[i], "tn") for i in range(DEPTH)])
    dcond = _mm(dmod_sh[0], ada_w[0], "nt")
    for i in range(1, DEPTH):
        dcond = dcond + _mm(dmod_sh[i], ada_w[i], "nt")
    g4 = _small_all_gather(_pad_rows(dcond[N_DEV], SMALL_W)).reshape(N_DEV, -1)[:, :d]
    dcond_ctx = g4[0] + g4[2] + g4[4] + g4[6]
    grad_small["c_ctx"] = dcond_ctx * _silu_grad(c_ctx)
    for n, ax in SMALL:
        if ax is not None:
            width = w_loc[n].shape[ax]
            grad_small[n] = lax.dynamic_slice_in_dim(grad_small[n], chip * width, width, axis=ax)

    gfull = {"mlp_w1": jnp.stack(gbig["mlp_w1"]), "mlp_w2": jnp.stack(gbig["mlp_w2"]),
             "ssd_w_in": jnp.concatenate([gbig["ssd_w_main"], gbig["ssd_w_dt"]], axis=1)[None],
             "ssd_w_out": gbig["ssd_w_out"][None], "ret_w_in": gbig["ret_w_in"][None], "ret_w_out": gbig["ret_w_out"][None],
             "hgrn_w_in": gbig["hgrn_w_in"][None], "hgrn_w_out": gbig["hgrn_w_out"][None],
             "gdn_w_in": jnp.concatenate([gbig["gdn_w_main"], gbig["gdn_w_gate"]], axis=1)[None],
             "gdn_w_out": gbig["gdn_w_out"][None]}
    gp = _pack_grads(gfull, shard_shapes)
    rows = gp.shape[1]
    half = rows // 2
    gp = gp.reshape(N_CHIPS, 2, half, PACK_W)
    mine = lax.dynamic_index_in_dim(gp, core, axis=1, keepdims=False)
    theirs = lax.dynamic_index_in_dim(gp, 1 - core, axis=1, keepdims=False)
    got = _sibling_swap(theirs)
    pair = _add_n([mine.reshape(-1, PACK_W), got.reshape(-1, PACK_W)], bf16).reshape(N_CHIPS, half, PACK_W)
    landed = _chip_exchange(pair)
    own = lax.dynamic_index_in_dim(pair, chip, axis=0, keepdims=False)
    red_half = _add_n([own, landed[0], landed[1], landed[2]], f32)
    other_half = _sibling_swap(red_half)
    red = jnp.where(core == 0, jnp.concatenate([red_half, other_half], axis=0),
                    jnp.concatenate([other_half, red_half], axis=0))
    grad_big, r0 = {}, 0
    for n in BIG:
        nrows = w_loc[n].size // PACK_W
        grad_big[n] = red[r0:r0 + nrows].reshape(w_loc[n].shape)
        r0 += nrows

    grads = dict(grad_big)
    grads.update(grad_small)
    grads["ada_w"] = grad_ada_w
    delta, new_m, new_v = {}, {}, {}
    for n in BIG + ("ada_w",):
        width = w_loc[n].shape[-1]
        outs = _adamw(*[a.reshape(-1, width) for a in (w_loc[n], grads[n], m_loc[n], v_loc[n])])
        delta[n], new_m[n], new_v[n] = (o.reshape(w_loc[n].shape) for o in outs)
    names = [n for n, _ in SMALL]
    packs = [_pad_rows(jnp.concatenate([t[n].reshape(-1) for n in names]), SMALL_W) for t in (w_loc, grads, m_loc, v_loc)]
    outs = _adamw(*packs)
    off = 0
    for n in names:
        size = w_loc[n].size
        delta[n], new_m[n], new_v[n] = (o.reshape(-1)[off:off + size].reshape(w_loc[n].shape) for o in outs)
        off += size

    return (loss, gx[None], *[grads[n] for n in WEIGHTS], *[delta[n] for n in WEIGHTS],
            *[new_m[n] for n in WEIGHTS], *[new_v[n] for n in WEIGHTS])
```

```python
import functools
import math

import jax
import jax.numpy as jnp
from jax import lax
from jax.experimental import pallas as pl
from jax.experimental.pallas import tpu as pltpu

f32 = jnp.float32
bf16 = jnp.bfloat16
MESH = pl.DeviceIdType.MESH

D_MODEL = 2048
DEPTH = 4
GRID_W = 64
CHUNK = 64
SUB = 16
HGRN_HEADS_PER_STEP = 4
GDN_HEADS_PER_STEP = 8
NORM_EPS = 1e-6
ROPE_BASE = 10000.0
N_CHIPS = 4
N_DEV = 8
PACK_W = 2048
SMALL_W = 1024
VMEM_LIMIT = 48 * 1024 * 1024

SSD_D_INNER, SSD_HEADS, SSD_GROUPS, SSD_STATE, SSD_HEAD_DIM = 4096, 64, 8, 128, 64
SSD_CONV_CH = SSD_D_INNER + 2 * SSD_GROUPS * SSD_STATE
SSD_MAIN = SSD_D_INNER + SSD_CONV_CH
RET_HEADS, RET_QK, RET_V, RET_DV = 8, 256, 512, 4096
HGRN_HEADS, HGRN_EXPAND = 16, 128
GDN_K_HEADS, GDN_V_HEADS, GDN_HEAD = 16, 32, 128
GDN_DK, GDN_DV = 2048, 4096
GDN_CONV_CH = 2 * GDN_DK + GDN_DV
GDN_MAIN = GDN_CONV_CH + GDN_DV

ADAM_LR, ADAM_B1, ADAM_B2, ADAM_EPS, ADAM_WD, ADAM_STEP = 0.001, 0.9, 0.999, 1e-08, 0.01, 10

BIG = ("mlp_w1", "mlp_w2", "ssd_w_in", "ssd_w_out", "ret_w_in", "ret_w_out", "hgrn_w_in", "hgrn_w_out",
       "gdn_w_in", "gdn_w_out")
COL_SHARDED = ("mlp_w1", "ssd_w_in", "ret_w_in", "hgrn_w_in", "gdn_w_in")
SMALL = (("c_ctx", None), ("ada_b", None), ("norm_g", 2), ("final_g", None), ("ssd_conv_w", 2), ("ssd_conv_b", None),
         ("ssd_dt_bias", None), ("ssd_a_log", None), ("ssd_d", None), ("ssd_norm_g", None), ("ret_log_decay", None),
         ("hgrn_lb_logits", None), ("hgrn_norm_g", 1), ("gdn_conv_w", 2), ("gdn_dt_bias", None), ("gdn_a_log", None),
         ("gdn_norm_g", None))
WEIGHTS = ("c_ctx", "ada_w", "ada_b", "norm_g", "mlp_w1", "mlp_w2", "final_g", "ssd_w_in", "ssd_conv_w", "ssd_conv_b",
           "ssd_dt_bias", "ssd_a_log", "ssd_d", "ssd_norm_g", "ssd_w_out", "ret_w_in", "ret_log_decay", "ret_w_out",
           "hgrn_w_in", "hgrn_lb_logits", "hgrn_norm_g", "hgrn_w_out", "gdn_w_in", "gdn_conv_w", "gdn_dt_bias",
           "gdn_a_log", "gdn_norm_g", "gdn_w_out")


def _dot(a, b, prec=None):
    return lax.dot_general(a, b, (((1,), (0,)), ((), ())), precision=prec, preferred_element_type=f32)


def _dot_nt(a, b, prec=None):
    return lax.dot_general(a, b, (((1,), (1,)), ((), ())), precision=prec, preferred_element_type=f32)


def _dot_tn(a, b, prec=None):
    return lax.dot_general(a, b, (((0,), (0,)), ((), ())), precision=prec, preferred_element_type=f32)


def _split2(x):
    hi = x.astype(bf16)
    return hi, (x - hi.astype(f32)).astype(bf16)


def _split3(x):
    hi = x.astype(bf16)
    rest = x - hi.astype(f32)
    mid = rest.astype(bf16)
    return hi, mid, (rest - mid.astype(f32)).astype(bf16)


@jax.custom_vjp
def _dot_sel(mask, x):
    hi, mid, lo = _split3(x)
    return _dot(mask, hi) + _dot(mask, mid) + _dot(mask, lo)


def _dot_sel_fwd(mask, x):
    return _dot_sel(mask, x), mask


def _dot_sel_bwd(mask, g):
    hi, mid, lo = _split3(g)
    return jnp.zeros_like(mask), _dot_tn(mask, hi) + _dot_tn(mask, mid) + _dot_tn(mask, lo)


_dot_sel.defvjp(_dot_sel_fwd, _dot_sel_bwd)


def _dot3_with(dot, a, b):
    a_hi, a_lo = _split2(a)
    b_hi, b_lo = _split2(b)
    return dot(a_hi, b_hi) + dot(a_hi, b_lo) + dot(a_lo, b_hi)


@jax.custom_vjp
def _dot3(a, b):
    return _dot3_with(_dot, a, b)


def _dot3_fwd(a, b):
    return _dot3(a, b), (a, b)


def _dot3_bwd(res, g):
    a, b = res
    return _dot3_with(_dot_nt, g, b), _dot3_with(_dot_tn, a, g)


_dot3.defvjp(_dot3_fwd, _dot3_bwd)


def _iota(shape, d):
    return lax.broadcasted_iota(jnp.int32, shape, d)


def _log2(n):
    assert n & (n - 1) == 0
    return n.bit_length() - 1


def _pick(n, cands):
    for t in cands:
        if n % t == 0:
            return t
    return n


MM_VMEM_BUDGET = 36 * 1024 * 1024
MM_STEP_BYTES = 1.2e6
MM_ACC_WEIGHT = 0.15


def _mm_tiles(m, n, k, sa, sb, so):
    best = None
    tms = [t for t in (2304, 2048, 1152, 1024, 768, 576, 512, 384, 256, 128, 64, 32, 16, 8) if m % t == 0] or [m]
    tns = [t for t in (2048, 1024, 512, 256, 128) if n % t == 0] or [n]
    tks = [t for t in (k, 4096, 2048, 1024, 768, 512, 256, 128, 64, 32, 16, 8) if k % t == 0]
    for tm in tms:
        for tn in tns:
            for tk in tks:
                nk = k // tk
                vmem = 2 * (tm * tk * sa + tk * tn * sb + tm * tn * so) + (tm * tn * 4 if nk > 1 else 0)
                if vmem > MM_VMEM_BUDGET:
                    continue
                cost = m * k * sa * (1 if nk == 1 else n // tn) + k * n * sb * (m // tm) + m * n * so
                cost += (m // tm) * (n // tn) * nk * MM_STEP_BYTES
                cost += MM_ACC_WEIGHT * m * n * 8 * nk if nk > 1 else 0
                if best is None or cost < best[0]:
                    best = (cost, tm, tn, tk)
    return best[1:]


def _mm(a, b, mode, out_dtype=f32):
    if mode == "nn":
        (m, k), n = a.shape, b.shape[1]
    elif mode == "nt":
        (m, k), n = a.shape, b.shape[0]
    else:
        (k, m), n = a.shape, b.shape[1]
    tm, tn, tk = _mm_tiles(m, n, k, a.dtype.itemsize, b.dtype.itemsize, jnp.dtype(out_dtype).itemsize)
    nk = k // tk
    dot = {"nn": _dot, "nt": _dot_nt, "tn": _dot_tn}[mode]

    def body(a_ref, b_ref, o_ref, *scratch):
        prod = dot(a_ref[...].astype(bf16), b_ref[...].astype(bf16))
        if nk == 1:
            o_ref[...] = prod.astype(out_dtype)
            return
        acc_ref, = scratch
        kk = pl.program_id(2)

        @pl.when(kk == 0)
        def _():
            acc_ref[...] = jnp.zeros_like(acc_ref)

        acc_ref[...] += prod

        @pl.when(kk == nk - 1)
        def _():
            o_ref[...] = acc_ref[...].astype(out_dtype)

    if mode == "tn":
        a_spec = pl.BlockSpec((tk, tm), lambda i, j, kk: (kk, i))
    else:
        a_spec = pl.BlockSpec((tm, tk), lambda i, j, kk: (i, kk))
    if mode == "nt":
        b_spec = pl.BlockSpec((tn, tk), lambda i, j, kk: (j, kk))
    else:
        b_spec = pl.BlockSpec((tk, tn), lambda i, j, kk: (kk, j))
    return pl.pallas_call(
        body,
        grid=(m // tm, n // tn, nk),
        in_specs=[a_spec, b_spec],
        out_specs=pl.BlockSpec((tm, tn), lambda i, j, kk: (i, j)),
        out_shape=jax.ShapeDtypeStruct((m, n), out_dtype),
        scratch_shapes=[pltpu.VMEM((tm, tn), f32)] if nk > 1 else [],
        compiler_params=pltpu.CompilerParams(dimension_semantics=("parallel", "parallel", "arbitrary"),
                                             vmem_limit_bytes=VMEM_LIMIT),
        name=f"mm_{mode}_{m}x{k}x{n}_{jnp.dtype(out_dtype).name}",
    )(a, b)


@jax.custom_vjp
def pmm(a, w):
    return _mm(a, w, "nn")


def _pmm_fwd(a, w):
    return _mm(a, w, "nn"), (a, w)


def _pmm_bwd(res, g):
    a, w = res
    return _mm(g, w, "nt"), _mm(a, g, "tn", bf16)


pmm.defvjp(_pmm_fwd, _pmm_bwd)


def _eff(i, n, rev):
    return (n - 1 - i) if rev else i


def _sd_chunk(q, k, v, laq, lap, s, *, r, p, rev):
    qc = CHUNK
    n = q.shape[1]
    tril = (_eff(_iota((qc, qc), 0), qc, rev) >= _eff(_iota((qc, qc), 1), qc, rev)).astype(bf16)
    ones = jnp.ones((qc, qc), bf16)
    rowx = _eff(_iota((qc, r * qc), 0), qc, rev)
    colx = _eff(_iota((qc, r * qc), 1) & (qc - 1), qc, rev)
    cumcol = _dot_sel(tril, laq)
    cumrow = _dot_sel(ones, laq * (rowx <= colx).astype(f32))
    decay = jnp.where(rowx >= colx, jnp.exp(jnp.minimum(cumcol - cumrow, 0.0)), 0.0)
    qb, kb, vb = q.astype(bf16), k.astype(bf16), v.astype(bf16)
    ktile = jnp.concatenate([kb] * r, axis=0) if r > 1 else kb
    attn = (_dot_nt(qb, ktile) * decay).astype(bf16)
    if r > 1:
        vt = jnp.concatenate([vb] * r, axis=0)
        same_head = (_iota((r * qc, r * p), 0) >> _log2(qc)) == (_iota((r * qc, r * p), 1) >> _log2(p))
        vt = jnp.where(same_head, vt, jnp.zeros_like(vt))
    else:
        vt = vb
    cum_p = _dot_sel(tril, lap)
    tot_q = _dot_sel(ones, lap)
    tot_n = _dot_sel(jnp.ones((n, qc), bf16), lap)
    y = _dot(attn, vt) + jnp.exp(cum_p) * _dot(qb, s.astype(bf16))
    s_new = jnp.exp(tot_n) * s + _dot_tn(kb, (v * jnp.exp(tot_q - cum_p)).astype(bf16))
    return y, s_new


def _vd_chunk(q, k, v, lf, st, *, rev, hb):
    c = SUB
    cc = c * c
    kd = vd = HGRN_EXPAND
    sh = _log2(c)
    n_sub = CHUNK // c
    rt = _eff(_iota((cc, c), 0) >> sh, c, rev)
    rs = _eff(_iota((cc, c), 0) & (c - 1), c, rev)
    j = _eff(_iota((cc, c), 1), c, rev)
    rep_t = (rt == j).astype(bf16)
    rep_s = rs == j
    rep_s_b, rep_s_f = rep_s.astype(bf16), rep_s.astype(f32)
    rep_t_tr = (_eff(_iota((c, cc), 1) >> sh, c, rev) == _eff(_iota((c, cc), 0), c, rev)).astype(bf16)
    between = ((j > rs) & (j <= rt)).astype(bf16)
    valid = _eff(_iota((cc, kd), 0) >> sh, c, rev) >= _eff(_iota((cc, kd), 0) & (c - 1), c, rev)
    tril = (_eff(_iota((c, c), 0), c, rev) >= _eff(_iota((c, c), 1), c, rev)).astype(bf16)
    ones_c, ones_v = jnp.ones((c, c), bf16), jnp.ones((vd, c), bf16)
    cells = [(h, i) for h in range(hb) for i in range(n_sub)]

    def cut(a, h, i):
        return a[i * c:(i + 1) * c, h * kd:(h + 1) * kd]

    qs = {hi: cut(q, *hi) for hi in cells}
    ks = {hi: cut(k, *hi) for hi in cells}
    lfs = {hi: cut(lf, *hi) for hi in cells}
    vb = {hi: cut(v, *hi).astype(bf16) for hi in cells}
    seg = {hi: _dot_sel(between, lfs[hi]) for hi in cells}
    e = {hi: jnp.where(valid, jnp.exp(seg[hi]), 0.0) for hi in cells}
    z = {hi: _dot(rep_t, qs[hi].astype(bf16)) * _dot(rep_s_b, ks[hi].astype(bf16)) * e[hi] for hi in cells}
    w = {hi: jnp.sum(z[hi], axis=1, keepdims=True) * rep_s_f for hi in cells}
    attn = {hi: _dot_sel(rep_t_tr, w[hi]) for hi in cells}
    cum = {hi: _dot_sel(tril, lfs[hi]) for hi in cells}
    tot = {hi: _dot_sel(ones_c, lfs[hi]) for hi in cells}
    tot_v = {hi: jnp.exp(_dot_sel(ones_v, lfs[hi])) for hi in cells}
    y_in = {hi: _dot(attn[hi].astype(bf16), vb[hi]) for hi in cells}
    q_e = {hi: (qs[hi] * jnp.exp(cum[hi])).astype(bf16) for hi in cells}
    k_e = {hi: (ks[hi] * jnp.exp(tot[hi] - cum[hi])).astype(bf16) for hi in cells}
    sts = [st[h] for h in range(hb)]
    ys = {}
    for i in (range(n_sub - 1, -1, -1) if rev else range(n_sub)):
        for h in range(hb):
            ys[(h, i)] = y_in[(h, i)] + _dot_nt(q_e[(h, i)], sts[h].astype(bf16))
            sts[h] = sts[h] * tot_v[(h, i)] + _dot_tn(vb[(h, i)], k_e[(h, i)])
    y = jnp.concatenate([jnp.concatenate([ys[(h, i)] for i in range(n_sub)], axis=0) for h in range(hb)], axis=1)
    return y, jnp.concatenate([a[None] for a in sts], axis=0)


def _dl_chunk(q, k, v, bq, laq, bk, lak, s, *, rev, hb, rr):
    qc = CHUNK
    kd = GDN_HEAD
    row, col = _eff(_iota((qc, qc), 0), qc, rev), _eff(_iota((qc, qc), 1), qc, rev)
    tril = (row >= col).astype(bf16)
    triu = (row <= col).astype(f32)
    ones = jnp.ones((qc, qc), bf16)
    ones_k = jnp.ones((kd, qc), bf16)
    eye = (row == col).astype(f32)
    heads = range(hb)
    lanes = [slice(j * kd, (j + 1) * kd) for j in heads]
    kf = [k[:, lanes[j // rr]] for j in heads]
    qb = [q[:, lanes[j // rr]].astype(bf16) for j in heads]
    kb = [a.astype(bf16) for a in kf]
    kk = [_dot_nt(kb[j], kb[j]) for j in heads]
    qk = [_dot_nt(qb[j], kb[j]) for j in heads]
    eseg = [jnp.exp(jnp.minimum(_dot_sel(tril, laq[j]) - _dot_sel(ones, laq[j] * triu), 0.0)) for j in heads]
    cum_k = [_dot_sel(tril, lak[:, lanes[j]]) for j in heads]
    tot = [_dot_sel(ones, lak[:, lanes[j]]) for j in heads]
    tot_k = [_dot_sel(ones_k, lak[:, lanes[j]]) for j in heads]
    ecum = [jnp.exp(a) for a in cum_k]
    mp = [-(bq[j] * kk[j] * jnp.where(row > col, eseg[j], 0.0)) for j in heads]
    tinv = [eye + a for a in mp]
    for _ in range(_log2(qc) - 1):
        mp = [_dot3(a, a) for a in mp]
        tinv = [t + _dot3(t, a) for t, a in zip(tinv, mp)]
    u = [_dot3(tinv[j], v[:, lanes[j]] * bk[:, lanes[j]]) for j in heads]
    w = [_dot3(tinv[j], kf[j] * bk[:, lanes[j]] * ecum[j]) for j in heads]
    sb = [s[j].astype(bf16) for j in heads]
    v_new = [u[j] - _dot(w[j].astype(bf16), sb[j]) for j in heads]
    ys = [_dot((qk[j] * jnp.where(row >= col, eseg[j], 0.0)).astype(bf16), v_new[j].astype(bf16)) + ecum[j] * _dot(qb[j], sb[j])
          for j in heads]
    ss = [jnp.exp(tot_k[j]) * s[j] + _dot_tn(kb[j], (v_new[j] * jnp.exp(tot[j] - cum_k[j])).astype(bf16)) for j in heads]
    return jnp.concatenate(ys, axis=1), jnp.concatenate([a[None] for a in ss], axis=0)


def _scan_call(chunk, ins, maps, blocks, state_shape, y_shape, y_block, y_map, units, nc, ncx, rev, name,
               dy=None, sprev=None):
    n = len(ins)
    params = pltpu.CompilerParams(dimension_semantics=("parallel", "arbitrary"), vmem_limit_bytes=VMEM_LIMIT)
    state_block = (None, None) + state_shape
    zeros = (0,) * len(state_shape)

    def chunk_at(pos):
        return jnp.where(pos < ncx, ncx - 1 - pos, nc + ncx - 1 - pos) if rev else pos

    def at(m, pos_of):
        return lambda u, c: m(u, chunk_at(pos_of(c)))

    if dy is None:
        def fwd_body(*refs):
            in_refs, y_ref, sp_ref, s_scr = refs[:n], refs[n], refs[n + 1], refs[n + 2]

            @pl.when(pl.program_id(1) == 0)
            def _():
                s_scr[...] = jnp.zeros_like(s_scr)

            s = s_scr[...]
            sp_ref[...] = s
            y, s_new = chunk(*[r[...] for r in in_refs], s)
            y_ref[...] = y
            s_scr[...] = s_new

        same = lambda c: c
        return pl.pallas_call(
            fwd_body, grid=(units, nc),
            in_specs=[pl.BlockSpec(b, at(m, same)) for b, m in zip(blocks, maps)],
            out_specs=[pl.BlockSpec(y_block, at(y_map, same)), pl.BlockSpec(state_block, lambda u, c: (u, c) + zeros)],
            out_shape=[jax.ShapeDtypeStruct(y_shape, f32), jax.ShapeDtypeStruct((units, nc) + state_shape, f32)],
            scratch_shapes=[pltpu.VMEM(state_shape, f32)],
            compiler_params=params, name=name,
        )(*ins)

    def bwd_body(*refs):
        in_refs, sp_ref, dy_ref = refs[:n], refs[n], refs[n + 1]
        out_refs, ds_scr = refs[n + 2:2 * n + 2], refs[2 * n + 2]

        @pl.when(pl.program_id(1) == 0)
        def _():
            ds_scr[...] = jnp.zeros_like(ds_scr)

        _, vjp = jax.vjp(chunk, *[r[...] for r in in_refs], sp_ref[...])
        grads = vjp((dy_ref[...], ds_scr[...]))
        for r, g in zip(out_refs, grads[:n]):
            r[...] = g
        ds_scr[...] = grads[n]

    back = lambda c: nc - 1 - c
    return pl.pallas_call(
        bwd_body, grid=(units, nc),
        in_specs=[pl.BlockSpec(b, at(m, back)) for b, m in zip(blocks, maps)]
        + [pl.BlockSpec(state_block, lambda u, c: (u, nc - 1 - c) + zeros), pl.BlockSpec(y_block, at(y_map, back))],
        out_specs=[pl.BlockSpec(b, at(m, back)) for b, m in zip(blocks, maps)],
        out_shape=[jax.ShapeDtypeStruct(a.shape, f32) for a in ins],
        scratch_shapes=[pltpu.VMEM(state_shape, f32)],
        compiler_params=params, name=name,
    )(*ins, sprev, dy)


def _scan_name(kind, rev, t, dy):
    return f"{kind}_{'bwd' if dy is not None else 'fwd'}_{'rev' if rev else 'fore'}_{t}"


def _sd_call(g, r, p, rev, ncx, ins, dy=None, sprev=None):
    q, _, v, _, _ = ins
    t = q.shape[0]
    n = q.shape[1] // g
    im = lambda u, c: (c, u)
    imx = lambda u, c: (u, c, 0)
    bq, bv, bx = (CHUNK, n), (CHUNK, r * p), (None, CHUNK, r * CHUNK)
    return _scan_call(functools.partial(_sd_chunk, r=r, p=p, rev=rev), list(ins), [im, im, im, imx, im], [bq, bq, bv, bx, bv],
                      (n, r * p), v.shape, bv, im, g, t // CHUNK, ncx, rev, _scan_name(f"sd{n}x{r}x{p}", rev, t, dy), dy, sprev)


def _vd_call(rev, ncx, ins, dy=None, sprev=None):
    q = ins[0]
    t, h = q.shape[0], q.shape[1] // HGRN_EXPAND
    hb = HGRN_HEADS_PER_STEP
    im = lambda u, c: (c, u)
    blk = (CHUNK, hb * HGRN_EXPAND)
    return _scan_call(functools.partial(_vd_chunk, rev=rev, hb=hb), list(ins), [im] * 4, [blk] * 4,
                      (hb, HGRN_EXPAND, HGRN_EXPAND), q.shape, blk, im, h // hb, t // CHUNK, ncx, rev,
                      _scan_name("vd", rev, t, dy), dy, sprev)


def _dl_call(rev, ncx, ins, dy=None, sprev=None):
    q, _, v, _, _, _, _ = ins
    t = q.shape[0]
    hv = v.shape[1] // GDN_HEAD
    rr = hv // (q.shape[1] // GDN_HEAD)
    hb = GDN_HEADS_PER_STEP
    im = lambda u, c: (c, u)
    imx = lambda u, c: (u, c, 0)
    bqk, bv, bx = (CHUNK, hb // rr * GDN_HEAD), (CHUNK, hb * GDN_HEAD), (hb, CHUNK, CHUNK)
    return _scan_call(functools.partial(_dl_chunk, rev=rev, hb=hb, rr=rr), list(ins), [im, im, im, imx, imx, im, im],
                      [bqk, bqk, bv, bx, bx, bv, bv], (hb, GDN_HEAD, GDN_HEAD), v.shape, bv, im, hv // hb, t // CHUNK, ncx, rev,
                      _scan_name("dl", rev, t, dy), dy, sprev)


_SCAN_CALLS = {"ssd": functools.partial(_sd_call, SSD_GROUPS, SSD_HEADS // SSD_GROUPS, SSD_HEAD_DIM),
               "ret": functools.partial(_sd_call, RET_HEADS, 1, RET_V), "hgrn": _vd_call, "gdn": _dl_call}
_SCANS = {}


def _scan(kind, rev, lc):
    key = (kind, rev, lc)
    if key not in _SCANS:
        call = functools.partial(_SCAN_CALLS[kind], rev, lc // CHUNK)

        @jax.custom_vjp
        def scan(*ins):
            return call(ins)[0]

        def fwd(*ins):
            y, sprev = call(ins)
            return y, (ins, sprev)

        def bwd(res, dy):
            ins, sprev = res
            return tuple(call(ins, dy, sprev))

        scan.defvjp(fwd, bwd)
        _SCANS[key] = scan
    return _SCANS[key]


def _place():
    x, y, c = lax.axis_index("x"), lax.axis_index("y"), lax.axis_index("c")
    chips = [(1 - x, y), (x, 1 - y), (1 - x, 1 - y)]
    return x, y, c, chips


def _small_all_gather(block):
    m_per, n = block.shape

    def body(x_ref, out_ref, send_sems, recv_sems, local_sem):
        x, y, c, chips = _place()
        me, sibling = (x, y, c), (x, y, 1 - c)

        def rows(px, py, pc):
            return out_ref.at[pl.ds((4 * px + 2 * py + pc) * m_per, m_per), :]

        def copy(k, blk, to, src=None):
            return pltpu.make_async_remote_copy(
                src_ref=rows(*blk) if src is None else src, dst_ref=rows(*blk),
                send_sem=send_sems.at[k], recv_sem=recv_sems.at[k], device_id=to, device_id_type=MESH)

        mine = pltpu.make_async_copy(x_ref, rows(*me), local_sem)
        mine.start()
        first = [copy(0, me, sibling, src=x_ref)]
        first += [copy(1 + j, me, (*chip, c), src=x_ref) for j, chip in enumerate(chips)]
        for cp in first:
            cp.start()
        passed = [copy(4 + j, (*chip, c), sibling) for j, chip in enumerate(chips)]
        for j, chip in enumerate(chips):
            copy(1 + j, (*chip, c), me).wait_recv()
            passed[j].start()
        copy(0, sibling, me).wait_recv()
        for j, chip in enumerate(chips):
            copy(4 + j, (*chip, 1 - c), me).wait_recv()
        for cp in first + passed:
            cp.wait_send()
        mine.wait()

    return pl.pallas_call(
        body,
        out_shape=jax.ShapeDtypeStruct((N_DEV * m_per, n), block.dtype),
        in_specs=[pl.BlockSpec(memory_space=pltpu.VMEM)],
        out_specs=pl.BlockSpec(memory_space=pltpu.VMEM),
        scratch_shapes=[pltpu.SemaphoreType.DMA((7,)), pltpu.SemaphoreType.DMA((7,)), pltpu.SemaphoreType.DMA],
        compiler_params=pltpu.CompilerParams(vmem_limit_bytes=VMEM_LIMIT),
        name=f"small_all_gather_{m_per}x{n}",
    )(block)


def _any_spec():
    return pl.BlockSpec(memory_space=pl.ANY)


def _weight_all_gather(pack):
    rows, width = pack.shape
    half = rows // 2

    def body(w_ref, out_ref, send_sems, recv_sems, local_sem):
        x, y, c, chips = _place()
        sibling = (x, y, 1 - c)

        def part(px, py, pc):
            return out_ref.at[2 * px + py, pl.ds(pl.multiple_of(pc * half, 16), half), :]

        def copy(k, blk, to, src=None):
            return pltpu.make_async_remote_copy(
                src_ref=part(*blk) if src is None else src, dst_ref=part(*blk),
                send_sem=send_sems.at[k], recv_sem=recv_sems.at[k], device_id=to, device_id_type=MESH)

        mine = pltpu.make_async_copy(w_ref, out_ref.at[2 * x + y], local_sem)
        mine.start()
        my_half = w_ref.at[pl.ds(pl.multiple_of(c * half, 16), half), :]
        first = [copy(j, (x, y, c), (*chip, c), src=my_half) for j, chip in enumerate(chips)]
        for cp in first:
            cp.start()
        passed = [copy(3 + j, (*chip, c), sibling) for j, chip in enumerate(chips)]
        for j, chip in enumerate(chips):
            copy(j, (*chip, c), (x, y, c)).wait_recv()
            passed[j].start()
        for j, chip in enumerate(chips):
            copy(3 + j, (*chip, 1 - c), (x, y, c)).wait_recv()
        for cp in first + passed:
            cp.wait_send()
        mine.wait()

    return pl.pallas_call(
        body,
        out_shape=jax.ShapeDtypeStruct((N_CHIPS, rows, width), pack.dtype),
        in_specs=[_any_spec()], out_specs=_any_spec(),
        scratch_shapes=[pltpu.SemaphoreType.DMA((6,)), pltpu.SemaphoreType.DMA((6,)), pltpu.SemaphoreType.DMA],
        name="weight_all_gather",
    )(pack)


def _sibling_swap(send):
    def body(s_ref, out_ref, send_sem, recv_sem):
        x, y, c, _ = _place()
        cp = pltpu.make_async_remote_copy(src_ref=s_ref, dst_ref=out_ref, send_sem=send_sem, recv_sem=recv_sem,
                                          device_id=(x, y, 1 - c), device_id_type=MESH)
        cp.start()
        cp.wait()

    return pl.pallas_call(
        body, out_shape=jax.ShapeDtypeStruct(send.shape, send.dtype),
        in_specs=[_any_spec()], out_specs=_any_spec(),
        scratch_shapes=[pltpu.SemaphoreType.DMA, pltpu.SemaphoreType.DMA],
        name=f"sibling_swap_{'x'.join(map(str, send.shape))}_{jnp.dtype(send.dtype).name}",
    )(send)


def _chip_exchange(parts):
    _, rows, width = parts.shape

    def body(p_ref, out_ref, send_sems, recv_sems):
        x, y, c, chips = _place()
        copies = [pltpu.make_async_remote_copy(
            src_ref=p_ref.at[2 * px + py], dst_ref=out_ref.at[j], send_sem=send_sems.at[j], recv_sem=recv_sems.at[j],
            device_id=(px, py, c), device_id_type=MESH) for j, (px, py) in enumerate(chips)]
        for cp in copies:
            cp.start()
        for cp in copies:
            cp.wait()

    return pl.pallas_call(
        body, out_shape=jax.ShapeDtypeStruct((3, rows, width), parts.dtype),
        in_specs=[_any_spec()], out_specs=_any_spec(),
        scratch_shapes=[pltpu.SemaphoreType.DMA((3,)), pltpu.SemaphoreType.DMA((3,))],
        name="chip_exchange",
    )(parts)


def _row_tile(rows, width, n_arrays):
    budget = VMEM_LIMIT // (4 * 2 * n_arrays * width * 2)
    return _pick(rows, tuple(t for t in (1024, 512, 256, 128, 64, 32, 16, 8) if t <= max(budget, 8)))


def _add_n(arrays, out_dtype):
    rows, width = arrays[0].shape
    n = len(arrays)
    tr = _row_tile(rows, width, n + 1)

    def body(*refs):
        acc = refs[0][...].astype(f32)
        for r in refs[1:n]:
            acc = acc + r[...].astype(f32)
        refs[n][...] = acc.astype(out_dtype)

    spec = pl.BlockSpec((tr, width), lambda i: (i, 0))
    return pl.pallas_call(
        body, grid=(rows // tr,), in_specs=[spec] * n, out_specs=spec,
        out_shape=jax.ShapeDtypeStruct((rows, width), out_dtype),
        compiler_params=pltpu.CompilerParams(dimension_semantics=("parallel",), vmem_limit_bytes=VMEM_LIMIT),
        name=f"add{n}_{rows}x{width}_{jnp.dtype(out_dtype).name}",
    )(*arrays)


def _adamw(w, g, m, v):
    rows, width = w.shape
    tr = _row_tile(rows, width, 7)
    c1 = 1.0 / (1.0 - ADAM_B1 ** ADAM_STEP)
    c2 = 1.0 / (1.0 - ADAM_B2 ** ADAM_STEP)

    def body(w_ref, g_ref, m_ref, v_ref, d_ref, mo_ref, vo_ref):
        gg = g_ref[...]
        m_new = ADAM_B1 * m_ref[...] + (1.0 - ADAM_B1) * gg
        v_new = ADAM_B2 * v_ref[...] + (1.0 - ADAM_B2) * (gg * gg)
        d_ref[...] = -ADAM_LR * ((m_new * c1) / (jnp.sqrt(v_new * c2) + ADAM_EPS) + ADAM_WD * w_ref[...])
        mo_ref[...] = m_new
        vo_ref[...] = v_new

    spec = pl.BlockSpec((tr, width), lambda i: (i, 0))
    sds = jax.ShapeDtypeStruct((rows, width), f32)
    return pl.pallas_call(
        body, grid=(rows // tr,), in_specs=[spec] * 4, out_specs=[spec] * 3, out_shape=[sds] * 3,
        compiler_params=pltpu.CompilerParams(dimension_semantics=("parallel",), vmem_limit_bytes=VMEM_LIMIT),
        name=f"adamw_{rows}x{width}",
    )(w, g, m, v)


def _rmsnorm(x, g):
    return x * lax.rsqrt(jnp.mean(x * x, axis=-1, keepdims=True) + NORM_EPS) * g


def _conv(u, w, lc):
    t = u.shape[0]
    pos = jnp.arange(t)[:, None]
    zero = jnp.zeros((1, u.shape[1]), u.dtype)
    prev = jnp.where((pos == 0) | (pos == lc), 0.0, jnp.concatenate([zero, u[:-1]], axis=0))
    nxt = jnp.where((pos == lc - 1) | (pos == t - 1), 0.0, jnp.concatenate([u[1:], zero], axis=0))
    return w[0] * prev + w[1] * u + w[2] * nxt


def _lanes_q(a):
    return jnp.broadcast_to(a.T[:, :, None], (a.shape[1], a.shape[0], CHUNK))


def _ssd_mixer(h, lc, start, w_main, w_dt, conv_w, conv_b, dt_bias, a_log, d_skip, norm_g, w_out):
    t = h.shape[0]
    r = SSD_HEADS // SSD_GROUPS
    u = pmm(h, w_main)
    dt = pmm(h, w_dt)
    z, xbc = u[:, :SSD_D_INNER], u[:, SSD_D_INNER:]
    xbc = jax.nn.silu(_conv(xbc, conv_w, lc) + conv_b)
    xs = xbc[:, :SSD_D_INNER]
    bm = xbc[:, SSD_D_INNER:SSD_D_INNER + SSD_GROUPS * SSD_STATE]
    cm = xbc[:, SSD_D_INNER + SSD_GROUPS * SSD_STATE:]
    dt = jax.nn.softplus(dt.reshape(t, 2, SSD_HEADS) + dt_bias)
    log_a = -jnp.exp(a_log) * dt
    per_lane = lambda a: jnp.repeat(a, SSD_HEAD_DIM, axis=1)
    y = jnp.repeat(d_skip, SSD_HEAD_DIM) * xs
    for d in (0, 1):
        lap = per_lane(log_a[:, d])
        laq = lap.reshape(t, SSD_GROUPS, r * CHUNK).transpose(1, 0, 2)
        y = y + _scan("ssd", d == 1, lc)(cm, bm, xs * per_lane(dt[:, d]), laq, lap)
    y = y[start:] * jax.nn.silu(z[start:])
    n = t - start
    y = _rmsnorm(y.reshape(n, SSD_GROUPS, -1), norm_g.reshape(SSD_GROUPS, -1)).reshape(n, SSD_D_INNER)
    return pmm(y, w_out)


def _rope(a, rows):
    pos = jnp.arange(rows * GRID_W)
    row = (pos // GRID_W).astype(f32)
    col = (pos % GRID_W).astype(f32)
    half = a.shape[-1] // 2
    inv_freq = ROPE_BASE ** (-jnp.arange(0, half, 2, dtype=f32) / half)

    def rot(u, p):
        ang = p[:, None] * inv_freq
        cos, sin = jnp.cos(ang)[:, None, :], jnp.sin(ang)[:, None, :]
        u1, u2 = jnp.split(u, 2, axis=-1)
        return jnp.concatenate([u1 * cos - u2 * sin, u2 * cos + u1 * sin], axis=-1)

    return jnp.concatenate([rot(a[..., :half], row), rot(a[..., half:], col)], axis=-1)


def _ret_mixer(h, lc, start, w_in, log_decay, w_out):
    t = h.shape[0]
    u = pmm(h, w_in)
    q = u[:, :D_MODEL].reshape(t, RET_HEADS, RET_QK)
    k = u[:, D_MODEL:2 * D_MODEL].reshape(t, RET_HEADS, RET_QK) * RET_QK ** -0.5
    v = u[:, 2 * D_MODEL:2 * D_MODEL + RET_DV]
    gate = u[:, 2 * D_MODEL + RET_DV:]
    rows = (t - lc) // GRID_W
    q = jnp.concatenate([q[:lc], _rope(q[lc:], rows)], axis=0).reshape(t, D_MODEL)
    k = jnp.concatenate([k[:lc], _rope(k[lc:], rows)], axis=0).reshape(t, D_MODEL)
    y = 0.0
    for d in (0, 1):
        laq = jnp.broadcast_to(log_decay[d][:, None, None], (RET_HEADS, t, CHUNK))
        lap = jnp.broadcast_to(jnp.repeat(log_decay[d], RET_V)[None, :], (t, RET_DV))
        y = y + _scan("ret", d == 1, lc)(q, k, v, laq, lap)
    n = t - start
    y = y[start:].reshape(n, RET_HEADS, RET_V)
    mu = jnp.mean(y, axis=-1, keepdims=True)
    var = jnp.mean(jnp.square(y - mu), axis=-1, keepdims=True)
    y = ((y - mu) * lax.rsqrt(var + NORM_EPS)).reshape(n, RET_DV) * jax.nn.silu(gate[start:])
    return pmm(y, w_out)


def _hgrn_mixer(h, lc, start, w_in, lb, norm_g, w_out):
    t = h.shape[0]
    u = pmm(h, w_in)
    q, f_f, f_b, inp, gate = (u[:, i * D_MODEL:(i + 1) * D_MODEL] for i in range(5))

    def gates(f):
        log_f = jnp.logaddexp(jnp.log(lb), jnp.log1p(-lb) + jax.nn.log_sigmoid(f))
        return log_f, (1 - lb) * jax.nn.sigmoid(-f)

    lf_f, k_f = gates(f_f)
    lf_b, k_b = gates(f_b)
    y = _scan("hgrn", False, lc)(q, k_f, inp, lf_f) + _scan("hgrn", True, lc)(q, k_b, inp, lf_b)
    n = t - start
    y = _rmsnorm(y[start:].reshape(n, HGRN_HEADS, HGRN_EXPAND), norm_g.reshape(HGRN_HEADS, HGRN_EXPAND))
    y = y.reshape(n, D_MODEL) * jax.nn.silu(gate[start:])
    return pmm(y, w_out)


def _l2norm(a):
    return a * lax.rsqrt(jnp.sum(a * a, axis=-1, keepdims=True) + 1e-6)


def _gdn_mixer(h, lc, start, w_main, w_gate, conv_w, dt_bias, a_log, norm_g, w_out):
    t = h.shape[0]
    u = pmm(h, w_main)
    ba = pmm(h, w_gate)
    qkv = jax.nn.silu(_conv(u[:, :GDN_CONV_CH], conv_w, lc))
    z = u[:, GDN_CONV_CH:]
    q = (_l2norm(qkv[:, :GDN_DK].reshape(t, GDN_K_HEADS, GDN_HEAD)) * GDN_HEAD ** -0.5).reshape(t, GDN_DK)
    k = _l2norm(qkv[:, GDN_DK:2 * GDN_DK].reshape(t, GDN_K_HEADS, GDN_HEAD)).reshape(t, GDN_DK)
    v = qkv[:, 2 * GDN_DK:]
    beta = jax.nn.sigmoid(ba[:, :2 * GDN_V_HEADS].reshape(t, 2, GDN_V_HEADS))
    log_a = -jnp.exp(a_log) * jax.nn.softplus(ba[:, 2 * GDN_V_HEADS:].reshape(t, 2, GDN_V_HEADS) + dt_bias)
    per_lane = lambda a: jnp.repeat(a, GDN_HEAD, axis=1)
    y = 0.0
    for d in (0, 1):
        y = y + _scan("gdn", d == 1, lc)(q, k, v, _lanes_q(beta[:, d]), _lanes_q(log_a[:, d]),
                                         per_lane(beta[:, d]), per_lane(log_a[:, d]))
    n = t - start
    y = _rmsnorm(y[start:].reshape(n, GDN_V_HEADS, GDN_HEAD), norm_g) * jax.nn.silu(
        z[start:].reshape(n, GDN_V_HEADS, GDN_HEAD))
    return pmm(y.reshape(n, GDN_DV), w_out)


def _local_loss(x, mod, big, small, ctx, target):
    lc = ctx.shape[0]
    tok = jnp.concatenate([ctx, x], axis=0)
    t = tok.shape[0]
    is_ctx = (jnp.arange(t) < lc)[:, None]
    p = jax.nn.softmax(small["hgrn_lb_logits"], axis=0)
    for i in range(DEPTH):
        last = i == DEPTH - 1
        start = lc if last else 0
        sh1, sc1, g1, sh2, sc2, g2 = (jnp.where(is_ctx, m[1], m[0]) for m in jnp.split(mod[i], 6, axis=-1))
        h = _rmsnorm(tok, small["norm_g"][i, 0]) * (1 + sc1) + sh1
        if i == 0:
            y = _ssd_mixer(h, lc, start, big["ssd_w_main"], big["ssd_w_dt"], small["ssd_conv_w"][0], small["ssd_conv_b"][0],
                           small["ssd_dt_bias"][0], small["ssd_a_log"][0], small["ssd_d"][0], small["ssd_norm_g"][0],
                           big["ssd_w_out"])
        elif i == 1:
            y = _ret_mixer(h, lc, start, big["ret_w_in"], small["ret_log_decay"][0], big["ret_w_out"])
        elif i == 2:
            lb = jnp.cumsum(p, axis=0)[i] - p[0]
            y = _hgrn_mixer(h, lc, start, big["hgrn_w_in"], lb, small["hgrn_norm_g"][0], big["hgrn_w_out"])
        else:
            y = _gdn_mixer(h, lc, start, big["gdn_w_main"], big["gdn_w_gate"], small["gdn_conv_w"][0],
                           small["gdn_dt_bias"][0], small["gdn_a_log"][0], small["gdn_norm_g"][0], big["gdn_w_out"])
        if last:
            tok, sc2, sh2, g1, g2 = tok[lc:], sc2[lc:], sh2[lc:], g1[lc:], g2[lc:]
        tok = tok + g1 * y
        h = _rmsnorm(tok, small["norm_g"][i, 1]) * (1 + sc2) + sh2
        hid = jnp.square(jax.nn.relu(pmm(h, big["mlp_w1"][i])))
        tok = tok + g2 * pmm(hid, big["mlp_w2"][i])
    out = _rmsnorm(tok, small["final_g"])
    return 0.5 * jnp.sum(jnp.mean(jnp.square(out - target), axis=-1))


def _rows_of(a, width):
    return a.reshape(-1, width)


def _pad_rows(flat, width, mult=8):
    rows = -(-flat.shape[0] // width)
    rows = -(-rows // mult) * mult
    return jnp.pad(flat, (0, rows * width - flat.shape[0])).reshape(rows, width)


def _unpack_weights(gathered, shard_shapes):
    full, r0 = {}, 0
    for name in BIG:
        shp = shard_shapes[name]
        rows = math.prod(shp) // PACK_W
        blk = gathered[:, r0:r0 + rows].reshape((N_CHIPS,) + shp)
        r0 += rows
        if name in COL_SHARDED:
            full[name] = jnp.moveaxis(blk, 0, 2).reshape(shp[0], shp[1], N_CHIPS * shp[2])
        else:
            full[name] = jnp.moveaxis(blk, 0, 1).reshape(shp[0], N_CHIPS * shp[1], shp[2])
    return full


def _pack_grads(grads, shard_shapes):
    parts = []
    for name in BIG:
        shp = shard_shapes[name]
        g = grads[name]
        if name in COL_SHARDED:
            blk = jnp.moveaxis(g.reshape(shp[0], shp[1], N_CHIPS, shp[2]), 2, 0)
        else:
            blk = jnp.moveaxis(g.reshape(shp[0], N_CHIPS, shp[1], shp[2]), 1, 0)
        parts.append(blk.reshape(N_CHIPS, -1, PACK_W))
    return jnp.concatenate(parts, axis=1)


def _silu_grad(a):
    s = jax.nn.sigmoid(a)
    return s * (1 + a * (1 - s))


def kernel(x, c, ctx, c_ctx, ada_w, ada_b, norm_g, mlp_w1, mlp_w2, final_g, ssd_w_in, ssd_conv_w, ssd_conv_b, ssd_dt_bias, ssd_a_log, ssd_d, ssd_norm_g, ssd_w_out, ret_w_in, ret_log_decay, ret_w_out, hgrn_w_in, hgrn_lb_logits, hgrn_norm_g, hgrn_w_out, gdn_w_in, gdn_conv_w, gdn_dt_bias, gdn_a_log, gdn_norm_g, gdn_w_out, loss_target, m_c_ctx, m_ada_w, m_ada_b, m_norm_g, m_mlp_w1, m_mlp_w2, m_final_g, m_ssd_w_in, m_ssd_conv_w, m_ssd_conv_b, m_ssd_dt_bias, m_ssd_a_log, m_ssd_d, m_ssd_norm_g, m_ssd_w_out, m_ret_w_in, m_ret_log_decay, m_ret_w_out, m_hgrn_w_in, m_hgrn_lb_logits, m_hgrn_norm_g, m_hgrn_w_out, m_gdn_w_in, m_gdn_conv_w, m_gdn_dt_bias, m_gdn_a_log, m_gdn_norm_g, m_gdn_w_out, v_c_ctx, v_ada_w, v_ada_b, v_norm_g, v_mlp_w1, v_mlp_w2, v_final_g, v_ssd_w_in, v_ssd_conv_w, v_ssd_conv_b, v_ssd_dt_bias, v_ssd_a_log, v_ssd_d, v_ssd_norm_g, v_ssd_w_out, v_ret_w_in, v_ret_log_decay, v_ret_w_out, v_hgrn_w_in, v_hgrn_lb_logits, v_hgrn_norm_g, v_hgrn_w_out, v_gdn_w_in, v_gdn_conv_w, v_gdn_dt_bias, v_gdn_a_log, v_gdn_norm_g, v_gdn_w_out):
    env = dict(locals())
    w_loc = {n: env[n] for n in WEIGHTS}
    m_loc = {n: env["m_" + n] for n in WEIGHTS}
    v_loc = {n: env["v_" + n] for n in WEIGHTS}
    chip = 2 * lax.axis_index("x") + lax.axis_index("y")
    core = lax.axis_index("c")
    dev = 2 * chip + core
    d = D_MODEL

    sharded_small = [n for n, ax in SMALL if ax is not None]
    flat1 = jnp.concatenate([c.reshape(-1)] + [w_loc[n].reshape(-1) for n in sharded_small])
    g1 = _small_all_gather(_pad_rows(flat1, SMALL_W)).reshape(N_DEV, -1)
    c_all = g1[:, :d]
    small, off = {n: w_loc[n] for n, ax in SMALL if ax is None}, d
    for n in sharded_small:
        ax = dict(SMALL)[n]
        size = w_loc[n].size
        pieces = [g1[2 * k, off:off + size].reshape(w_loc[n].shape) for k in range(N_CHIPS)]
        small[n] = jnp.concatenate(pieces, axis=ax)
        off += size

    n_sh = ada_w.shape[2]
    cond_in = jnp.concatenate([c_all, c_ctx[None]], axis=0)
    cond = jnp.pad(jax.nn.silu(cond_in), ((0, 16 - N_DEV - 1), (0, 0)))
    ada_b_sh = lax.dynamic_slice_in_dim(ada_b, chip * n_sh, n_sh, axis=1)
    mod_sh = jnp.stack([_mm(cond, ada_w[i], "nn") + ada_b_sh[i] for i in range(DEPTH)])
    g2 = _small_all_gather(mod_sh.reshape(-1, SMALL_W)).reshape(N_DEV, DEPTH, 16, n_sh)
    mod_all = jnp.concatenate([g2[2 * k] for k in range(N_CHIPS)], axis=-1)
    mod_loc = jnp.stack([lax.dynamic_index_in_dim(mod_all, dev, axis=1, keepdims=False), mod_all[:, N_DEV]], axis=1)

    shard_shapes = {n: w_loc[n].shape for n in BIG}
    pack = jnp.concatenate([_rows_of(w_loc[n].astype(bf16), PACK_W) for n in BIG], axis=0)
    full = _unpack_weights(_weight_all_gather(pack), shard_shapes)
    big = {"mlp_w1": tuple(full["mlp_w1"][i] for i in range(DEPTH)),
           "mlp_w2": tuple(full["mlp_w2"][i] for i in range(DEPTH)),
           "ssd_w_main": full["ssd_w_in"][0, :, :SSD_MAIN], "ssd_w_dt": full["ssd_w_in"][0, :, SSD_MAIN:],
           "ssd_w_out": full["ssd_w_out"][0], "ret_w_in": full["ret_w_in"][0], "ret_w_out": full["ret_w_out"][0],
           "hgrn_w_in": full["hgrn_w_in"][0], "hgrn_w_out": full["hgrn_w_out"][0],
           "gdn_w_main": full["gdn_w_in"][0, :, :GDN_MAIN], "gdn_w_gate": full["gdn_w_in"][0, :, GDN_MAIN:],
           "gdn_w_out": full["gdn_w_out"][0]}

    small_diff = {n: small[n] for n, _ in SMALL if n not in ("c_ctx", "ada_b")}
    loss_loc, (gx, gmod, gbig, gsmall) = jax.value_and_grad(_local_loss, argnums=(0, 1, 2, 3))(
        x[0], mod_loc, big, small_diff, ctx[0], loss_target[0])
    loss = lax.psum(loss_loc, ("x", "y", "c"))

    small_names = [n for n, _ in SMALL if n not in ("c_ctx", "ada_b")]
    flat3 = jnp.concatenate([gmod.reshape(-1)] + [gsmall[n].reshape(-1) for n in small_names])
    g3 = _small_all_gather(_pad_rows(flat3, SMALL_W)).reshape(N_DEV, -1)
    gmod_all = g3[:, :gmod.size].reshape(N_DEV, DEPTH, 2, 6 * d)
    ctx_row = gmod_all[0, :, 1]
    for b in range(1, N_DEV):
        ctx_row = ctx_row + gmod_all[b, :, 1]
    dmod = jnp.concatenate([jnp.moveaxis(gmod_all[:, :, 0], 0, 1), ctx_row[:, None],
                            jnp.zeros((DEPTH, 16 - N_DEV - 1, 6 * d), f32)], axis=1)
    grad_small, off = {}, gmod.size
    for n in small_names:
        size = small[n].size
        tot = g3[0, off:off + size]
        for b in range(1, N_DEV):
            tot = tot + g3[b, off:off + size]
        grad_small[n] = tot.reshape(small[n].shape)
        off += size
    grad_small["ada_b"] = jnp.sum(dmod, axis=1)
    dmod_sh = lax.dynamic_slice_in_dim(dmod, chip * n_sh, n_sh, axis=2)
    grad_ada_w = jnp.stack([_mm(cond, dmod_sh[i], "tn") for i in range(DEPTH)])
    dcond = _mm(dmod_sh[0], ada_w[0], "nt")
    for i in range(1, DEPTH):
        dcond = dcond + _mm(dmod_sh[i], ada_w[i], "nt")
    g4 = _small_all_gather(_pad_rows(dcond[N_DEV], SMALL_W)).reshape(N_DEV, -1)[:, :d]
    dcond_ctx = g4[0] + g4[2] + g4[4] + g4[6]
    grad_small["c_ctx"] = dcond_ctx * _silu_grad(c_ctx)
    for n, ax in SMALL:
        if ax is not None:
            width = w_loc[n].shape[ax]
            grad_small[n] = lax.dynamic_slice_in_dim(grad_small[n], chip * width, width, axis=ax)

    gfull = {"mlp_w1": jnp.stack(gbig["mlp_w1"]), "mlp_w2": jnp.stack(gbig["mlp_w2"]),
             "ssd_w_in": jnp.concatenate([gbig["ssd_w_main"], gbig["ssd_w_dt"]], axis=1)[None],
             "ssd_w_out": gbig["ssd_w_out"][None], "ret_w_in": gbig["ret_w_in"][None], "ret_w_out": gbig["ret_w_out"][None],
             "hgrn_w_in": gbig["hgrn_w_in"][None], "hgrn_w_out": gbig["hgrn_w_out"][None],
             "gdn_w_in": jnp.concatenate([gbig["gdn_w_main"], gbig["gdn_w_gate"]], axis=1)[None],
             "gdn_w_out": gbig["gdn_w_out"][None]}
    gp = _pack_grads(gfull, shard_shapes)
    rows = gp.shape[1]
    half = rows // 2
    gp = gp.reshape(N_CHIPS, 2, half, PACK_W)
    mine = lax.dynamic_index_in_dim(gp, core, axis=1, keepdims=False)
    theirs = lax.dynamic_index_in_dim(gp, 1 - core, axis=1, keepdims=False)
    got = _sibling_swap(theirs)
    pair = _add_n([mine.reshape(-1, PACK_W), got.reshape(-1, PACK_W)], bf16).reshape(N_CHIPS, half, PACK_W)
    landed = _chip_exchange(pair)
    own = lax.dynamic_index_in_dim(pair, chip, axis=0, keepdims=False)
    red_half = _add_n([own, landed[0], landed[1], landed[2]], f32)
    other_half = _sibling_swap(red_half)
    red = jnp.where(core == 0, jnp.concatenate([red_half, other_half], axis=0),
                    jnp.concatenate([other_half, red_half], axis=0))
    grad_big, r0 = {}, 0
    for n in BIG:
        nrows = w_loc[n].size // PACK_W
        grad_big[n] = red[r0:r0 + nrows].reshape(w_loc[n].shape)
        r0 += nrows

    grads = dict(grad_big)
    grads.update(grad_small)
    grads["ada_w"] = grad_ada_w
    delta, new_m, new_v = {}, {}, {}
    for n in BIG + ("ada_w",):
        width = w_loc[n].shape[-1]
        outs = _adamw(*[a.reshape(-1, width) for a in (w_loc[n], grads[n], m_loc[n], v_loc[n])])
        delta[n], new_m[n], new_v[n] = (o.reshape(w_loc[n].shape) for o in outs)
    names = [n for n, _ in SMALL]
    packs = [_pad_rows(jnp.concatenate([t[n].reshape(-1) for n in names]), SMALL_W) for t in (w_loc, grads, m_loc, v_loc)]
    outs = _adamw(*packs)
    off = 0
    for n in names:
        size = w_loc[n].size
        delta[n], new_m[n], new_v[n] = (o.reshape(-1)[off:off + size].reshape(w_loc[n].shape) for o in outs)
        off += size

    return (loss, gx[None], *[grads[n] for n in WEIGHTS], *[delta[n] for n in WEIGHTS],
            *[new_m[n] for n in WEIGHTS], *[new_v[n] for n in WEIGHTS])
```

```python
import functools
import math

import jax
import jax.numpy as jnp
from jax import lax
from jax.experimental import pallas as pl
from jax.experimental.pallas import tpu as pltpu

f32 = jnp.float32
bf16 = jnp.bfloat16
MESH = pl.DeviceIdType.MESH

D_MODEL = 2048
DEPTH = 4
GRID_W = 64
CHUNK = 64
SUB = 16
SD_UNITS_PER_STEP = 4
HGRN_HEADS_PER_STEP = 4
GDN_HEADS_PER_STEP = 8
NORM_EPS = 1e-6
ROPE_BASE = 10000.0
N_CHIPS = 4
N_DEV = 8
PACK_W = 2048
SMALL_W = 1024
VMEM_LIMIT = 48 * 1024 * 1024

SSD_D_INNER, SSD_HEADS, SSD_GROUPS, SSD_STATE, SSD_HEAD_DIM = 4096, 64, 8, 128, 64
SSD_CONV_CH = SSD_D_INNER + 2 * SSD_GROUPS * SSD_STATE
SSD_MAIN = SSD_D_INNER + SSD_CONV_CH
RET_HEADS, RET_QK, RET_V, RET_DV = 8, 256, 512, 4096
HGRN_HEADS, HGRN_EXPAND = 16, 128
GDN_K_HEADS, GDN_V_HEADS, GDN_HEAD = 16, 32, 128
GDN_DK, GDN_DV = 2048, 4096
GDN_CONV_CH = 2 * GDN_DK + GDN_DV
GDN_MAIN = GDN_CONV_CH + GDN_DV

ADAM_LR, ADAM_B1, ADAM_B2, ADAM_EPS, ADAM_WD, ADAM_STEP = 0.001, 0.9, 0.999, 1e-08, 0.01, 10

BIG = ("mlp_w1", "mlp_w2", "ssd_w_in", "ssd_w_out", "ret_w_in", "ret_w_out", "hgrn_w_in", "hgrn_w_out",
       "gdn_w_in", "gdn_w_out")
COL_SHARDED = ("mlp_w1", "ssd_w_in", "ret_w_in", "hgrn_w_in", "gdn_w_in")
SMALL = (("c_ctx", None), ("ada_b", None), ("norm_g", 2), ("final_g", None), ("ssd_conv_w", 2), ("ssd_conv_b", None),
         ("ssd_dt_bias", None), ("ssd_a_log", None), ("ssd_d", None), ("ssd_norm_g", None), ("ret_log_decay", None),
         ("hgrn_lb_logits", None), ("hgrn_norm_g", 1), ("gdn_conv_w", 2), ("gdn_dt_bias", None), ("gdn_a_log", None),
         ("gdn_norm_g", None))
WEIGHTS = ("c_ctx", "ada_w", "ada_b", "norm_g", "mlp_w1", "mlp_w2", "final_g", "ssd_w_in", "ssd_conv_w", "ssd_conv_b",
           "ssd_dt_bias", "ssd_a_log", "ssd_d", "ssd_norm_g", "ssd_w_out", "ret_w_in", "ret_log_decay", "ret_w_out",
           "hgrn_w_in", "hgrn_lb_logits", "hgrn_norm_g", "hgrn_w_out", "gdn_w_in", "gdn_conv_w", "gdn_dt_bias",
           "gdn_a_log", "gdn_norm_g", "gdn_w_out")


def _dot(a, b, prec=None):
    return lax.dot_general(a, b, (((1,), (0,)), ((), ())), precision=prec, preferred_element_type=f32)


def _dot_nt(a, b, prec=None):
    return lax.dot_general(a, b, (((1,), (1,)), ((), ())), precision=prec, preferred_element_type=f32)


def _dot_tn(a, b, prec=None):
    return lax.dot_general(a, b, (((0,), (0,)), ((), ())), precision=prec, preferred_element_type=f32)


def _split2(x):
    hi = x.astype(bf16)
    return hi, (x - hi.astype(f32)).astype(bf16)


def _split3(x):
    hi = x.astype(bf16)
    rest = x - hi.astype(f32)
    mid = rest.astype(bf16)
    return hi, mid, (rest - mid.astype(f32)).astype(bf16)


@jax.custom_vjp
def _dot_sel(mask, x):
    hi, mid, lo = _split3(x)
    return _dot(mask, hi) + _dot(mask, mid) + _dot(mask, lo)


def _dot_sel_fwd(mask, x):
    return _dot_sel(mask, x), mask


def _dot_sel_bwd(mask, g):
    hi, mid, lo = _split3(g)
    return jnp.zeros_like(mask), _dot_tn(mask, hi) + _dot_tn(mask, mid) + _dot_tn(mask, lo)


_dot_sel.defvjp(_dot_sel_fwd, _dot_sel_bwd)


def _dot3_with(dot, a, b):
    a_hi, a_lo = _split2(a)
    b_hi, b_lo = _split2(b)
    return dot(a_hi, b_hi) + dot(a_hi, b_lo) + dot(a_lo, b_hi)


@jax.custom_vjp
def _dot3(a, b):
    return _dot3_with(_dot, a, b)


def _dot3_fwd(a, b):
    return _dot3(a, b), (a, b)


def _dot3_bwd(res, g):
    a, b = res
    return _dot3_with(_dot_nt, g, b), _dot3_with(_dot_tn, a, g)


_dot3.defvjp(_dot3_fwd, _dot3_bwd)


def _iota(shape, d):
    return lax.broadcasted_iota(jnp.int32, shape, d)


def _log2(n):
    assert n & (n - 1) == 0
    return n.bit_length() - 1


def _pick(n, cands):
    for t in cands:
        if n % t == 0:
            return t
    return n


MM_VMEM_BUDGET = 36 * 1024 * 1024
MM_STEP_BYTES = 1.2e6
MM_ACC_WEIGHT = 0.15


def _mm_tiles(m, n, k, sa, sb, so):
    best = None
    tms = [t for t in (2304, 2048, 1152, 1024, 768, 576, 512, 384, 256, 128, 64, 32, 16, 8) if m % t == 0] or [m]
    tns = [t for t in (2048, 1024, 512, 256, 128) if n % t == 0] or [n]
    tks = [t for t in (k, 4096, 2048, 1024, 768, 512, 256, 128, 64, 32, 16, 8) if k % t == 0]
    for tm in tms:
        for tn in tns:
            for tk in tks:
                nk = k // tk
                vmem = 2 * (tm * tk * sa + tk * tn * sb + tm * tn * so) + (tm * tn * 4 if nk > 1 else 0)
                if vmem > MM_VMEM_BUDGET:
                    continue
                cost = m * k * sa * (1 if nk == 1 else n // tn) + k * n * sb * (m // tm) + m * n * so
                cost += (m // tm) * (n // tn) * nk * MM_STEP_BYTES
                cost += MM_ACC_WEIGHT * m * n * 8 * nk if nk > 1 else 0
                if best is None or cost < best[0]:
                    best = (cost, tm, tn, tk)
    return best[1:]


def _mm(a, b, mode, out_dtype=f32):
    if mode == "nn":
        (m, k), n = a.shape, b.shape[1]
    elif mode == "nt":
        (m, k), n = a.shape, b.shape[0]
    else:
        (k, m), n = a.shape, b.shape[1]
    tm, tn, tk = _mm_tiles(m, n, k, a.dtype.itemsize, b.dtype.itemsize, jnp.dtype(out_dtype).itemsize)
    nk = k // tk
    dot = {"nn": _dot, "nt": _dot_nt, "tn": _dot_tn}[mode]

    def body(a_ref, b_ref, o_ref, *scratch):
        prod = dot(a_ref[...].astype(bf16), b_ref[...].astype(bf16))
        if nk == 1:
            o_ref[...] = prod.astype(out_dtype)
            return
        acc_ref, = scratch
        kk = pl.program_id(2)

        @pl.when(kk == 0)
        def _():
            acc_ref[...] = jnp.zeros_like(acc_ref)

        acc_ref[...] += prod

        @pl.when(kk == nk - 1)
        def _():
            o_ref[...] = acc_ref[...].astype(out_dtype)

    if mode == "tn":
        a_spec = pl.BlockSpec((tk, tm), lambda i, j, kk: (kk, i))
    else:
        a_spec = pl.BlockSpec((tm, tk), lambda i, j, kk: (i, kk))
    if mode == "nt":
        b_spec = pl.BlockSpec((tn, tk), lambda i, j, kk: (j, kk))
    else:
        b_spec = pl.BlockSpec((tk, tn), lambda i, j, kk: (kk, j))
    return pl.pallas_call(
        body,
        grid=(m // tm, n // tn, nk),
        in_specs=[a_spec, b_spec],
        out_specs=pl.BlockSpec((tm, tn), lambda i, j, kk: (i, j)),
        out_shape=jax.ShapeDtypeStruct((m, n), out_dtype),
        scratch_shapes=[pltpu.VMEM((tm, tn), f32)] if nk > 1 else [],
        compiler_params=pltpu.CompilerParams(dimension_semantics=("parallel", "parallel", "arbitrary"),
                                             vmem_limit_bytes=VMEM_LIMIT),
        name=f"mm_{mode}_{m}x{k}x{n}_{jnp.dtype(out_dtype).name}",
    )(a, b)


@jax.custom_vjp
def pmm(a, w):
    return _mm(a, w, "nn")


def _pmm_fwd(a, w):
    return _mm(a, w, "nn"), (a, w)


def _pmm_bwd(res, g):
    a, w = res
    return _mm(g, w, "nt"), _mm(a, g, "tn", bf16)


pmm.defvjp(_pmm_fwd, _pmm_bwd)


def _eff(i, n, rev):
    return (n - 1 - i) if rev else i


def _sd_chunk(q, k, v, laq, lap, s, *, r, p, rev, ub):
    qc = CHUNK
    n = q.shape[1] // ub
    w = r * p
    units = range(ub)
    tril = (_eff(_iota((qc, qc), 0), qc, rev) >= _eff(_iota((qc, qc), 1), qc, rev)).astype(bf16)
    ones = jnp.ones((qc, qc), bf16)
    ones_n = jnp.ones((n, qc), bf16)
    rowx = _eff(_iota((qc, r * qc), 0), qc, rev)
    colx = _eff(_iota((qc, r * qc), 1) & (qc - 1), qc, rev)
    upper = (rowx <= colx).astype(f32)
    qb = [q[:, u * n:(u + 1) * n].astype(bf16) for u in units]
    kb = [k[:, u * n:(u + 1) * n].astype(bf16) for u in units]
    vf = [v[:, u * w:(u + 1) * w] for u in units]
    lp = [lap[:, u * w:(u + 1) * w] for u in units]
    cumcol = [_dot_sel(tril, laq[u]) for u in units]
    cumrow = [_dot_sel(ones, laq[u] * upper) for u in units]
    decay = [jnp.where(rowx >= colx, jnp.exp(jnp.minimum(cumcol[u] - cumrow[u], 0.0)), 0.0) for u in units]
    ktile = [jnp.concatenate([kb[u]] * r, axis=0) if r > 1 else kb[u] for u in units]
    attn = [(_dot_nt(qb[u], ktile[u]) * decay[u]).astype(bf16) for u in units]
    vt = [a.astype(bf16) for a in vf]
    if r > 1:
        same_head = (_iota((r * qc, w), 0) >> _log2(qc)) == (_iota((r * qc, w), 1) >> _log2(p))
        vt = [jnp.where(same_head, jnp.concatenate([a] * r, axis=0), jnp.zeros((r * qc, w), bf16)) for a in vt]
    cum_p = [_dot_sel(tril, a) for a in lp]
    tot_q = [_dot_sel(ones, a) for a in lp]
    tot_n = [_dot_sel(ones_n, a) for a in lp]
    y = [_dot(attn[u], vt[u]) + jnp.exp(cum_p[u]) * _dot(qb[u], s[u].astype(bf16)) for u in units]
    s_new = [jnp.exp(tot_n[u]) * s[u] + _dot_tn(kb[u], (vf[u] * jnp.exp(tot_q[u] - cum_p[u])).astype(bf16)) for u in units]
    return jnp.concatenate(y, axis=1), jnp.concatenate([a[None] for a in s_new], axis=0)


def _vd_chunk(q, k, v, lf, st, *, rev, hb):
    c = SUB
    cc = c * c
    kd = vd = HGRN_EXPAND
    sh = _log2(c)
    n_sub = CHUNK // c
    rt = _eff(_iota((cc, c), 0) >> sh, c, rev)
    rs = _eff(_iota((cc, c), 0) & (c - 1), c, rev)
    j = _eff(_iota((cc, c), 1), c, rev)
    rep_t = (rt == j).astype(bf16)
    rep_s = rs == j
    rep_s_b, rep_s_f = rep_s.astype(bf16), rep_s.astype(f32)
    rep_t_tr = (_eff(_iota((c, cc), 1) >> sh, c, rev) == _eff(_iota((c, cc), 0), c, rev)).astype(bf16)
    between = ((j > rs) & (j <= rt)).astype(bf16)
    valid = _eff(_iota((cc, kd), 0) >> sh, c, rev) >= _eff(_iota((cc, kd), 0) & (c - 1), c, rev)
    tril = (_eff(_iota((c, c), 0), c, rev) >= _eff(_iota((c, c), 1), c, rev)).astype(bf16)
    ones_c, ones_v = jnp.ones((c, c), bf16), jnp.ones((vd, c), bf16)
    cells = [(h, i) for h in range(hb) for i in range(n_sub)]

    def cut(a, h, i):
        return a[i * c:(i + 1) * c, h * kd:(h + 1) * kd]

    qs = {hi: cut(q, *hi) for hi in cells}
    ks = {hi: cut(k, *hi) for hi in cells}
    lfs = {hi: cut(lf, *hi) for hi in cells}
    vb = {hi: cut(v, *hi).astype(bf16) for hi in cells}
    seg = {hi: _dot_sel(between, lfs[hi]) for hi in cells}
    e = {hi: jnp.where(valid, jnp.exp(seg[hi]), 0.0) for hi in cells}
    z = {hi: _dot(rep_t, qs[hi].astype(bf16)) * _dot(rep_s_b, ks[hi].astype(bf16)) * e[hi] for hi in cells}
    w = {hi: jnp.sum(z[hi], axis=1, keepdims=True) * rep_s_f for hi in cells}
    attn = {hi: _dot_sel(rep_t_tr, w[hi]) for hi in cells}
    cum = {hi: _dot_sel(tril, lfs[hi]) for hi in cells}
    tot = {hi: _dot_sel(ones_c, lfs[hi]) for hi in cells}
    tot_v = {hi: jnp.exp(_dot_sel(ones_v, lfs[hi])) for hi in cells}
    y_in = {hi: _dot(attn[hi].astype(bf16), vb[hi]) for hi in cells}
    q_e = {hi: (qs[hi] * jnp.exp(cum[hi])).astype(bf16) for hi in cells}
    k_e = {hi: (ks[hi] * jnp.exp(tot[hi] - cum[hi])).astype(bf16) for hi in cells}
    sts = [st[h] for h in range(hb)]
    ys = {}
    for i in (range(n_sub - 1, -1, -1) if rev else range(n_sub)):
        for h in range(hb):
            ys[(h, i)] = y_in[(h, i)] + _dot_nt(q_e[(h, i)], sts[h].astype(bf16))
            sts[h] = sts[h] * tot_v[(h, i)] + _dot_tn(vb[(h, i)], k_e[(h, i)])
    y = jnp.concatenate([jnp.concatenate([ys[(h, i)] for i in range(n_sub)], axis=0) for h in range(hb)], axis=1)
    return y, jnp.concatenate([a[None] for a in sts], axis=0)


def _dl_chunk(q, k, v, bq, laq, bk, lak, s, *, rev, hb, rr):
    qc = CHUNK
    kd = GDN_HEAD
    row, col = _eff(_iota((qc, qc), 0), qc, rev), _eff(_iota((qc, qc), 1), qc, rev)
    tril = (row >= col).astype(bf16)
    triu = (row <= col).astype(f32)
    ones = jnp.ones((qc, qc), bf16)
    ones_k = jnp.ones((kd, qc), bf16)
    eye = (row == col).astype(f32)
    heads = range(hb)
    lanes = [slice(j * kd, (j + 1) * kd) for j in heads]
    kf = [k[:, lanes[j // rr]] for j in heads]
    qb = [q[:, lanes[j // rr]].astype(bf16) for j in heads]
    kb = [a.astype(bf16) for a in kf]
    kk = [_dot_nt(kb[j], kb[j]) for j in heads]
    qk = [_dot_nt(qb[j], kb[j]) for j in heads]
    eseg = [jnp.exp(jnp.minimum(_dot_sel(tril, laq[j]) - _dot_sel(ones, laq[j] * triu), 0.0)) for j in heads]
    cum_k = [_dot_sel(tril, lak[:, lanes[j]]) for j in heads]
    tot = [_dot_sel(ones, lak[:, lanes[j]]) for j in heads]
    tot_k = [_dot_sel(ones_k, lak[:, lanes[j]]) for j in heads]
    ecum = [jnp.exp(a) for a in cum_k]
    mp = [-(bq[j] * kk[j] * jnp.where(row > col, eseg[j], 0.0)) for j in heads]
    tinv = [eye + a for a in mp]
    for _ in range(_log2(qc) - 1):
        mp = [_dot3(a, a) for a in mp]
        tinv = [t + _dot3(t, a) for t, a in zip(tinv, mp)]
    u = [_dot3(tinv[j], v[:, lanes[j]] * bk[:, lanes[j]]) for j in heads]
    w = [_dot3(tinv[j], kf[j] * bk[:, lanes[j]] * ecum[j]) for j in heads]
    sb = [s[j].astype(bf16) for j in heads]
    v_new = [u[j] - _dot(w[j].astype(bf16), sb[j]) for j in heads]
    ys = [_dot((qk[j] * jnp.where(row >= col, eseg[j], 0.0)).astype(bf16), v_new[j].astype(bf16)) + ecum[j] * _dot(qb[j], sb[j])
          for j in heads]
    ss = [jnp.exp(tot_k[j]) * s[j] + _dot_tn(kb[j], (v_new[j] * jnp.exp(tot[j] - cum_k[j])).astype(bf16)) for j in heads]
    return jnp.concatenate(ys, axis=1), jnp.concatenate([a[None] for a in ss], axis=0)


def _scan_call(chunk, ins, maps, blocks, state_shape, y_shape, y_block, y_map, units, nc, ncx, rev, name,
               dy=None, sprev=None):
    n = len(ins)
    params = pltpu.CompilerParams(dimension_semantics=("parallel", "arbitrary"), vmem_limit_bytes=VMEM_LIMIT)
    state_block = (None, None) + state_shape
    zeros = (0,) * len(state_shape)

    def chunk_at(pos):
        return jnp.where(pos < ncx, ncx - 1 - pos, nc + ncx - 1 - pos) if rev else pos

    def at(m, pos_of):
        return lambda u, c: m(u, chunk_at(pos_of(c)))

    if dy is None:
        def fwd_body(*refs):
            in_refs, y_ref, sp_ref, s_scr = refs[:n], refs[n], refs[n + 1], refs[n + 2]

            @pl.when(pl.program_id(1) == 0)
            def _():
                s_scr[...] = jnp.zeros_like(s_scr)

            s = s_scr[...]
            sp_ref[...] = s
            y, s_new = chunk(*[r[...] for r in in_refs], s)
            y_ref[...] = y
            s_scr[...] = s_new

        same = lambda c: c
        return pl.pallas_call(
            fwd_body, grid=(units, nc),
            in_specs=[pl.BlockSpec(b, at(m, same)) for b, m in zip(blocks, maps)],
            out_specs=[pl.BlockSpec(y_block, at(y_map, same)), pl.BlockSpec(state_block, lambda u, c: (u, c) + zeros)],
            out_shape=[jax.ShapeDtypeStruct(y_shape, f32), jax.ShapeDtypeStruct((units, nc) + state_shape, f32)],
            scratch_shapes=[pltpu.VMEM(state_shape, f32)],
            compiler_params=params, name=name,
        )(*ins)

    def bwd_body(*refs):
        in_refs, sp_ref, dy_ref = refs[:n], refs[n], refs[n + 1]
        out_refs, ds_scr = refs[n + 2:2 * n + 2], refs[2 * n + 2]

        @pl.when(pl.program_id(1) == 0)
        def _():
            ds_scr[...] = jnp.zeros_like(ds_scr)

        _, vjp = jax.vjp(chunk, *[r[...] for r in in_refs], sp_ref[...])
        grads = vjp((dy_ref[...], ds_scr[...]))
        for r, g in zip(out_refs, grads[:n]):
            r[...] = g
        ds_scr[...] = grads[n]

    back = lambda c: nc - 1 - c
    return pl.pallas_call(
        bwd_body, grid=(units, nc),
        in_specs=[pl.BlockSpec(b, at(m, back)) for b, m in zip(blocks, maps)]
        + [pl.BlockSpec(state_block, lambda u, c: (u, nc - 1 - c) + zeros), pl.BlockSpec(y_block, at(y_map, back))],
        out_specs=[pl.BlockSpec(b, at(m, back)) for b, m in zip(blocks, maps)],
        out_shape=[jax.ShapeDtypeStruct(a.shape, f32) for a in ins],
        scratch_shapes=[pltpu.VMEM(state_shape, f32)],
        compiler_params=params, name=name,
    )(*ins, sprev, dy)


def _scan_name(kind, rev, t, dy):
    return f"{kind}_{'bwd' if dy is not None else 'fwd'}_{'rev' if rev else 'fore'}_{t}"


def _sd_call(g, r, p, rev, ncx, ins, dy=None, sprev=None):
    q, _, v, _, _ = ins
    t = q.shape[0]
    n = q.shape[1] // g
    ub = SD_UNITS_PER_STEP
    im = lambda u, c: (c, u)
    imx = lambda u, c: (u, c, 0)
    bq, bv, bx = (CHUNK, ub * n), (CHUNK, ub * r * p), (ub, CHUNK, r * CHUNK)
    return _scan_call(functools.partial(_sd_chunk, r=r, p=p, rev=rev, ub=ub), list(ins), [im, im, im, imx, im],
                      [bq, bq, bv, bx, bv], (ub, n, r * p), v.shape, bv, im, g // ub, t // CHUNK, ncx, rev,
                      _scan_name(f"sd{n}x{r}x{p}", rev, t, dy), dy, sprev)


def _vd_call(rev, ncx, ins, dy=None, sprev=None):
    q = ins[0]
    t, h = q.shape[0], q.shape[1] // HGRN_EXPAND
    hb = HGRN_HEADS_PER_STEP
    im = lambda u, c: (c, u)
    blk = (CHUNK, hb * HGRN_EXPAND)
    return _scan_call(functools.partial(_vd_chunk, rev=rev, hb=hb), list(ins), [im] * 4, [blk] * 4,
                      (hb, HGRN_EXPAND, HGRN_EXPAND), q.shape, blk, im, h // hb, t // CHUNK, ncx, rev,
                      _scan_name("vd", rev, t, dy), dy, sprev)


def _dl_call(rev, ncx, ins, dy=None, sprev=None):
    q, _, v, _, _, _, _ = ins
    t = q.shape[0]
    hv = v.shape[1] // GDN_HEAD
    rr = hv // (q.shape[1] // GDN_HEAD)
    hb = GDN_HEADS_PER_STEP
    im = lambda u, c: (c, u)
    imx = lambda u, c: (u, c, 0)
    bqk, bv, bx = (CHUNK, hb // rr * GDN_HEAD), (CHUNK, hb * GDN_HEAD), (hb, CHUNK, CHUNK)
    return _scan_call(functools.partial(_dl_chunk, rev=rev, hb=hb, rr=rr), list(ins), [im, im, im, imx, imx, im, im],
                      [bqk, bqk, bv, bx, bx, bv, bv], (hb, GDN_HEAD, GDN_HEAD), v.shape, bv, im, hv // hb, t // CHUNK, ncx, rev,
                      _scan_name("dl", rev, t, dy), dy, sprev)


_SCAN_CALLS = {"ssd": functools.partial(_sd_call, SSD_GROUPS, SSD_HEADS // SSD_GROUPS, SSD_HEAD_DIM),
               "ret": functools.partial(_sd_call, RET_HEADS, 1, RET_V), "hgrn": _vd_call, "gdn": _dl_call}
_SCANS = {}


def _scan(kind, rev, lc):
    key = (kind, rev, lc)
    if key not in _SCANS:
        call = functools.partial(_SCAN_CALLS[kind], rev, lc // CHUNK)

        @jax.custom_vjp
        def scan(*ins):
            return call(ins)[0]

        def fwd(*ins):
            y, sprev = call(ins)
            return y, (ins, sprev)

        def bwd(res, dy):
            ins, sprev = res
            return tuple(call(ins, dy, sprev))

        scan.defvjp(fwd, bwd)
        _SCANS[key] = scan
    return _SCANS[key]


def _place():
    x, y, c = lax.axis_index("x"), lax.axis_index("y"), lax.axis_index("c")
    chips = [(1 - x, y), (x, 1 - y), (1 - x, 1 - y)]
    return x, y, c, chips


def _small_all_gather(block):
    m_per, n = block.shape

    def body(x_ref, out_ref, send_sems, recv_sems, local_sem):
        x, y, c, chips = _place()
        me, sibling = (x, y, c), (x, y, 1 - c)

        def rows(px, py, pc):
            return out_ref.at[pl.ds((4 * px + 2 * py + pc) * m_per, m_per), :]

        def copy(k, blk, to, src=None):
            return pltpu.make_async_remote_copy(
                src_ref=rows(*blk) if src is None else src, dst_ref=rows(*blk),
                send_sem=send_sems.at[k], recv_sem=recv_sems.at[k], device_id=to, device_id_type=MESH)

        mine = pltpu.make_async_copy(x_ref, rows(*me), local_sem)
        mine.start()
        first = [copy(0, me, sibling, src=x_ref)]
        first += [copy(1 + j, me, (*chip, c), src=x_ref) for j, chip in enumerate(chips)]
        for cp in first:
            cp.start()
        passed = [copy(4 + j, (*chip, c), sibling) for j, chip in enumerate(chips)]
        for j, chip in enumerate(chips):
            copy(1 + j, (*chip, c), me).wait_recv()
            passed[j].start()
        copy(0, sibling, me).wait_recv()
        for j, chip in enumerate(chips):
            copy(4 + j, (*chip, 1 - c), me).wait_recv()
        for cp in first + passed:
            cp.wait_send()
        mine.wait()

    return pl.pallas_call(
        body,
        out_shape=jax.ShapeDtypeStruct((N_DEV * m_per, n), block.dtype),
        in_specs=[pl.BlockSpec(memory_space=pltpu.VMEM)],
        out_specs=pl.BlockSpec(memory_space=pltpu.VMEM),
        scratch_shapes=[pltpu.SemaphoreType.DMA((7,)), pltpu.SemaphoreType.DMA((7,)), pltpu.SemaphoreType.DMA],
        compiler_params=pltpu.CompilerParams(vmem_limit_bytes=VMEM_LIMIT),
        name=f"small_all_gather_{m_per}x{n}",
    )(block)


def _any_spec():
    return pl.BlockSpec(memory_space=pl.ANY)


def _weight_all_gather(pack):
    rows, width = pack.shape
    half = rows // 2

    def body(w_ref, out_ref, send_sems, recv_sems, local_sem):
        x, y, c, chips = _place()
        sibling = (x, y, 1 - c)

        def part(px, py, pc):
            return out_ref.at[2 * px + py, pl.ds(pl.multiple_of(pc * half, 16), half), :]

        def copy(k, blk, to, src=None):
            return pltpu.make_async_remote_copy(
                src_ref=part(*blk) if src is None else src, dst_ref=part(*blk),
                send_sem=send_sems.at[k], recv_sem=recv_sems.at[k], device_id=to, device_id_type=MESH)

        mine = pltpu.make_async_copy(w_ref, out_ref.at[2 * x + y], local_sem)
        mine.start()
        my_half = w_ref.at[pl.ds(pl.multiple_of(c * half, 16), half), :]
        first = [copy(j, (x, y, c), (*chip, c), src=my_half) for j, chip in enumerate(chips)]
        for cp in first:
            cp.start()
        passed = [copy(3 + j, (*chip, c), sibling) for j, chip in enumerate(chips)]
        for j, chip in enumerate(chips):
            copy(j, (*chip, c), (x, y, c)).wait_recv()
            passed[j].start()
        for j, chip in enumerate(chips):
            copy(3 + j, (*chip, 1 - c), (x, y, c)).wait_recv()
        for cp in first + passed:
            cp.wait_send()
        mine.wait()

    return pl.pallas_call(
        body,
        out_shape=jax.ShapeDtypeStruct((N_CHIPS, rows, width), pack.dtype),
        in_specs=[_any_spec()], out_specs=_any_spec(),
        scratch_shapes=[pltpu.SemaphoreType.DMA((6,)), pltpu.SemaphoreType.DMA((6,)), pltpu.SemaphoreType.DMA],
        name="weight_all_gather",
    )(pack)


def _sibling_swap(send):
    def body(s_ref, out_ref, send_sem, recv_sem):
        x, y, c, _ = _place()
        cp = pltpu.make_async_remote_copy(src_ref=s_ref, dst_ref=out_ref, send_sem=send_sem, recv_sem=recv_sem,
                                          device_id=(x, y, 1 - c), device_id_type=MESH)
        cp.start()
        cp.wait()

    return pl.pallas_call(
        body, out_shape=jax.ShapeDtypeStruct(send.shape, send.dtype),
        in_specs=[_any_spec()], out_specs=_any_spec(),
        scratch_shapes=[pltpu.SemaphoreType.DMA, pltpu.SemaphoreType.DMA],
        name=f"sibling_swap_{'x'.join(map(str, send.shape))}_{jnp.dtype(send.dtype).name}",
    )(send)


def _sibling_share(mine):
    def body(m_ref, out_ref, send_sem, recv_sem, local_sem):
        x, y, c, _ = _place()
        sibling = (x, y, 1 - c)
        local = pltpu.make_async_copy(m_ref, out_ref.at[c], local_sem)
        local.start()
        give = pltpu.make_async_remote_copy(src_ref=m_ref, dst_ref=out_ref.at[c], send_sem=send_sem, recv_sem=recv_sem,
                                            device_id=sibling, device_id_type=MESH)
        give.start()
        take = pltpu.make_async_remote_copy(src_ref=m_ref, dst_ref=out_ref.at[1 - c], send_sem=send_sem, recv_sem=recv_sem,
                                            device_id=sibling, device_id_type=MESH)
        take.wait_recv()
        give.wait_send()
        local.wait()

    return pl.pallas_call(
        body, out_shape=jax.ShapeDtypeStruct((2,) + mine.shape, mine.dtype),
        in_specs=[_any_spec()], out_specs=_any_spec(),
        scratch_shapes=[pltpu.SemaphoreType.DMA, pltpu.SemaphoreType.DMA, pltpu.SemaphoreType.DMA],
        name="sibling_share",
    )(mine)


def _chip_exchange(parts):
    _, rows, width = parts.shape

    def body(p_ref, out_ref, send_sems, recv_sems):
        x, y, c, chips = _place()
        copies = [pltpu.make_async_remote_copy(
            src_ref=p_ref.at[2 * px + py], dst_ref=out_ref.at[j], send_sem=send_sems.at[j], recv_sem=recv_sems.at[j],
            device_id=(px, py, c), device_id_type=MESH) for j, (px, py) in enumerate(chips)]
        for cp in copies:
            cp.start()
        for cp in copies:
            cp.wait()

    return pl.pallas_call(
        body, out_shape=jax.ShapeDtypeStruct((3, rows, width), parts.dtype),
        in_specs=[_any_spec()], out_specs=_any_spec(),
        scratch_shapes=[pltpu.SemaphoreType.DMA((3,)), pltpu.SemaphoreType.DMA((3,))],
        name="chip_exchange",
    )(parts)


def _row_tile(rows, width, n_arrays):
    budget = VMEM_LIMIT // (4 * 2 * n_arrays * width * 2)
    return _pick(rows, tuple(t for t in (1024, 512, 256, 128, 64, 32, 16, 8) if t <= max(budget, 8)))


def _add_n(arrays, out_dtype):
    rows, width = arrays[0].shape
    n = len(arrays)
    tr = _row_tile(rows, width, n + 1)

    def body(*refs):
        acc = refs[0][...].astype(f32)
        for r in refs[1:n]:
            acc = acc + r[...].astype(f32)
        refs[n][...] = acc.astype(out_dtype)

    spec = pl.BlockSpec((tr, width), lambda i: (i, 0))
    return pl.pallas_call(
        body, grid=(rows // tr,), in_specs=[spec] * n, out_specs=spec,
        out_shape=jax.ShapeDtypeStruct((rows, width), out_dtype),
        compiler_params=pltpu.CompilerParams(dimension_semantics=("parallel",), vmem_limit_bytes=VMEM_LIMIT),
        name=f"add{n}_{rows}x{width}_{jnp.dtype(out_dtype).name}",
    )(*arrays)


def _adamw(w, g, m, v):
    rows, width = w.shape
    tr = _row_tile(rows, width, 7)
    c1 = 1.0 / (1.0 - ADAM_B1 ** ADAM_STEP)
    c2 = 1.0 / (1.0 - ADAM_B2 ** ADAM_STEP)

    def body(w_ref, g_ref, m_ref, v_ref, d_ref, mo_ref, vo_ref):
        gg = g_ref[...]
        m_new = ADAM_B1 * m_ref[...] + (1.0 - ADAM_B1) * gg
        v_new = ADAM_B2 * v_ref[...] + (1.0 - ADAM_B2) * (gg * gg)
        d_ref[...] = -ADAM_LR * ((m_new * c1) / (jnp.sqrt(v_new * c2) + ADAM_EPS) + ADAM_WD * w_ref[...])
        mo_ref[...] = m_new
        vo_ref[...] = v_new

    spec = pl.BlockSpec((tr, width), lambda i: (i, 0))
    sds = jax.ShapeDtypeStruct((rows, width), f32)
    return pl.pallas_call(
        body, grid=(rows // tr,), in_specs=[spec] * 4, out_specs=[spec] * 3, out_shape=[sds] * 3,
        compiler_params=pltpu.CompilerParams(dimension_semantics=("parallel",), vmem_limit_bytes=VMEM_LIMIT),
        name=f"adamw_{rows}x{width}",
    )(w, g, m, v)


def _rmsnorm(x, g):
    return x * lax.rsqrt(jnp.mean(x * x, axis=-1, keepdims=True) + NORM_EPS) * g


def _conv(u, w, lc):
    t = u.shape[0]
    pos = jnp.arange(t)[:, None]
    zero = jnp.zeros((1, u.shape[1]), u.dtype)
    prev = jnp.where((pos == 0) | (pos == lc), 0.0, jnp.concatenate([zero, u[:-1]], axis=0))
    nxt = jnp.where((pos == lc - 1) | (pos == t - 1), 0.0, jnp.concatenate([u[1:], zero], axis=0))
    return w[0] * prev + w[1] * u + w[2] * nxt


def _lanes_q(a):
    return jnp.broadcast_to(a.T[:, :, None], (a.shape[1], a.shape[0], CHUNK))


def _ssd_mixer(h, lc, start, w_main, w_dt, conv_w, conv_b, dt_bias, a_log, d_skip, norm_g, w_out):
    t = h.shape[0]
    r = SSD_HEADS // SSD_GROUPS
    u = pmm(h, w_main)
    dt = pmm(h, w_dt)
    z, xbc = u[:, :SSD_D_INNER], u[:, SSD_D_INNER:]
    xbc = jax.nn.silu(_conv(xbc, conv_w, lc) + conv_b)
    xs = xbc[:, :SSD_D_INNER]
    bm = xbc[:, SSD_D_INNER:SSD_D_INNER + SSD_GROUPS * SSD_STATE]
    cm = xbc[:, SSD_D_INNER + SSD_GROUPS * SSD_STATE:]
    dt = jax.nn.softplus(dt.reshape(t, 2, SSD_HEADS) + dt_bias)
    log_a = -jnp.exp(a_log) * dt
    per_lane = lambda a: jnp.repeat(a, SSD_HEAD_DIM, axis=1)
    y = jnp.repeat(d_skip, SSD_HEAD_DIM) * xs
    for d in (0, 1):
        lap = per_lane(log_a[:, d])
        laq = lap.reshape(t, SSD_GROUPS, r * CHUNK).transpose(1, 0, 2)
        y = y + _scan("ssd", d == 1, lc)(cm, bm, xs * per_lane(dt[:, d]), laq, lap)
    y = y[start:] * jax.nn.silu(z[start:])
    n = t - start
    y = _rmsnorm(y.reshape(n, SSD_GROUPS, -1), norm_g.reshape(SSD_GROUPS, -1)).reshape(n, SSD_D_INNER)
    return pmm(y, w_out)


def _rope(a, rows):
    pos = jnp.arange(rows * GRID_W)
    row = (pos // GRID_W).astype(f32)
    col = (pos % GRID_W).astype(f32)
    half = a.shape[-1] // 2
    inv_freq = ROPE_BASE ** (-jnp.arange(0, half, 2, dtype=f32) / half)

    def rot(u, p):
        ang = p[:, None] * inv_freq
        cos, sin = jnp.cos(ang)[:, None, :], jnp.sin(ang)[:, None, :]
        u1, u2 = jnp.split(u, 2, axis=-1)
        return jnp.concatenate([u1 * cos - u2 * sin, u2 * cos + u1 * sin], axis=-1)

    return jnp.concatenate([rot(a[..., :half], row), rot(a[..., half:], col)], axis=-1)


def _ret_mixer(h, lc, start, w_in, log_decay, w_out):
    t = h.shape[0]
    u = pmm(h, w_in)
    q = u[:, :D_MODEL].reshape(t, RET_HEADS, RET_QK)
    k = u[:, D_MODEL:2 * D_MODEL].reshape(t, RET_HEADS, RET_QK) * RET_QK ** -0.5
    v = u[:, 2 * D_MODEL:2 * D_MODEL + RET_DV]
    gate = u[:, 2 * D_MODEL + RET_DV:]
    rows = (t - lc) // GRID_W
    q = jnp.concatenate([q[:lc], _rope(q[lc:], rows)], axis=0).reshape(t, D_MODEL)
    k = jnp.concatenate([k[:lc], _rope(k[lc:], rows)], axis=0).reshape(t, D_MODEL)
    y = 0.0
    for d in (0, 1):
        laq = jnp.broadcast_to(log_decay[d][:, None, None], (RET_HEADS, t, CHUNK))
        lap = jnp.broadcast_to(jnp.repeat(log_decay[d], RET_V)[None, :], (t, RET_DV))
        y = y + _scan("ret", d == 1, lc)(q, k, v, laq, lap)
    n = t - start
    y = y[start:].reshape(n, RET_HEADS, RET_V)
    mu = jnp.mean(y, axis=-1, keepdims=True)
    var = jnp.mean(jnp.square(y - mu), axis=-1, keepdims=True)
    y = ((y - mu) * lax.rsqrt(var + NORM_EPS)).reshape(n, RET_DV) * jax.nn.silu(gate[start:])
    return pmm(y, w_out)


def _hgrn_mixer(h, lc, start, w_in, lb, norm_g, w_out):
    t = h.shape[0]
    u = pmm(h, w_in)
    q, f_f, f_b, inp, gate = (u[:, i * D_MODEL:(i + 1) * D_MODEL] for i in range(5))

    def gates(f):
        log_f = jnp.logaddexp(jnp.log(lb), jnp.log1p(-lb) + jax.nn.log_sigmoid(f))
        return log_f, (1 - lb) * jax.nn.sigmoid(-f)

    lf_f, k_f = gates(f_f)
    lf_b, k_b = gates(f_b)
    y = _scan("hgrn", False, lc)(q, k_f, inp, lf_f) + _scan("hgrn", True, lc)(q, k_b, inp, lf_b)
    n = t - start
    y = _rmsnorm(y[start:].reshape(n, HGRN_HEADS, HGRN_EXPAND), norm_g.reshape(HGRN_HEADS, HGRN_EXPAND))
    y = y.reshape(n, D_MODEL) * jax.nn.silu(gate[start:])
    return pmm(y, w_out)


def _l2norm(a):
    return a * lax.rsqrt(jnp.sum(a * a, axis=-1, keepdims=True) + 1e-6)


def _gdn_mixer(h, lc, start, w_main, w_gate, conv_w, dt_bias, a_log, norm_g, w_out):
    t = h.shape[0]
    u = pmm(h, w_main)
    ba = pmm(h, w_gate)
    qkv = jax.nn.silu(_conv(u[:, :GDN_CONV_CH], conv_w, lc))
    z = u[:, GDN_CONV_CH:]
    q = (_l2norm(qkv[:, :GDN_DK].reshape(t, GDN_K_HEADS, GDN_HEAD)) * GDN_HEAD ** -0.5).reshape(t, GDN_DK)
    k = _l2norm(qkv[:, GDN_DK:2 * GDN_DK].reshape(t, GDN_K_HEADS, GDN_HEAD)).reshape(t, GDN_DK)
    v = qkv[:, 2 * GDN_DK:]
    beta = jax.nn.sigmoid(ba[:, :2 * GDN_V_HEADS].reshape(t, 2, GDN_V_HEADS))
    log_a = -jnp.exp(a_log) * jax.nn.softplus(ba[:, 2 * GDN_V_HEADS:].reshape(t, 2, GDN_V_HEADS) + dt_bias)
    per_lane = lambda a: jnp.repeat(a, GDN_HEAD, axis=1)
    y = 0.0
    for d in (0, 1):
        y = y + _scan("gdn", d == 1, lc)(q, k, v, _lanes_q(beta[:, d]), _lanes_q(log_a[:, d]),
                                         per_lane(beta[:, d]), per_lane(log_a[:, d]))
    n = t - start
    y = _rmsnorm(y[start:].reshape(n, GDN_V_HEADS, GDN_HEAD), norm_g) * jax.nn.silu(
        z[start:].reshape(n, GDN_V_HEADS, GDN_HEAD))
    return pmm(y.reshape(n, GDN_DV), w_out)


def _local_loss(x, mod, big, small, ctx, target):
    lc = ctx.shape[0]
    tok = jnp.concatenate([ctx, x], axis=0)
    t = tok.shape[0]
    is_ctx = (jnp.arange(t) < lc)[:, None]
    p = jax.nn.softmax(small["hgrn_lb_logits"], axis=0)
    for i in range(DEPTH):
        last = i == DEPTH - 1
        start = lc if last else 0
        sh1, sc1, g1, sh2, sc2, g2 = (jnp.where(is_ctx, m[1], m[0]) for m in jnp.split(mod[i], 6, axis=-1))
        h = _rmsnorm(tok, small["norm_g"][i, 0]) * (1 + sc1) + sh1
        if i == 0:
            y = _ssd_mixer(h, lc, start, big["ssd_w_main"], big["ssd_w_dt"], small["ssd_conv_w"][0], small["ssd_conv_b"][0],
                           small["ssd_dt_bias"][0], small["ssd_a_log"][0], small["ssd_d"][0], small["ssd_norm_g"][0],
                           big["ssd_w_out"])
        elif i == 1:
            y = _ret_mixer(h, lc, start, big["ret_w_in"], small["ret_log_decay"][0], big["ret_w_out"])
        elif i == 2:
            lb = jnp.cumsum(p, axis=0)[i] - p[0]
            y = _hgrn_mixer(h, lc, start, big["hgrn_w_in"], lb, small["hgrn_norm_g"][0], big["hgrn_w_out"])
        else:
            y = _gdn_mixer(h, lc, start, big["gdn_w_main"], big["gdn_w_gate"], small["gdn_conv_w"][0],
                           small["gdn_dt_bias"][0], small["gdn_a_log"][0], small["gdn_norm_g"][0], big["gdn_w_out"])
        if last:
            tok, sc2, sh2, g1, g2 = tok[lc:], sc2[lc:], sh2[lc:], g1[lc:], g2[lc:]
        tok = tok + g1 * y
        h = _rmsnorm(tok, small["norm_g"][i, 1]) * (1 + sc2) + sh2
        hid = jnp.square(jax.nn.relu(pmm(h, big["mlp_w1"][i])))
        tok = tok + g2 * pmm(hid, big["mlp_w2"][i])
    out = _rmsnorm(tok, small["final_g"])
    return 0.5 * jnp.sum(jnp.mean(jnp.square(out - target), axis=-1))


def _rows_of(a, width):
    return a.reshape(-1, width)


def _pad_rows(flat, width, mult=8):
    rows = -(-flat.shape[0] // width)
    rows = -(-rows // mult) * mult
    return jnp.pad(flat, (0, rows * width - flat.shape[0])).reshape(rows, width)


def _unpack_weights(gathered, shard_shapes):
    full, r0 = {}, 0
    for name in BIG:
        shp = shard_shapes[name]
        rows = math.prod(shp) // PACK_W
        blk = gathered[:, r0:r0 + rows].reshape((N_CHIPS,) + shp)
        r0 += rows
        if name in COL_SHARDED:
            full[name] = jnp.moveaxis(blk, 0, 2).reshape(shp[0], shp[1], N_CHIPS * shp[2])
        else:
            full[name] = jnp.moveaxis(blk, 0, 1).reshape(shp[0], N_CHIPS * shp[1], shp[2])
    return full


def _pack_grads(grads, shard_shapes):
    parts = []
    for name in BIG:
        shp = shard_shapes[name]
        g = grads[name]
        if name in COL_SHARDED:
            blk = jnp.moveaxis(g.reshape(shp[0], shp[1], N_CHIPS, shp[2]), 2, 0)
        else:
            blk = jnp.moveaxis(g.reshape(shp[0], N_CHIPS, shp[1], shp[2]), 1, 0)
        parts.append(blk.reshape(N_CHIPS, -1, PACK_W))
    return jnp.concatenate(parts, axis=1)


def _silu_grad(a):
    s = jax.nn.sigmoid(a)
    return s * (1 + a * (1 - s))


def kernel(x, c, ctx, c_ctx, ada_w, ada_b, norm_g, mlp_w1, mlp_w2, final_g, ssd_w_in, ssd_conv_w, ssd_conv_b, ssd_dt_bias, ssd_a_log, ssd_d, ssd_norm_g, ssd_w_out, ret_w_in, ret_log_decay, ret_w_out, hgrn_w_in, hgrn_lb_logits, hgrn_norm_g, hgrn_w_out, gdn_w_in, gdn_conv_w, gdn_dt_bias, gdn_a_log, gdn_norm_g, gdn_w_out, loss_target, m_c_ctx, m_ada_w, m_ada_b, m_norm_g, m_mlp_w1, m_mlp_w2, m_final_g, m_ssd_w_in, m_ssd_conv_w, m_ssd_conv_b, m_ssd_dt_bias, m_ssd_a_log, m_ssd_d, m_ssd_norm_g, m_ssd_w_out, m_ret_w_in, m_ret_log_decay, m_ret_w_out, m_hgrn_w_in, m_hgrn_lb_logits, m_hgrn_norm_g, m_hgrn_w_out, m_gdn_w_in, m_gdn_conv_w, m_gdn_dt_bias, m_gdn_a_log, m_gdn_norm_g, m_gdn_w_out, v_c_ctx, v_ada_w, v_ada_b, v_norm_g, v_mlp_w1, v_mlp_w2, v_final_g, v_ssd_w_in, v_ssd_conv_w, v_ssd_conv_b, v_ssd_dt_bias, v_ssd_a_log, v_ssd_d, v_ssd_norm_g, v_ssd_w_out, v_ret_w_in, v_ret_log_decay, v_ret_w_out, v_hgrn_w_in, v_hgrn_lb_logits, v_hgrn_norm_g, v_hgrn_w_out, v_gdn_w_in, v_gdn_conv_w, v_gdn_dt_bias, v_gdn_a_log, v_gdn_norm_g, v_gdn_w_out):
    env = dict(locals())
    w_loc = {n: env[n] for n in WEIGHTS}
    m_loc = {n: env["m_" + n] for n in WEIGHTS}
    v_loc = {n: env["v_" + n] for n in WEIGHTS}
    chip = 2 * lax.axis_index("x") + lax.axis_index("y")
    core = lax.axis_index("c")
    dev = 2 * chip + core
    d = D_MODEL

    sharded_small = [n for n, ax in SMALL if ax is not None]
    flat1 = jnp.concatenate([c.reshape(-1)] + [w_loc[n].reshape(-1) for n in sharded_small])
    g1 = _small_all_gather(_pad_rows(flat1, SMALL_W)).reshape(N_DEV, -1)
    c_all = g1[:, :d]
    small, off = {n: w_loc[n] for n, ax in SMALL if ax is None}, d
    for n in sharded_small:
        ax = dict(SMALL)[n]
        size = w_loc[n].size
        pieces = [g1[2 * k, off:off + size].reshape(w_loc[n].shape) for k in range(N_CHIPS)]
        small[n] = jnp.concatenate(pieces, axis=ax)
        off += size

    n_sh = ada_w.shape[2]
    cond_in = jnp.concatenate([c_all, c_ctx[None]], axis=0)
    cond = jnp.pad(jax.nn.silu(cond_in), ((0, 16 - N_DEV - 1), (0, 0)))
    ada_b_sh = lax.dynamic_slice_in_dim(ada_b, chip * n_sh, n_sh, axis=1)
    mod_sh = jnp.stack([_mm(cond, ada_w[i], "nn") + ada_b_sh[i] for i in range(DEPTH)])
    g2 = _small_all_gather(mod_sh.reshape(-1, SMALL_W)).reshape(N_DEV, DEPTH, 16, n_sh)
    mod_all = jnp.concatenate([g2[2 * k] for k in range(N_CHIPS)], axis=-1)
    mod_loc = jnp.stack([lax.dynamic_index_in_dim(mod_all, dev, axis=1, keepdims=False), mod_all[:, N_DEV]], axis=1)

    shard_shapes = {n: w_loc[n].shape for n in BIG}
    pack = jnp.concatenate([_rows_of(w_loc[n].astype(bf16), PACK_W) for n in BIG], axis=0)
    full = _unpack_weights(_weight_all_gather(pack), shard_shapes)
    big = {"mlp_w1": tuple(full["mlp_w1"][i] for i in range(DEPTH)),
           "mlp_w2": tuple(full["mlp_w2"][i] for i in range(DEPTH)),
           "ssd_w_main": full["ssd_w_in"][0, :, :SSD_MAIN], "ssd_w_dt": full["ssd_w_in"][0, :, SSD_MAIN:],
           "ssd_w_out": full["ssd_w_out"][0], "ret_w_in": full["ret_w_in"][0], "ret_w_out": full["ret_w_out"][0],
           "hgrn_w_in": full["hgrn_w_in"][0], "hgrn_w_out": full["hgrn_w_out"][0],
           "gdn_w_main": full["gdn_w_in"][0, :, :GDN_MAIN], "gdn_w_gate": full["gdn_w_in"][0, :, GDN_MAIN:],
           "gdn_w_out": full["gdn_w_out"][0]}

    small_diff = {n: small[n] for n, _ in SMALL if n not in ("c_ctx", "ada_b")}
    loss_loc, (gx, gmod, gbig, gsmall) = jax.value_and_grad(_local_loss, argnums=(0, 1, 2, 3))(
        x[0], mod_loc, big, small_diff, ctx[0], loss_target[0])
    loss = lax.psum(loss_loc, ("x", "y", "c"))

    small_names = [n for n, _ in SMALL if n not in ("c_ctx", "ada_b")]
    flat3 = jnp.concatenate([gmod.reshape(-1)] + [gsmall[n].reshape(-1) for n in small_names])
    g3 = _small_all_gather(_pad_rows(flat3, SMALL_W)).reshape(N_DEV, -1)
    gmod_all = g3[:, :gmod.size].reshape(N_DEV, DEPTH, 2, 6 * d)
    ctx_row = gmod_all[0, :, 1]
    for b in range(1, N_DEV):
        ctx_row = ctx_row + gmod_all[b, :, 1]
    dmod = jnp.concatenate([jnp.moveaxis(gmod_all[:, :, 0], 0, 1), ctx_row[:, None],
                            jnp.zeros((DEPTH, 16 - N_DEV - 1, 6 * d), f32)], axis=1)
    grad_small, off = {}, gmod.size
    for n in small_names:
        size = small[n].size
        tot = g3[0, off:off + size]
        for b in range(1, N_DEV):
            tot = tot + g3[b, off:off + size]
        grad_small[n] = tot.reshape(small[n].shape)
        off += size
    grad_small["ada_b"] = jnp.sum(dmod, axis=1)
    dmod_sh = lax.dynamic_slice_in_dim(dmod, chip * n_sh, n_sh, axis=2)
    grad_ada_w = jnp.stack([_mm(cond, dmod_sh[i], "tn") for i in range(DEPTH)])
    dcond = _mm(dmod_sh[0], ada_w[0], "nt")
    for i in range(1, DEPTH):
        dcond = dcond + _mm(dmod_sh[i], ada_w[i], "nt")
    g4 = _small_all_gather(_pad_rows(dcond[N_DEV], SMALL_W)).reshape(N_DEV, -1)[:, :d]
    dcond_ctx = g4[0] + g4[2] + g4[4] + g4[6]
    grad_small["c_ctx"] = dcond_ctx * _silu_grad(c_ctx)
    for n, ax in SMALL:
        if ax is not None:
            width = w_loc[n].shape[ax]
            grad_small[n] = lax.dynamic_slice_in_dim(grad_small[n], chip * width, width, axis=ax)

    gfull = {"mlp_w1": jnp.stack(gbig["mlp_w1"]), "mlp_w2": jnp.stack(gbig["mlp_w2"]),
             "ssd_w_in": jnp.concatenate([gbig["ssd_w_main"], gbig["ssd_w_dt"]], axis=1)[None],
             "ssd_w_out": gbig["ssd_w_out"][None], "ret_w_in": gbig["ret_w_in"][None], "ret_w_out": gbig["ret_w_out"][None],
             "hgrn_w_in": gbig["hgrn_w_in"][None], "hgrn_w_out": gbig["hgrn_w_out"][None],
             "gdn_w_in": jnp.concatenate([gbig["gdn_w_main"], gbig["gdn_w_gate"]], axis=1)[None],
             "gdn_w_out": gbig["gdn_w_out"][None]}
    gp = _pack_grads(gfull, shard_shapes)
    rows = gp.shape[1]
    half = rows // 2
    gp = gp.reshape(N_CHIPS, 2, half, PACK_W)
    mine = lax.dynamic_index_in_dim(gp, core, axis=1, keepdims=False)
    theirs = lax.dynamic_index_in_dim(gp, 1 - core, axis=1, keepdims=False)
    got = _sibling_swap(theirs)
    pair = _add_n([mine.reshape(-1, PACK_W), got.reshape(-1, PACK_W)], bf16).reshape(N_CHIPS, half, PACK_W)
    landed = _chip_exchange(pair)
    own = lax.dynamic_index_in_dim(pair, chip, axis=0, keepdims=False)
    red_half = _add_n([own, landed[0], landed[1], landed[2]], f32)
    red = _sibling_share(red_half).reshape(rows, PACK_W)
    grad_big, r0 = {}, 0
    for n in BIG:
        nrows = w_loc[n].size // PACK_W
        grad_big[n] = red[r0:r0 + nrows].reshape(w_loc[n].shape)
        r0 += nrows

    grads = dict(grad_big)
    grads.update(grad_small)
    grads["ada_w"] = grad_ada_w
    delta, new_m, new_v = {}, {}, {}
    for n in BIG + ("ada_w",):
        width = w_loc[n].shape[-1]
        outs = _adamw(*[a.reshape(-1, width) for a in (w_loc[n], grads[n], m_loc[n], v_loc[n])])
        delta[n], new_m[n], new_v[n] = (o.reshape(w_loc[n].shape) for o in outs)
    names = [n for n, _ in SMALL]
    packs = [_pad_rows(jnp.concatenate([t[n].reshape(-1) for n in names]), SMALL_W) for t in (w_loc, grads, m_loc, v_loc)]
    outs = _adamw(*packs)
    off = 0
    for n in names:
        size = w_loc[n].size
        delta[n], new_m[n], new_v[n] = (o.reshape(-1)[off:off + size].reshape(w_loc[n].shape) for o in outs)
        off += size

    return (loss, gx[None], *[grads[n] for n in WEIGHTS], *[delta[n] for n in WEIGHTS],
            *[new_m[n] for n in WEIGHTS], *[new_v[n] for n in WEIGHTS])
```

```python
import functools
import math

import jax
import jax.numpy as jnp
from jax import lax
from jax.experimental import pallas as pl
from jax.experimental.pallas import tpu as pltpu

f32 = jnp.float32
bf16 = jnp.bfloat16
MESH = pl.DeviceIdType.MESH

D_MODEL = 2048
DEPTH = 4
GRID_W = 64
CHUNK = 64
SUB = 16
SD_UNITS_PER_STEP = 4
HGRN_HEADS_PER_STEP = 4
GDN_HEADS_PER_STEP = 8
NORM_EPS = 1e-6
ROPE_BASE = 10000.0
N_CHIPS = 4
N_DEV = 8
PACK_W = 2048
SMALL_W = 1024
VMEM_LIMIT = 48 * 1024 * 1024

SSD_D_INNER, SSD_HEADS, SSD_GROUPS, SSD_STATE, SSD_HEAD_DIM = 4096, 64, 8, 128, 64
SSD_CONV_CH = SSD_D_INNER + 2 * SSD_GROUPS * SSD_STATE
SSD_MAIN = SSD_D_INNER + SSD_CONV_CH
RET_HEADS, RET_QK, RET_V, RET_DV = 8, 256, 512, 4096
HGRN_HEADS, HGRN_EXPAND = 16, 128
GDN_K_HEADS, GDN_V_HEADS, GDN_HEAD = 16, 32, 128
GDN_DK, GDN_DV = 2048, 4096
GDN_CONV_CH = 2 * GDN_DK + GDN_DV
GDN_MAIN = GDN_CONV_CH + GDN_DV

ADAM_LR, ADAM_B1, ADAM_B2, ADAM_EPS, ADAM_WD, ADAM_STEP = 0.001, 0.9, 0.999, 1e-08, 0.01, 10

BIG = ("mlp_w1", "mlp_w2", "ssd_w_in", "ssd_w_out", "ret_w_in", "ret_w_out", "hgrn_w_in", "hgrn_w_out",
       "gdn_w_in", "gdn_w_out")
COL_SHARDED = ("mlp_w1", "ssd_w_in", "ret_w_in", "hgrn_w_in", "gdn_w_in")
SMALL = (("c_ctx", None), ("ada_b", None), ("norm_g", 2), ("final_g", None), ("ssd_conv_w", 2), ("ssd_conv_b", None),
         ("ssd_dt_bias", None), ("ssd_a_log", None), ("ssd_d", None), ("ssd_norm_g", None), ("ret_log_decay", None),
         ("hgrn_lb_logits", None), ("hgrn_norm_g", 1), ("gdn_conv_w", 2), ("gdn_dt_bias", None), ("gdn_a_log", None),
         ("gdn_norm_g", None))
WEIGHTS = ("c_ctx", "ada_w", "ada_b", "norm_g", "mlp_w1", "mlp_w2", "final_g", "ssd_w_in", "ssd_conv_w", "ssd_conv_b",
           "ssd_dt_bias", "ssd_a_log", "ssd_d", "ssd_norm_g", "ssd_w_out", "ret_w_in", "ret_log_decay", "ret_w_out",
           "hgrn_w_in", "hgrn_lb_logits", "hgrn_norm_g", "hgrn_w_out", "gdn_w_in", "gdn_conv_w", "gdn_dt_bias",
           "gdn_a_log", "gdn_norm_g", "gdn_w_out")


def _dot(a, b, prec=None):
    return lax.dot_general(a, b, (((1,), (0,)), ((), ())), precision=prec, preferred_element_type=f32)


def _dot_nt(a, b, prec=None):
    return lax.dot_general(a, b, (((1,), (1,)), ((), ())), precision=prec, preferred_element_type=f32)


def _dot_tn(a, b, prec=None):
    return lax.dot_general(a, b, (((0,), (0,)), ((), ())), precision=prec, preferred_element_type=f32)


def _split2(x):
    hi = x.astype(bf16)
    return hi, (x - hi.astype(f32)).astype(bf16)


def _split3(x):
    hi = x.astype(bf16)
    rest = x - hi.astype(f32)
    mid = rest.astype(bf16)
    return hi, mid, (rest - mid.astype(f32)).astype(bf16)


@jax.custom_vjp
def _dot_sel(mask, x):
    hi, mid, lo = _split3(x)
    return _dot(mask, hi) + _dot(mask, mid) + _dot(mask, lo)


def _dot_sel_fwd(mask, x):
    return _dot_sel(mask, x), mask


def _dot_sel_bwd(mask, g):
    hi, mid, lo = _split3(g)
    return jnp.zeros_like(mask), _dot_tn(mask, hi) + _dot_tn(mask, mid) + _dot_tn(mask, lo)


_dot_sel.defvjp(_dot_sel_fwd, _dot_sel_bwd)


def _dot3_with(dot, a, b):
    a_hi, a_lo = _split2(a)
    b_hi, b_lo = _split2(b)
    return dot(a_hi, b_hi) + dot(a_hi, b_lo) + dot(a_lo, b_hi)


@jax.custom_vjp
def _dot3(a, b):
    return _dot3_with(_dot, a, b)


def _dot3_fwd(a, b):
    return _dot3(a, b), (a, b)


def _dot3_bwd(res, g):
    a, b = res
    return _dot3_with(_dot_nt, g, b), _dot3_with(_dot_tn, a, g)


_dot3.defvjp(_dot3_fwd, _dot3_bwd)


def _iota(shape, d):
    return lax.broadcasted_iota(jnp.int32, shape, d)


def _log2(n):
    assert n & (n - 1) == 0
    return n.bit_length() - 1


def _pick(n, cands):
    for t in cands:
        if n % t == 0:
            return t
    return n


MM_VMEM_BUDGET = 36 * 1024 * 1024
MM_STEP_BYTES = 1.2e6
MM_ACC_WEIGHT = 0.15


def _mm_tiles(m, n, k, sa, sb, so):
    best = None
    tms = [t for t in (2304, 2048, 1152, 1024, 768, 576, 512, 384, 256, 128, 64, 32, 16, 8) if m % t == 0] or [m]
    tns = [t for t in (2048, 1024, 512, 256, 128) if n % t == 0] or [n]
    tks = [t for t in (k, 4096, 2048, 1024, 768, 512, 256, 128, 64, 32, 16, 8) if k % t == 0]
    for tm in tms:
        for tn in tns:
            for tk in tks:
                nk = k // tk
                vmem = 2 * (tm * tk * sa + tk * tn * sb + tm * tn * so) + (tm * tn * 4 if nk > 1 else 0)
                if vmem > MM_VMEM_BUDGET:
                    continue
                cost = m * k * sa * (1 if nk == 1 else n // tn) + k * n * sb * (m // tm) + m * n * so
                cost += (m // tm) * (n // tn) * nk * MM_STEP_BYTES
                cost += MM_ACC_WEIGHT * m * n * 8 * nk if nk > 1 else 0
                if best is None or cost < best[0]:
                    best = (cost, tm, tn, tk)
    return best[1:]


def _mm(a, b, mode, out_dtype=f32):
    if mode == "nn":
        (m, k), n = a.shape, b.shape[1]
    elif mode == "nt":
        (m, k), n = a.shape, b.shape[0]
    else:
        (k, m), n = a.shape, b.shape[1]
    tm, tn, tk = _mm_tiles(m, n, k, a.dtype.itemsize, b.dtype.itemsize, jnp.dtype(out_dtype).itemsize)
    nk = k // tk
    dot = {"nn": _dot, "nt": _dot_nt, "tn": _dot_tn}[mode]

    def body(a_ref, b_ref, o_ref, *scratch):
        prod = dot(a_ref[...].astype(bf16), b_ref[...].astype(bf16))
        if nk == 1:
            o_ref[...] = prod.astype(out_dtype)
            return
        acc_ref, = scratch
        kk = pl.program_id(2)

        @pl.when(kk == 0)
        def _():
            acc_ref[...] = jnp.zeros_like(acc_ref)

        acc_ref[...] += prod

        @pl.when(kk == nk - 1)
        def _():
            o_ref[...] = acc_ref[...].astype(out_dtype)

    if mode == "tn":
        a_spec = pl.BlockSpec((tk, tm), lambda i, j, kk: (kk, i))
    else:
        a_spec = pl.BlockSpec((tm, tk), lambda i, j, kk: (i, kk))
    if mode == "nt":
        b_spec = pl.BlockSpec((tn, tk), lambda i, j, kk: (j, kk))
    else:
        b_spec = pl.BlockSpec((tk, tn), lambda i, j, kk: (kk, j))
    return pl.pallas_call(
        body,
        grid=(m // tm, n // tn, nk),
        in_specs=[a_spec, b_spec],
        out_specs=pl.BlockSpec((tm, tn), lambda i, j, kk: (i, j)),
        out_shape=jax.ShapeDtypeStruct((m, n), out_dtype),
        scratch_shapes=[pltpu.VMEM((tm, tn), f32)] if nk > 1 else [],
        compiler_params=pltpu.CompilerParams(dimension_semantics=("parallel", "parallel", "arbitrary"),
                                             vmem_limit_bytes=VMEM_LIMIT),
        name=f"mm_{mode}_{m}x{k}x{n}_{jnp.dtype(out_dtype).name}",
    )(a, b)


@jax.custom_vjp
def pmm(a, w):
    return _mm(a, w, "nn")


def _pmm_fwd(a, w):
    return _mm(a, w, "nn"), (a, w)


def _pmm_bwd(res, g):
    a, w = res
    return _mm(g, w, "nt"), _mm(a, g, "tn", bf16)


pmm.defvjp(_pmm_fwd, _pmm_bwd)


def _eff(i, n, rev):
    return (n - 1 - i) if rev else i


def _sd_chunk(q, k, v, laq, lap, s, *, r, p, rev, ub):
    qc = CHUNK
    n = q.shape[1] // ub
    w = r * p
    units = range(ub)
    tril = (_eff(_iota((qc, qc), 0), qc, rev) >= _eff(_iota((qc, qc), 1), qc, rev)).astype(bf16)
    ones = jnp.ones((qc, qc), bf16)
    ones_n = jnp.ones((n, qc), bf16)
    rowx = _eff(_iota((qc, r * qc), 0), qc, rev)
    colx = _eff(_iota((qc, r * qc), 1) & (qc - 1), qc, rev)
    upper = (rowx <= colx).astype(f32)
    qb = [q[:, u * n:(u + 1) * n].astype(bf16) for u in units]
    kb = [k[:, u * n:(u + 1) * n].astype(bf16) for u in units]
    vf = [v[:, u * w:(u + 1) * w] for u in units]
    lp = [lap[:, u * w:(u + 1) * w] for u in units]
    cumcol = [_dot_sel(tril, laq[u]) for u in units]
    cumrow = [_dot_sel(ones, laq[u] * upper) for u in units]
    decay = [jnp.where(rowx >= colx, jnp.exp(jnp.minimum(cumcol[u] - cumrow[u], 0.0)), 0.0) for u in units]
    ktile = [jnp.concatenate([kb[u]] * r, axis=0) if r > 1 else kb[u] for u in units]
    attn = [(_dot_nt(qb[u], ktile[u]) * decay[u]).astype(bf16) for u in units]
    vt = [a.astype(bf16) for a in vf]
    if r > 1:
        same_head = (_iota((r * qc, w), 0) >> _log2(qc)) == (_iota((r * qc, w), 1) >> _log2(p))
        vt = [jnp.where(same_head, jnp.concatenate([a] * r, axis=0), jnp.zeros((r * qc, w), bf16)) for a in vt]
    cum_p = [_dot_sel(tril, a) for a in lp]
    tot_q = [_dot_sel(ones, a) for a in lp]
    tot_n = [_dot_sel(ones_n, a) for a in lp]
    y = [_dot(attn[u], vt[u]) + jnp.exp(cum_p[u]) * _dot(qb[u], s[u].astype(bf16)) for u in units]
    s_new = [jnp.exp(tot_n[u]) * s[u] + _dot_tn(kb[u], (vf[u] * jnp.exp(tot_q[u] - cum_p[u])).astype(bf16)) for u in units]
    return jnp.concatenate(y, axis=1), jnp.concatenate([a[None] for a in s_new], axis=0)


def _vd_chunk(q, k, v, lf, st, *, rev, hb):
    c = SUB
    cc = c * c
    kd = vd = HGRN_EXPAND
    sh = _log2(c)
    n_sub = CHUNK // c
    rt = _eff(_iota((cc, c), 0) >> sh, c, rev)
    rs = _eff(_iota((cc, c), 0) & (c - 1), c, rev)
    j = _eff(_iota((cc, c), 1), c, rev)
    rep_t = (rt == j).astype(bf16)
    rep_s = rs == j
    rep_s_b, rep_s_f = rep_s.astype(bf16), rep_s.astype(f32)
    rep_t_tr = (_eff(_iota((c, cc), 1) >> sh, c, rev) == _eff(_iota((c, cc), 0), c, rev)).astype(bf16)
    between = ((j > rs) & (j <= rt)).astype(bf16)
    valid = _eff(_iota((cc, kd), 0) >> sh, c, rev) >= _eff(_iota((cc, kd), 0) & (c - 1), c, rev)
    tril = (_eff(_iota((c, c), 0), c, rev) >= _eff(_iota((c, c), 1), c, rev)).astype(bf16)
    ones_c, ones_v = jnp.ones((c, c), bf16), jnp.ones((vd, c), bf16)
    cells = [(h, i) for h in range(hb) for i in range(n_sub)]

    def cut(a, h, i):
        return a[i * c:(i + 1) * c, h * kd:(h + 1) * kd]

    qs = {hi: cut(q, *hi) for hi in cells}
    ks = {hi: cut(k, *hi) for hi in cells}
    lfs = {hi: cut(lf, *hi) for hi in cells}
    vb = {hi: cut(v, *hi).astype(bf16) for hi in cells}
    seg = {hi: _dot_sel(between, lfs[hi]) for hi in cells}
    e = {hi: jnp.where(valid, jnp.exp(seg[hi]), 0.0) for hi in cells}
    z = {hi: _dot(rep_t, qs[hi].astype(bf16)) * _dot(rep_s_b, ks[hi].astype(bf16)) * e[hi] for hi in cells}
    w = {hi: jnp.sum(z[hi], axis=1, keepdims=True) * rep_s_f for hi in cells}
    attn = {hi: _dot_sel(rep_t_tr, w[hi]) for hi in cells}
    cum = {hi: _dot_sel(tril, lfs[hi]) for hi in cells}
    tot = {hi: _dot_sel(ones_c, lfs[hi]) for hi in cells}
    tot_v = {hi: jnp.exp(_dot_sel(ones_v, lfs[hi])) for hi in cells}
    y_in = {hi: _dot(attn[hi].astype(bf16), vb[hi]) for hi in cells}
    q_e = {hi: (qs[hi] * jnp.exp(cum[hi])).astype(bf16) for hi in cells}
    k_e = {hi: (ks[hi] * jnp.exp(tot[hi] - cum[hi])).astype(bf16) for hi in cells}
    sts = [st[h] for h in range(hb)]
    ys = {}
    for i in (range(n_sub - 1, -1, -1) if rev else range(n_sub)):
        for h in range(hb):
            ys[(h, i)] = y_in[(h, i)] + _dot_nt(q_e[(h, i)], sts[h].astype(bf16))
            sts[h] = sts[h] * tot_v[(h, i)] + _dot_tn(vb[(h, i)], k_e[(h, i)])
    y = jnp.concatenate([jnp.concatenate([ys[(h, i)] for i in range(n_sub)], axis=0) for h in range(hb)], axis=1)
    return y, jnp.concatenate([a[None] for a in sts], axis=0)


def _dl_chunk(q, k, v, bq, laq, bk, lak, s, *, rev, hb, rr):
    qc = CHUNK
    kd = GDN_HEAD
    row, col = _eff(_iota((qc, qc), 0), qc, rev), _eff(_iota((qc, qc), 1), qc, rev)
    tril = (row >= col).astype(bf16)
    triu = (row <= col).astype(f32)
    ones = jnp.ones((qc, qc), bf16)
    ones_k = jnp.ones((kd, qc), bf16)
    eye = (row == col).astype(f32)
    heads = range(hb)
    lanes = [slice(j * kd, (j + 1) * kd) for j in heads]
    kf = [k[:, lanes[j // rr]] for j in heads]
    qb = [q[:, lanes[j // rr]].astype(bf16) for j in heads]
    kb = [a.astype(bf16) for a in kf]
    kk = [_dot_nt(kb[j], kb[j]) for j in heads]
    qk = [_dot_nt(qb[j], kb[j]) for j in heads]
    eseg = [jnp.exp(jnp.minimum(_dot_sel(tril, laq[j]) - _dot_sel(ones, laq[j] * triu), 0.0)) for j in heads]
    cum_k = [_dot_sel(tril, lak[:, lanes[j]]) for j in heads]
    tot = [_dot_sel(ones, lak[:, lanes[j]]) for j in heads]
    tot_k = [_dot_sel(ones_k, lak[:, lanes[j]]) for j in heads]
    ecum = [jnp.exp(a) for a in cum_k]
    mp = [-(bq[j] * kk[j] * jnp.where(row > col, eseg[j], 0.0)) for j in heads]
    tinv = [eye + a for a in mp]
    for _ in range(_log2(qc) - 1):
        mp = [_dot3(a, a) for a in mp]
        tinv = [t + _dot3(t, a) for t, a in zip(tinv, mp)]
    u = [_dot3(tinv[j], v[:, lanes[j]] * bk[:, lanes[j]]) for j in heads]
    w = [_dot3(tinv[j], kf[j] * bk[:, lanes[j]] * ecum[j]) for j in heads]
    sb = [s[j].astype(bf16) for j in heads]
    v_new = [u[j] - _dot(w[j].astype(bf16), sb[j]) for j in heads]
    ys = [_dot((qk[j] * jnp.where(row >= col, eseg[j], 0.0)).astype(bf16), v_new[j].astype(bf16)) + ecum[j] * _dot(qb[j], sb[j])
          for j in heads]
    ss = [jnp.exp(tot_k[j]) * s[j] + _dot_tn(kb[j], (v_new[j] * jnp.exp(tot[j] - cum_k[j])).astype(bf16)) for j in heads]
    return jnp.concatenate(ys, axis=1), jnp.concatenate([a[None] for a in ss], axis=0)


def _scan_call(chunk, ins, maps, blocks, state_shape, y_shape, y_block, y_map, units, nc, ncx, rev, name,
               dy=None, sprev=None):
    n = len(ins)
    params = pltpu.CompilerParams(dimension_semantics=("parallel", "arbitrary"), vmem_limit_bytes=VMEM_LIMIT)
    state_block = (None, None) + state_shape
    zeros = (0,) * len(state_shape)

    def chunk_at(pos):
        return jnp.where(pos < ncx, ncx - 1 - pos, nc + ncx - 1 - pos) if rev else pos

    def at(m, pos_of):
        return lambda u, c: m(u, chunk_at(pos_of(c)))

    if dy is None:
        def fwd_body(*refs):
            in_refs, y_ref, sp_ref, s_scr = refs[:n], refs[n], refs[n + 1], refs[n + 2]

            @pl.when(pl.program_id(1) == 0)
            def _():
                s_scr[...] = jnp.zeros_like(s_scr)

            s = s_scr[...]
            sp_ref[...] = s
            y, s_new = chunk(*[r[...] for r in in_refs], s)
            y_ref[...] = y
            s_scr[...] = s_new

        same = lambda c: c
        return pl.pallas_call(
            fwd_body, grid=(units, nc),
            in_specs=[pl.BlockSpec(b, at(m, same)) for b, m in zip(blocks, maps)],
            out_specs=[pl.BlockSpec(y_block, at(y_map, same)), pl.BlockSpec(state_block, lambda u, c: (u, c) + zeros)],
            out_shape=[jax.ShapeDtypeStruct(y_shape, f32), jax.ShapeDtypeStruct((units, nc) + state_shape, f32)],
            scratch_shapes=[pltpu.VMEM(state_shape, f32)],
            compiler_params=params, name=name,
        )(*ins)

    def bwd_body(*refs):
        in_refs, sp_ref, dy_ref = refs[:n], refs[n], refs[n + 1]
        out_refs, ds_scr = refs[n + 2:2 * n + 2], refs[2 * n + 2]

        @pl.when(pl.program_id(1) == 0)
        def _():
            ds_scr[...] = jnp.zeros_like(ds_scr)

        _, vjp = jax.vjp(chunk, *[r[...] for r in in_refs], sp_ref[...])
        grads = vjp((dy_ref[...], ds_scr[...]))
        for r, g in zip(out_refs, grads[:n]):
            r[...] = g
        ds_scr[...] = grads[n]

    back = lambda c: nc - 1 - c
    return pl.pallas_call(
        bwd_body, grid=(units, nc),
        in_specs=[pl.BlockSpec(b, at(m, back)) for b, m in zip(blocks, maps)]
        + [pl.BlockSpec(state_block, lambda u, c: (u, nc - 1 - c) + zeros), pl.BlockSpec(y_block, at(y_map, back))],
        out_specs=[pl.BlockSpec(b, at(m, back)) for b, m in zip(blocks, maps)],
        out_shape=[jax.ShapeDtypeStruct(a.shape, f32) for a in ins],
        scratch_shapes=[pltpu.VMEM(state_shape, f32)],
        compiler_params=params, name=name,
    )(*ins, sprev, dy)


def _scan_name(kind, rev, t, dy):
    return f"{kind}_{'bwd' if dy is not None else 'fwd'}_{'rev' if rev else 'fore'}_{t}"


def _sd_call(g, r, p, rev, ncx, ins, dy=None, sprev=None):
    q, _, v, _, _ = ins
    t = q.shape[0]
    n = q.shape[1] // g
    ub = SD_UNITS_PER_STEP
    im = lambda u, c: (c, u)
    imx = lambda u, c: (u, c, 0)
    bq, bv, bx = (CHUNK, ub * n), (CHUNK, ub * r * p), (ub, CHUNK, r * CHUNK)
    return _scan_call(functools.partial(_sd_chunk, r=r, p=p, rev=rev, ub=ub), list(ins), [im, im, im, imx, im],
                      [bq, bq, bv, bx, bv], (ub, n, r * p), v.shape, bv, im, g // ub, t // CHUNK, ncx, rev,
                      _scan_name(f"sd{n}x{r}x{p}", rev, t, dy), dy, sprev)


def _vd_call(rev, ncx, ins, dy=None, sprev=None):
    q = ins[0]
    t, h = q.shape[0], q.shape[1] // HGRN_EXPAND
    hb = HGRN_HEADS_PER_STEP
    im = lambda u, c: (c, u)
    blk = (CHUNK, hb * HGRN_EXPAND)
    return _scan_call(functools.partial(_vd_chunk, rev=rev, hb=hb), list(ins), [im] * 4, [blk] * 4,
                      (hb, HGRN_EXPAND, HGRN_EXPAND), q.shape, blk, im, h // hb, t // CHUNK, ncx, rev,
                      _scan_name("vd", rev, t, dy), dy, sprev)


def _dl_call(rev, ncx, ins, dy=None, sprev=None):
    q, _, v, _, _, _, _ = ins
    t = q.shape[0]
    hv = v.shape[1] // GDN_HEAD
    rr = hv // (q.shape[1] // GDN_HEAD)
    hb = GDN_HEADS_PER_STEP
    im = lambda u, c: (c, u)
    imx = lambda u, c: (u, c, 0)
    bqk, bv, bx = (CHUNK, hb // rr * GDN_HEAD), (CHUNK, hb * GDN_HEAD), (hb, CHUNK, CHUNK)
    return _scan_call(functools.partial(_dl_chunk, rev=rev, hb=hb, rr=rr), list(ins), [im, im, im, imx, imx, im, im],
                      [bqk, bqk, bv, bx, bx, bv, bv], (hb, GDN_HEAD, GDN_HEAD), v.shape, bv, im, hv // hb, t // CHUNK, ncx, rev,
                      _scan_name("dl", rev, t, dy), dy, sprev)


_SCAN_CALLS = {"ssd": functools.partial(_sd_call, SSD_GROUPS, SSD_HEADS // SSD_GROUPS, SSD_HEAD_DIM),
               "ret": functools.partial(_sd_call, RET_HEADS, 1, RET_V), "hgrn": _vd_call, "gdn": _dl_call}
_SCANS = {}


def _scan(kind, rev, lc):
    key = (kind, rev, lc)
    if key not in _SCANS:
        call = functools.partial(_SCAN_CALLS[kind], rev, lc // CHUNK)

        @jax.custom_vjp
        def scan(*ins):
            return call(ins)[0]

        def fwd(*ins):
            y, sprev = call(ins)
            return y, (ins, sprev)

        def bwd(res, dy):
            ins, sprev = res
            return tuple(call(ins, dy, sprev))

        scan.defvjp(fwd, bwd)
        _SCANS[key] = scan
    return _SCANS[key]


def _place():
    x, y, c = lax.axis_index("x"), lax.axis_index("y"), lax.axis_index("c")
    chips = [(1 - x, y), (x, 1 - y), (1 - x, 1 - y)]
    return x, y, c, chips


def _small_all_gather(block):
    m_per, n = block.shape

    def body(x_ref, out_ref, send_sems, recv_sems, local_sem):
        x, y, c, chips = _place()
        me, sibling = (x, y, c), (x, y, 1 - c)

        def rows(px, py, pc):
            return out_ref.at[pl.ds((4 * px + 2 * py + pc) * m_per, m_per), :]

        def copy(k, blk, to, src=None):
            return pltpu.make_async_remote_copy(
                src_ref=rows(*blk) if src is None else src, dst_ref=rows(*blk),
                send_sem=send_sems.at[k], recv_sem=recv_sems.at[k], device_id=to, device_id_type=MESH)

        mine = pltpu.make_async_copy(x_ref, rows(*me), local_sem)
        mine.start()
        first = [copy(0, me, sibling, src=x_ref)]
        first += [copy(1 + j, me, (*chip, c), src=x_ref) for j, chip in enumerate(chips)]
        for cp in first:
            cp.start()
        passed = [copy(4 + j, (*chip, c), sibling) for j, chip in enumerate(chips)]
        for j, chip in enumerate(chips):
            copy(1 + j, (*chip, c), me).wait_recv()
            passed[j].start()
        copy(0, sibling, me).wait_recv()
        for j, chip in enumerate(chips):
            copy(4 + j, (*chip, 1 - c), me).wait_recv()
        for cp in first + passed:
            cp.wait_send()
        mine.wait()

    return pl.pallas_call(
        body,
        out_shape=jax.ShapeDtypeStruct((N_DEV * m_per, n), block.dtype),
        in_specs=[pl.BlockSpec(memory_space=pltpu.VMEM)],
        out_specs=pl.BlockSpec(memory_space=pltpu.VMEM),
        scratch_shapes=[pltpu.SemaphoreType.DMA((7,)), pltpu.SemaphoreType.DMA((7,)), pltpu.SemaphoreType.DMA],
        compiler_params=pltpu.CompilerParams(vmem_limit_bytes=VMEM_LIMIT),
        name=f"small_all_gather_{m_per}x{n}",
    )(block)


def _any_spec():
    return pl.BlockSpec(memory_space=pl.ANY)


def _weight_all_gather(pack):
    rows, width = pack.shape
    half = rows // 2

    def body(w_ref, out_ref, send_sems, recv_sems):
        x, y, c, chips = _place()
        sibling = (x, y, 1 - c)

        def part(px, py, pc):
            return out_ref.at[2 * px + py, pl.ds(pl.multiple_of(pc * half, 16), half), :]

        def copy(k, blk, to, src=None):
            return pltpu.make_async_remote_copy(
                src_ref=part(*blk) if src is None else src, dst_ref=part(*blk),
                send_sem=send_sems.at[k], recv_sem=recv_sems.at[k], device_id=to, device_id_type=MESH)

        my_half = w_ref.at[pl.ds(pl.multiple_of(c * half, 16), half), :]
        first = [copy(j, (x, y, c), (*chip, c), src=my_half) for j, chip in enumerate(chips)]
        for cp in first:
            cp.start()
        passed = [copy(3 + j, (*chip, c), sibling) for j, chip in enumerate(chips)]
        for j, chip in enumerate(chips):
            copy(j, (*chip, c), (x, y, c)).wait_recv()
            passed[j].start()
        for j, chip in enumerate(chips):
            copy(3 + j, (*chip, 1 - c), (x, y, c)).wait_recv()
        for cp in first + passed:
            cp.wait_send()

    landed = pl.pallas_call(
        body,
        out_shape=jax.ShapeDtypeStruct((N_CHIPS, rows, width), pack.dtype),
        in_specs=[_any_spec()], out_specs=_any_spec(),
        scratch_shapes=[pltpu.SemaphoreType.DMA((6,)), pltpu.SemaphoreType.DMA((6,))],
        name="weight_all_gather",
    )(pack)
    return lax.dynamic_update_index_in_dim(landed, pack, 2 * lax.axis_index("x") + lax.axis_index("y"), 0)


def _sibling_swap(send):
    def body(s_ref, out_ref, send_sem, recv_sem):
        x, y, c, _ = _place()
        cp = pltpu.make_async_remote_copy(src_ref=s_ref, dst_ref=out_ref, send_sem=send_sem, recv_sem=recv_sem,
                                          device_id=(x, y, 1 - c), device_id_type=MESH)
        cp.start()
        cp.wait()

    return pl.pallas_call(
        body, out_shape=jax.ShapeDtypeStruct(send.shape, send.dtype),
        in_specs=[_any_spec()], out_specs=_any_spec(),
        scratch_shapes=[pltpu.SemaphoreType.DMA, pltpu.SemaphoreType.DMA],
        name=f"sibling_swap_{'x'.join(map(str, send.shape))}_{jnp.dtype(send.dtype).name}",
    )(send)


def _sibling_share(mine):
    def body(m_ref, out_ref, send_sem, recv_sem):
        x, y, c, _ = _place()
        sibling = (x, y, 1 - c)
        give = pltpu.make_async_remote_copy(src_ref=m_ref, dst_ref=out_ref.at[c], send_sem=send_sem, recv_sem=recv_sem,
                                            device_id=sibling, device_id_type=MESH)
        give.start()
        take = pltpu.make_async_remote_copy(src_ref=m_ref, dst_ref=out_ref.at[1 - c], send_sem=send_sem, recv_sem=recv_sem,
                                            device_id=sibling, device_id_type=MESH)
        take.wait_recv()
        give.wait_send()

    landed = pl.pallas_call(
        body, out_shape=jax.ShapeDtypeStruct((2,) + mine.shape, mine.dtype),
        in_specs=[_any_spec()], out_specs=_any_spec(),
        scratch_shapes=[pltpu.SemaphoreType.DMA, pltpu.SemaphoreType.DMA],
        name="sibling_share",
    )(mine)
    return lax.dynamic_update_index_in_dim(landed, mine, lax.axis_index("c"), 0)


def _chip_exchange(parts):
    _, rows, width = parts.shape

    def body(p_ref, out_ref, send_sems, recv_sems):
        x, y, c, chips = _place()
        copies = [pltpu.make_async_remote_copy(
            src_ref=p_ref.at[2 * px + py], dst_ref=out_ref.at[j], send_sem=send_sems.at[j], recv_sem=recv_sems.at[j],
            device_id=(px, py, c), device_id_type=MESH) for j, (px, py) in enumerate(chips)]
        for cp in copies:
            cp.start()
        for cp in copies:
            cp.wait()

    return pl.pallas_call(
        body, out_shape=jax.ShapeDtypeStruct((3, rows, width), parts.dtype),
        in_specs=[_any_spec()], out_specs=_any_spec(),
        scratch_shapes=[pltpu.SemaphoreType.DMA((3,)), pltpu.SemaphoreType.DMA((3,))],
        name="chip_exchange",
    )(parts)


def _row_tile(rows, width, n_arrays):
    budget = VMEM_LIMIT // (4 * 2 * n_arrays * width * 2)
    return _pick(rows, tuple(t for t in (1024, 512, 256, 128, 64, 32, 16, 8) if t <= max(budget, 8)))


def _add_n(arrays, out_dtype):
    rows, width = arrays[0].shape
    n = len(arrays)
    tr = _row_tile(rows, width, n + 1)

    def body(*refs):
        acc = refs[0][...].astype(f32)
        for r in refs[1:n]:
            acc = acc + r[...].astype(f32)
        refs[n][...] = acc.astype(out_dtype)

    spec = pl.BlockSpec((tr, width), lambda i: (i, 0))
    return pl.pallas_call(
        body, grid=(rows // tr,), in_specs=[spec] * n, out_specs=spec,
        out_shape=jax.ShapeDtypeStruct((rows, width), out_dtype),
        compiler_params=pltpu.CompilerParams(dimension_semantics=("parallel",), vmem_limit_bytes=VMEM_LIMIT),
        name=f"add{n}_{rows}x{width}_{jnp.dtype(out_dtype).name}",
    )(*arrays)


def _adamw(w, g, m, v):
    rows, width = w.shape
    tr = _row_tile(rows, width, 7)
    c1 = 1.0 / (1.0 - ADAM_B1 ** ADAM_STEP)
    c2 = 1.0 / (1.0 - ADAM_B2 ** ADAM_STEP)

    def body(w_ref, g_ref, m_ref, v_ref, d_ref, mo_ref, vo_ref):
        gg = g_ref[...]
        m_new = ADAM_B1 * m_ref[...] + (1.0 - ADAM_B1) * gg
        v_new = ADAM_B2 * v_ref[...] + (1.0 - ADAM_B2) * (gg * gg)
        d_ref[...] = -ADAM_LR * ((m_new * c1) / (jnp.sqrt(v_new * c2) + ADAM_EPS) + ADAM_WD * w_ref[...])
        mo_ref[...] = m_new
        vo_ref[...] = v_new

    spec = pl.BlockSpec((tr, width), lambda i: (i, 0))
    sds = jax.ShapeDtypeStruct((rows, width), f32)
    return pl.pallas_call(
        body, grid=(rows // tr,), in_specs=[spec] * 4, out_specs=[spec] * 3, out_shape=[sds] * 3,
        compiler_params=pltpu.CompilerParams(dimension_semantics=("parallel",), vmem_limit_bytes=VMEM_LIMIT),
        name=f"adamw_{rows}x{width}",
    )(w, g, m, v)


def _rmsnorm(x, g):
    return x * lax.rsqrt(jnp.mean(x * x, axis=-1, keepdims=True) + NORM_EPS) * g


def _conv(u, w, lc):
    t = u.shape[0]
    pos = jnp.arange(t)[:, None]
    zero = jnp.zeros((1, u.shape[1]), u.dtype)
    prev = jnp.where((pos == 0) | (pos == lc), 0.0, jnp.concatenate([zero, u[:-1]], axis=0))
    nxt = jnp.where((pos == lc - 1) | (pos == t - 1), 0.0, jnp.concatenate([u[1:], zero], axis=0))
    return w[0] * prev + w[1] * u + w[2] * nxt


def _lanes_q(a):
    return jnp.broadcast_to(a.T[:, :, None], (a.shape[1], a.shape[0], CHUNK))


def _ssd_mixer(h, lc, start, w_main, w_dt, conv_w, conv_b, dt_bias, a_log, d_skip, norm_g, w_out):
    t = h.shape[0]
    r = SSD_HEADS // SSD_GROUPS
    u = pmm(h, w_main)
    dt = pmm(h, w_dt)
    z, xbc = u[:, :SSD_D_INNER], u[:, SSD_D_INNER:]
    xbc = jax.nn.silu(_conv(xbc, conv_w, lc) + conv_b)
    xs = xbc[:, :SSD_D_INNER]
    bm = xbc[:, SSD_D_INNER:SSD_D_INNER + SSD_GROUPS * SSD_STATE]
    cm = xbc[:, SSD_D_INNER + SSD_GROUPS * SSD_STATE:]
    dt = jax.nn.softplus(dt.reshape(t, 2, SSD_HEADS) + dt_bias)
    log_a = -jnp.exp(a_log) * dt
    per_lane = lambda a: jnp.repeat(a, SSD_HEAD_DIM, axis=1)
    y = jnp.repeat(d_skip, SSD_HEAD_DIM) * xs
    for d in (0, 1):
        lap = per_lane(log_a[:, d])
        laq = lap.reshape(t, SSD_GROUPS, r * CHUNK).transpose(1, 0, 2)
        y = y + _scan("ssd", d == 1, lc)(cm, bm, xs * per_lane(dt[:, d]), laq, lap)
    y = y[start:] * jax.nn.silu(z[start:])
    n = t - start
    y = _rmsnorm(y.reshape(n, SSD_GROUPS, -1), norm_g.reshape(SSD_GROUPS, -1)).reshape(n, SSD_D_INNER)
    return pmm(y, w_out)


def _rope(a, rows):
    pos = jnp.arange(rows * GRID_W)
    row = (pos // GRID_W).astype(f32)
    col = (pos % GRID_W).astype(f32)
    half = a.shape[-1] // 2
    inv_freq = ROPE_BASE ** (-jnp.arange(0, half, 2, dtype=f32) / half)

    def rot(u, p):
        ang = p[:, None] * inv_freq
        cos, sin = jnp.cos(ang)[:, None, :], jnp.sin(ang)[:, None, :]
        u1, u2 = jnp.split(u, 2, axis=-1)
        return jnp.concatenate([u1 * cos - u2 * sin, u2 * cos + u1 * sin], axis=-1)

    return jnp.concatenate([rot(a[..., :half], row), rot(a[..., half:], col)], axis=-1)


def _ret_mixer(h, lc, start, w_in, log_decay, w_out):
    t = h.shape[0]
    u = pmm(h, w_in)
    q = u[:, :D_MODEL].reshape(t, RET_HEADS, RET_QK)
    k = u[:, D_MODEL:2 * D_MODEL].reshape(t, RET_HEADS, RET_QK) * RET_QK ** -0.5
    v = u[:, 2 * D_MODEL:2 * D_MODEL + RET_DV]
    gate = u[:, 2 * D_MODEL + RET_DV:]
    rows = (t - lc) // GRID_W
    q = jnp.concatenate([q[:lc], _rope(q[lc:], rows)], axis=0).reshape(t, D_MODEL)
    k = jnp.concatenate([k[:lc], _rope(k[lc:], rows)], axis=0).reshape(t, D_MODEL)
    y = 0.0
    for d in (0, 1):
        laq = jnp.broadcast_to(log_decay[d][:, None, None], (RET_HEADS, t, CHUNK))
        lap = jnp.broadcast_to(jnp.repeat(log_decay[d], RET_V)[None, :], (t, RET_DV))
        y = y + _scan("ret", d == 1, lc)(q, k, v, laq, lap)
    n = t - start
    y = y[start:].reshape(n, RET_HEADS, RET_V)
    mu = jnp.mean(y, axis=-1, keepdims=True)
    var = jnp.mean(jnp.square(y - mu), axis=-1, keepdims=True)
    y = ((y - mu) * lax.rsqrt(var + NORM_EPS)).reshape(n, RET_DV) * jax.nn.silu(gate[start:])
    return pmm(y, w_out)


def _hgrn_mixer(h, lc, start, w_in, lb, norm_g, w_out):
    t = h.shape[0]
    u = pmm(h, w_in)
    q, f_f, f_b, inp, gate = (u[:, i * D_MODEL:(i + 1) * D_MODEL] for i in range(5))

    def gates(f):
        log_f = jnp.logaddexp(jnp.log(lb), jnp.log1p(-lb) + jax.nn.log_sigmoid(f))
        return log_f, (1 - lb) * jax.nn.sigmoid(-f)

    lf_f, k_f = gates(f_f)
    lf_b, k_b = gates(f_b)
    y = _scan("hgrn", False, lc)(q, k_f, inp, lf_f) + _scan("hgrn", True, lc)(q, k_b, inp, lf_b)
    n = t - start
    y = _rmsnorm(y[start:].reshape(n, HGRN_HEADS, HGRN_EXPAND), norm_g.reshape(HGRN_HEADS, HGRN_EXPAND))
    y = y.reshape(n, D_MODEL) * jax.nn.silu(gate[start:])
    return pmm(y, w_out)


def _l2norm(a):
    return a * lax.rsqrt(jnp.sum(a * a, axis=-1, keepdims=True) + 1e-6)


def _gdn_mixer(h, lc, start, w_main, w_gate, conv_w, dt_bias, a_log, norm_g, w_out):
    t = h.shape[0]
    u = pmm(h, w_main)
    ba = pmm(h, w_gate)
    qkv = jax.nn.silu(_conv(u[:, :GDN_CONV_CH], conv_w, lc))
    z = u[:, GDN_CONV_CH:]
    q = (_l2norm(qkv[:, :GDN_DK].reshape(t, GDN_K_HEADS, GDN_HEAD)) * GDN_HEAD ** -0.5).reshape(t, GDN_DK)
    k = _l2norm(qkv[:, GDN_DK:2 * GDN_DK].reshape(t, GDN_K_HEADS, GDN_HEAD)).reshape(t, GDN_DK)
    v = qkv[:, 2 * GDN_DK:]
    beta = jax.nn.sigmoid(ba[:, :2 * GDN_V_HEADS].reshape(t, 2, GDN_V_HEADS))
    log_a = -jnp.exp(a_log) * jax.nn.softplus(ba[:, 2 * GDN_V_HEADS:].reshape(t, 2, GDN_V_HEADS) + dt_bias)
    per_lane = lambda a: jnp.repeat(a, GDN_HEAD, axis=1)
    y = 0.0
    for d in (0, 1):
        y = y + _scan("gdn", d == 1, lc)(q, k, v, _lanes_q(beta[:, d]), _lanes_q(log_a[:, d]),
                                         per_lane(beta[:, d]), per_lane(log_a[:, d]))
    n = t - start
    y = _rmsnorm(y[start:].reshape(n, GDN_V_HEADS, GDN_HEAD), norm_g) * jax.nn.silu(
        z[start:].reshape(n, GDN_V_HEADS, GDN_HEAD))
    return pmm(y.reshape(n, GDN_DV), w_out)


def _local_loss(x, mod, big, small, ctx, target):
    lc = ctx.shape[0]
    tok = jnp.concatenate([ctx, x], axis=0)
    t = tok.shape[0]
    is_ctx = (jnp.arange(t) < lc)[:, None]
    p = jax.nn.softmax(small["hgrn_lb_logits"], axis=0)
    for i in range(DEPTH):
        last = i == DEPTH - 1
        start = lc if last else 0
        sh1, sc1, g1, sh2, sc2, g2 = (jnp.where(is_ctx, m[1], m[0]) for m in jnp.split(mod[i], 6, axis=-1))
        h = _rmsnorm(tok, small["norm_g"][i, 0]) * (1 + sc1) + sh1
        if i == 0:
            y = _ssd_mixer(h, lc, start, big["ssd_w_main"], big["ssd_w_dt"], small["ssd_conv_w"][0], small["ssd_conv_b"][0],
                           small["ssd_dt_bias"][0], small["ssd_a_log"][0], small["ssd_d"][0], small["ssd_norm_g"][0],
                           big["ssd_w_out"])
        elif i == 1:
            y = _ret_mixer(h, lc, start, big["ret_w_in"], small["ret_log_decay"][0], big["ret_w_out"])
        elif i == 2:
            lb = jnp.cumsum(p, axis=0)[i] - p[0]
            y = _hgrn_mixer(h, lc, start, big["hgrn_w_in"], lb, small["hgrn_norm_g"][0], big["hgrn_w_out"])
        else:
            y = _gdn_mixer(h, lc, start, big["gdn_w_main"], big["gdn_w_gate"], small["gdn_conv_w"][0],
                           small["gdn_dt_bias"][0], small["gdn_a_log"][0], small["gdn_norm_g"][0], big["gdn_w_out"])
        if last:
            tok, sc2, sh2, g1, g2 = tok[lc:], sc2[lc:], sh2[lc:], g1[lc:], g2[lc:]
        tok = tok + g1 * y
        h = _rmsnorm(tok, small["norm_g"][i, 1]) * (1 + sc2) + sh2
        hid = jnp.square(jax.nn.relu(pmm(h, big["mlp_w1"][i])))
        tok = tok + g2 * pmm(hid, big["mlp_w2"][i])
    out = _rmsnorm(tok, small["final_g"])
    return 0.5 * jnp.sum(jnp.mean(jnp.square(out - target), axis=-1))


def _rows_of(a, width):
    return a.reshape(-1, width)


def _pad_rows(flat, width, mult=8):
    rows = -(-flat.shape[0] // width)
    rows = -(-rows // mult) * mult
    return jnp.pad(flat, (0, rows * width - flat.shape[0])).reshape(rows, width)


def _unpack_weights(gathered, shard_shapes):
    full, r0 = {}, 0
    for name in BIG:
        shp = shard_shapes[name]
        rows = math.prod(shp) // PACK_W
        blk = gathered[:, r0:r0 + rows].reshape((N_CHIPS,) + shp)
        r0 += rows
        if name in COL_SHARDED:
            full[name] = jnp.moveaxis(blk, 0, 2).reshape(shp[0], shp[1], N_CHIPS * shp[2])
        else:
            full[name] = jnp.moveaxis(blk, 0, 1).reshape(shp[0], N_CHIPS * shp[1], shp[2])
    return full


def _pack_grads(grads, shard_shapes):
    parts = []
    for name in BIG:
        shp = shard_shapes[name]
        g = grads[name]
        if name in COL_SHARDED:
            blk = jnp.moveaxis(g.reshape(shp[0], shp[1], N_CHIPS, shp[2]), 2, 0)
        else:
            blk = jnp.moveaxis(g.reshape(shp[0], N_CHIPS, shp[1], shp[2]), 1, 0)
        parts.append(blk.reshape(N_CHIPS, -1, PACK_W))
    return jnp.concatenate(parts, axis=1)


def _silu_grad(a):
    s = jax.nn.sigmoid(a)
    return s * (1 + a * (1 - s))


def kernel(x, c, ctx, c_ctx, ada_w, ada_b, norm_g, mlp_w1, mlp_w2, final_g, ssd_w_in, ssd_conv_w, ssd_conv_b, ssd_dt_bias, ssd_a_log, ssd_d, ssd_norm_g, ssd_w_out, ret_w_in, ret_log_decay, ret_w_out, hgrn_w_in, hgrn_lb_logits, hgrn_norm_g, hgrn_w_out, gdn_w_in, gdn_conv_w, gdn_dt_bias, gdn_a_log, gdn_norm_g, gdn_w_out, loss_target, m_c_ctx, m_ada_w, m_ada_b, m_norm_g, m_mlp_w1, m_mlp_w2, m_final_g, m_ssd_w_in, m_ssd_conv_w, m_ssd_conv_b, m_ssd_dt_bias, m_ssd_a_log, m_ssd_d, m_ssd_norm_g, m_ssd_w_out, m_ret_w_in, m_ret_log_decay, m_ret_w_out, m_hgrn_w_in, m_hgrn_lb_logits, m_hgrn_norm_g, m_hgrn_w_out, m_gdn_w_in, m_gdn_conv_w, m_gdn_dt_bias, m_gdn_a_log, m_gdn_norm_g, m_gdn_w_out, v_c_ctx, v_ada_w, v_ada_b, v_norm_g, v_mlp_w1, v_mlp_w2, v_final_g, v_ssd_w_in, v_ssd_conv_w, v_ssd_conv_b, v_ssd_dt_bias, v_ssd_a_log, v_ssd_d, v_ssd_norm_g, v_ssd_w_out, v_ret_w_in, v_ret_log_decay, v_ret_w_out, v_hgrn_w_in, v_hgrn_lb_logits, v_hgrn_norm_g, v_hgrn_w_out, v_gdn_w_in, v_gdn_conv_w, v_gdn_dt_bias, v_gdn_a_log, v_gdn_norm_g, v_gdn_w_out):
    env = dict(locals())
    w_loc = {n: env[n] for n in WEIGHTS}
    m_loc = {n: env["m_" + n] for n in WEIGHTS}
    v_loc = {n: env["v_" + n] for n in WEIGHTS}
    chip = 2 * lax.axis_index("x") + lax.axis_index("y")
    core = lax.axis_index("c")
    dev = 2 * chip + core
    d = D_MODEL

    sharded_small = [n for n, ax in SMALL if ax is not None]
    flat1 = jnp.concatenate([c.reshape(-1)] + [w_loc[n].reshape(-1) for n in sharded_small])
    g1 = _small_all_gather(_pad_rows(flat1, SMALL_W)).reshape(N_DEV, -1)
    c_all = g1[:, :d]
    small, off = {n: w_loc[n] for n, ax in SMALL if ax is None}, d
    for n in sharded_small:
        ax = dict(SMALL)[n]
        size = w_loc[n].size
        pieces = [g1[2 * k, off:off + size].reshape(w_loc[n].shape) for k in range(N_CHIPS)]
        small[n] = jnp.concatenate(pieces, axis=ax)
        off += size

    n_sh = ada_w.shape[2]
    cond_in = jnp.concatenate([c_all, c_ctx[None]], axis=0)
    cond = jnp.pad(jax.nn.silu(cond_in), ((0, 16 - N_DEV - 1), (0, 0)))
    ada_b_sh = lax.dynamic_slice_in_dim(ada_b, chip * n_sh, n_sh, axis=1)
    mod_sh = jnp.stack([_mm(cond, ada_w[i], "nn") + ada_b_sh[i] for i in range(DEPTH)])
    g2 = _small_all_gather(mod_sh.reshape(-1, SMALL_W)).reshape(N_DEV, DEPTH, 16, n_sh)
    mod_all = jnp.concatenate([g2[2 * k] for k in range(N_CHIPS)], axis=-1)
    mod_loc = jnp.stack([lax.dynamic_index_in_dim(mod_all, dev, axis=1, keepdims=False), mod_all[:, N_DEV]], axis=1)

    shard_shapes = {n: w_loc[n].shape for n in BIG}
    pack = jnp.concatenate([_rows_of(w_loc[n].astype(bf16), PACK_W) for n in BIG], axis=0)
    full = _unpack_weights(_weight_all_gather(pack), shard_shapes)
    big = {"mlp_w1": tuple(full["mlp_w1"][i] for i in range(DEPTH)),
           "mlp_w2": tuple(full["mlp_w2"][i] for i in range(DEPTH)),
           "ssd_w_main": full["ssd_w_in"][0, :, :SSD_MAIN], "ssd_w_dt": full["ssd_w_in"][0, :, SSD_MAIN:],
           "ssd_w_out": full["ssd_w_out"][0], "ret_w_in": full["ret_w_in"][0], "ret_w_out": full["ret_w_out"][0],
           "hgrn_w_in": full["hgrn_w_in"][0], "hgrn_w_out": full["hgrn_w_out"][0],
           "gdn_w_main": full["gdn_w_in"][0, :, :GDN_MAIN], "gdn_w_gate": full["gdn_w_in"][0, :, GDN_MAIN:],
           "gdn_w_out": full["gdn_w_out"][0]}

    small_diff = {n: small[n] for n, _ in SMALL if n not in ("c_ctx", "ada_b")}
    loss_loc, (gx, gmod, gbig, gsmall) = jax.value_and_grad(_local_loss, argnums=(0, 1, 2, 3))(
        x[0], mod_loc, big, small_diff, ctx[0], loss_target[0])
    loss = lax.psum(loss_loc, ("x", "y", "c"))

    small_names = [n for n, _ in SMALL if n not in ("c_ctx", "ada_b")]
    flat3 = jnp.concatenate([gmod.reshape(-1)] + [gsmall[n].reshape(-1) for n in small_names])
    g3 = _small_all_gather(_pad_rows(flat3, SMALL_W)).reshape(N_DEV, -1)
    gmod_all = g3[:, :gmod.size].reshape(N_DEV, DEPTH, 2, 6 * d)
    ctx_row = gmod_all[0, :, 1]
    for b in range(1, N_DEV):
        ctx_row = ctx_row + gmod_all[b, :, 1]
    dmod = jnp.concatenate([jnp.moveaxis(gmod_all[:, :, 0], 0, 1), ctx_row[:, None],
                            jnp.zeros((DEPTH, 16 - N_DEV - 1, 6 * d), f32)], axis=1)
    grad_small, off = {}, gmod.size
    for n in small_names:
        size = small[n].size
        tot = g3[0, off:off + size]
        for b in range(1, N_DEV):
            tot = tot + g3[b, off:off + size]
        grad_small[n] = tot.reshape(small[n].shape)
        off += size
    grad_small["ada_b"] = jnp.sum(dmod, axis=1)
    dmod_sh = lax.dynamic_slice_in_dim(dmod, chip * n_sh, n_sh, axis=2)
    grad_ada_w = jnp.stack([_mm(cond, dmod_sh[i], "tn") for i in range(DEPTH)])
    dcond = _mm(dmod_sh[0], ada_w[0], "nt")
    for i in range(1, DEPTH):
        dcond = dcond + _mm(dmod_sh[i], ada_w[i], "nt")
    g4 = _small_all_gather(_pad_rows(dcond[N_DEV], SMALL_W)).reshape(N_DEV, -1)[:, :d]
    dcond_ctx = g4[0] + g4[2] + g4[4] + g4[6]
    grad_small["c_ctx"] = dcond_ctx * _silu_grad(c_ctx)
    for n, ax in SMALL:
        if ax is not None:
            width = w_loc[n].shape[ax]
            grad_small[n] = lax.dynamic_slice_in_dim(grad_small[n], chip * width, width, axis=ax)

    gfull = {"mlp_w1": jnp.stack(gbig["mlp_w1"]), "mlp_w2": jnp.stack(gbig["mlp_w2"]),
             "ssd_w_in": jnp.concatenate([gbig["ssd_w_main"], gbig["ssd_w_dt"]], axis=1)[None],
             "ssd_w_out": gbig["ssd_w_out"][None], "ret_w_in": gbig["ret_w_in"][None], "ret_w_out": gbig["ret_w_out"][None],
             "hgrn_w_in": gbig["hgrn_w_in"][None], "hgrn_w_out": gbig["hgrn_w_out"][None],
             "gdn_w_in": jnp.concatenate([gbig["gdn_w_main"], gbig["gdn_w_gate"]], axis=1)[None],
             "gdn_w_out": gbig["gdn_w_out"][None]}
    gp = _pack_grads(gfull, shard_shapes)
    rows = gp.shape[1]
    half = rows // 2
    gp = gp.reshape(N_CHIPS, 2, half, PACK_W)
    mine = lax.dynamic_index_in_dim(gp, core, axis=1, keepdims=False)
    theirs = lax.dynamic_index_in_dim(gp, 1 - core, axis=1, keepdims=False)
    got = _sibling_swap(theirs)
    pair = _add_n([mine.reshape(-1, PACK_W), got.reshape(-1, PACK_W)], bf16).reshape(N_CHIPS, half, PACK_W)
    landed = _chip_exchange(pair)
    own = lax.dynamic_index_in_dim(pair, chip, axis=0, keepdims=False)
    red_half = _add_n([own, landed[0], landed[1], landed[2]], f32)
    red = _sibling_share(red_half).reshape(rows, PACK_W)
    grad_big, r0 = {}, 0
    for n in BIG:
        nrows = w_loc[n].size // PACK_W
        grad_big[n] = red[r0:r0 + nrows].reshape(w_loc[n].shape)
        r0 += nrows

    grads = dict(grad_big)
    grads.update(grad_small)
    grads["ada_w"] = grad_ada_w
    delta, new_m, new_v = {}, {}, {}
    for n in BIG + ("ada_w",):
        width = w_loc[n].shape[-1]
        outs = _adamw(*[a.reshape(-1, width) for a in (w_loc[n], grads[n], m_loc[n], v_loc[n])])
        delta[n], new_m[n], new_v[n] = (o.reshape(w_loc[n].shape) for o in outs)
    names = [n for n, _ in SMALL]
    packs = [_pad_rows(jnp.concatenate([t[n].reshape(-1) for n in names]), SMALL_W) for t in (w_loc, grads, m_loc, v_loc)]
    outs = _adamw(*packs)
    off = 0
    for n in names:
        size = w_loc[n].size
        delta[n], new_m[n], new_v[n] = (o.reshape(-1)[off:off + size].reshape(w_loc[n].shape) for o in outs)
        off += size

    return (loss, gx[None], *[grads[n] for n in WEIGHTS], *[delta[n] for n in WEIGHTS],
            *[new_m[n] for n in WEIGHTS], *[new_v[n] for n in WEIGHTS])
```

```python
import functools
import math

import jax
import jax.numpy as jnp
from jax import lax
from jax.experimental import pallas as pl
from jax.experimental.pallas import tpu as pltpu

f32 = jnp.float32
bf16 = jnp.bfloat16
MESH = pl.DeviceIdType.MESH

D_MODEL = 2048
DEPTH = 4
GRID_W = 64
CHUNK = 64
SUB = 16
SD_UNITS_PER_STEP = 4
HGRN_HEADS_PER_STEP = 8
GDN_HEADS_PER_STEP = 8
NORM_EPS = 1e-6
ROPE_BASE = 10000.0
N_CHIPS = 4
N_DEV = 8
PACK_W = 2048
SMALL_W = 1024
VMEM_LIMIT = 48 * 1024 * 1024

SSD_D_INNER, SSD_HEADS, SSD_GROUPS, SSD_STATE, SSD_HEAD_DIM = 4096, 64, 8, 128, 64
SSD_CONV_CH = SSD_D_INNER + 2 * SSD_GROUPS * SSD_STATE
SSD_MAIN = SSD_D_INNER + SSD_CONV_CH
RET_HEADS, RET_QK, RET_V, RET_DV = 8, 256, 512, 4096
HGRN_HEADS, HGRN_EXPAND = 16, 128
GDN_K_HEADS, GDN_V_HEADS, GDN_HEAD = 16, 32, 128
GDN_DK, GDN_DV = 2048, 4096
GDN_CONV_CH = 2 * GDN_DK + GDN_DV
GDN_MAIN = GDN_CONV_CH + GDN_DV

ADAM_LR, ADAM_B1, ADAM_B2, ADAM_EPS, ADAM_WD, ADAM_STEP = 0.001, 0.9, 0.999, 1e-08, 0.01, 10

BIG = ("mlp_w1", "mlp_w2", "ssd_w_in", "ssd_w_out", "ret_w_in", "ret_w_out", "hgrn_w_in", "hgrn_w_out",
       "gdn_w_in", "gdn_w_out")
COL_SHARDED = ("mlp_w1", "ssd_w_in", "ret_w_in", "hgrn_w_in", "gdn_w_in")
SMALL = (("c_ctx", None), ("ada_b", None), ("norm_g", 2), ("final_g", None), ("ssd_conv_w", 2), ("ssd_conv_b", None),
         ("ssd_dt_bias", None), ("ssd_a_log", None), ("ssd_d", None), ("ssd_norm_g", None), ("ret_log_decay", None),
         ("hgrn_lb_logits", None), ("hgrn_norm_g", 1), ("gdn_conv_w", 2), ("gdn_dt_bias", None), ("gdn_a_log", None),
         ("gdn_norm_g", None))
WEIGHTS = ("c_ctx", "ada_w", "ada_b", "norm_g", "mlp_w1", "mlp_w2", "final_g", "ssd_w_in", "ssd_conv_w", "ssd_conv_b",
           "ssd_dt_bias", "ssd_a_log", "ssd_d", "ssd_norm_g", "ssd_w_out", "ret_w_in", "ret_log_decay", "ret_w_out",
           "hgrn_w_in", "hgrn_lb_logits", "hgrn_norm_g", "hgrn_w_out", "gdn_w_in", "gdn_conv_w", "gdn_dt_bias",
           "gdn_a_log", "gdn_norm_g", "gdn_w_out")


def _dot(a, b, prec=None):
    return lax.dot_general(a, b, (((1,), (0,)), ((), ())), precision=prec, preferred_element_type=f32)


def _dot_nt(a, b, prec=None):
    return lax.dot_general(a, b, (((1,), (1,)), ((), ())), precision=prec, preferred_element_type=f32)


def _dot_tn(a, b, prec=None):
    return lax.dot_general(a, b, (((0,), (0,)), ((), ())), precision=prec, preferred_element_type=f32)


def _split2(x):
    hi = x.astype(bf16)
    return hi, (x - hi.astype(f32)).astype(bf16)


def _split3(x):
    hi = x.astype(bf16)
    rest = x - hi.astype(f32)
    mid = rest.astype(bf16)
    return hi, mid, (rest - mid.astype(f32)).astype(bf16)


@jax.custom_vjp
def _dot_sel(mask, x):
    hi, mid, lo = _split3(x)
    return _dot(mask, hi) + _dot(mask, mid) + _dot(mask, lo)


def _dot_sel_fwd(mask, x):
    return _dot_sel(mask, x), mask


def _dot_sel_bwd(mask, g):
    hi, mid, lo = _split3(g)
    return jnp.zeros_like(mask), _dot_tn(mask, hi) + _dot_tn(mask, mid) + _dot_tn(mask, lo)


_dot_sel.defvjp(_dot_sel_fwd, _dot_sel_bwd)


def _dot3_with(dot, a, b):
    a_hi, a_lo = _split2(a)
    b_hi, b_lo = _split2(b)
    return dot(a_hi, b_hi) + dot(a_hi, b_lo) + dot(a_lo, b_hi)


@jax.custom_vjp
def _dot3(a, b):
    return _dot3_with(_dot, a, b)


def _dot3_fwd(a, b):
    return _dot3(a, b), (a, b)


def _dot3_bwd(res, g):
    a, b = res
    return _dot3_with(_dot_nt, g, b), _dot3_with(_dot_tn, a, g)


_dot3.defvjp(_dot3_fwd, _dot3_bwd)


def _iota(shape, d):
    return lax.broadcasted_iota(jnp.int32, shape, d)


def _log2(n):
    assert n & (n - 1) == 0
    return n.bit_length() - 1


def _pick(n, cands):
    for t in cands:
        if n % t == 0:
            return t
    return n


MM_VMEM_BUDGET = 36 * 1024 * 1024
MM_STEP_BYTES = 1.2e6
MM_ACC_WEIGHT = 0.15


def _mm_tiles(m, n, k, sa, sb, so):
    best = None
    tms = [t for t in (2304, 2048, 1152, 1024, 768, 576, 512, 384, 256, 128, 64, 32, 16, 8) if m % t == 0] or [m]
    tns = [t for t in (2048, 1024, 512, 256, 128) if n % t == 0] or [n]
    tks = [t for t in (k, 4096, 2048, 1024, 768, 512, 256, 128, 64, 32, 16, 8) if k % t == 0]
    for tm in tms:
        for tn in tns:
            for tk in tks:
                nk = k // tk
                vmem = 2 * (tm * tk * sa + tk * tn * sb + tm * tn * so) + (tm * tn * 4 if nk > 1 else 0)
                if vmem > MM_VMEM_BUDGET:
                    continue
                cost = m * k * sa * (1 if nk == 1 else n // tn) + k * n * sb * (m // tm) + m * n * so
                cost += (m // tm) * (n // tn) * nk * MM_STEP_BYTES
                cost += MM_ACC_WEIGHT * m * n * 8 * nk if nk > 1 else 0
                if best is None or cost < best[0]:
                    best = (cost, tm, tn, tk)
    return best[1:]


def _mm(a, b, mode, out_dtype=f32):
    if mode == "nn":
        (m, k), n = a.shape, b.shape[1]
    elif mode == "nt":
        (m, k), n = a.shape, b.shape[0]
    else:
        (k, m), n = a.shape, b.shape[1]
    tm, tn, tk = _mm_tiles(m, n, k, a.dtype.itemsize, b.dtype.itemsize, jnp.dtype(out_dtype).itemsize)
    nk = k // tk
    dot = {"nn": _dot, "nt": _dot_nt, "tn": _dot_tn}[mode]

    def body(a_ref, b_ref, o_ref, *scratch):
        prod = dot(a_ref[...].astype(bf16), b_ref[...].astype(bf16))
        if nk == 1:
            o_ref[...] = prod.astype(out_dtype)
            return
        acc_ref, = scratch
        kk = pl.program_id(2)

        @pl.when(kk == 0)
        def _():
            acc_ref[...] = jnp.zeros_like(acc_ref)

        acc_ref[...] += prod

        @pl.when(kk == nk - 1)
        def _():
            o_ref[...] = acc_ref[...].astype(out_dtype)

    if mode == "tn":
        a_spec = pl.BlockSpec((tk, tm), lambda i, j, kk: (kk, i))
    else:
        a_spec = pl.BlockSpec((tm, tk), lambda i, j, kk: (i, kk))
    if mode == "nt":
        b_spec = pl.BlockSpec((tn, tk), lambda i, j, kk: (j, kk))
    else:
        b_spec = pl.BlockSpec((tk, tn), lambda i, j, kk: (kk, j))
    return pl.pallas_call(
        body,
        grid=(m // tm, n // tn, nk),
        in_specs=[a_spec, b_spec],
        out_specs=pl.BlockSpec((tm, tn), lambda i, j, kk: (i, j)),
        out_shape=jax.ShapeDtypeStruct((m, n), out_dtype),
        scratch_shapes=[pltpu.VMEM((tm, tn), f32)] if nk > 1 else [],
        compiler_params=pltpu.CompilerParams(dimension_semantics=("parallel", "parallel", "arbitrary"),
                                             vmem_limit_bytes=VMEM_LIMIT),
        name=f"mm_{mode}_{m}x{k}x{n}_{jnp.dtype(out_dtype).name}",
    )(a, b)


@jax.custom_vjp
def pmm(a, w):
    return _mm(a, w, "nn")


def _pmm_fwd(a, w):
    return _mm(a, w, "nn"), (a, w)


def _pmm_bwd(res, g):
    a, w = res
    return _mm(g, w, "nt"), _mm(a, g, "tn", bf16)


pmm.defvjp(_pmm_fwd, _pmm_bwd)


def _eff(i, n, rev):
    return (n - 1 - i) if rev else i


def _sd_chunk(q, k, v, laq, lap, s, *, r, p, rev, ub):
    qc = CHUNK
    n = q.shape[1] // ub
    w = r * p
    units = range(ub)
    tril = (_eff(_iota((qc, qc), 0), qc, rev) >= _eff(_iota((qc, qc), 1), qc, rev)).astype(bf16)
    ones = jnp.ones((qc, qc), bf16)
    ones_n = jnp.ones((n, qc), bf16)
    rowx = _eff(_iota((qc, r * qc), 0), qc, rev)
    colx = _eff(_iota((qc, r * qc), 1) & (qc - 1), qc, rev)
    upper = (rowx <= colx).astype(f32)
    qb = [q[:, u * n:(u + 1) * n].astype(bf16) for u in units]
    kb = [k[:, u * n:(u + 1) * n].astype(bf16) for u in units]
    vf = [v[:, u * w:(u + 1) * w] for u in units]
    lp = [lap[:, u * w:(u + 1) * w] for u in units]
    cumcol = [_dot_sel(tril, laq[u]) for u in units]
    cumrow = [_dot_sel(ones, laq[u] * upper) for u in units]
    decay = [jnp.where(rowx >= colx, jnp.exp(jnp.minimum(cumcol[u] - cumrow[u], 0.0)), 0.0) for u in units]
    ktile = [jnp.concatenate([kb[u]] * r, axis=0) if r > 1 else kb[u] for u in units]
    attn = [(_dot_nt(qb[u], ktile[u]) * decay[u]).astype(bf16) for u in units]
    vt = [a.astype(bf16) for a in vf]
    if r > 1:
        same_head = (_iota((r * qc, w), 0) >> _log2(qc)) == (_iota((r * qc, w), 1) >> _log2(p))
        vt = [jnp.where(same_head, jnp.concatenate([a] * r, axis=0), jnp.zeros((r * qc, w), bf16)) for a in vt]
    cum_p = [_dot_sel(tril, a) for a in lp]
    tot_q = [_dot_sel(ones, a) for a in lp]
    tot_n = [_dot_sel(ones_n, a) for a in lp]
    y = [_dot(attn[u], vt[u]) + jnp.exp(cum_p[u]) * _dot(qb[u], s[u].astype(bf16)) for u in units]
    s_new = [jnp.exp(tot_n[u]) * s[u] + _dot_tn(kb[u], (vf[u] * jnp.exp(tot_q[u] - cum_p[u])).astype(bf16)) for u in units]
    return jnp.concatenate(y, axis=1), jnp.concatenate([a[None] for a in s_new], axis=0)


def _vd_chunk(q, k, v, lf, st, *, rev, hb):
    c = SUB
    cc = c * c
    kd = vd = HGRN_EXPAND
    sh = _log2(c)
    n_sub = CHUNK // c
    rt = _eff(_iota((cc, c), 0) >> sh, c, rev)
    rs = _eff(_iota((cc, c), 0) & (c - 1), c, rev)
    j = _eff(_iota((cc, c), 1), c, rev)
    rep_t = (rt == j).astype(bf16)
    rep_s = rs == j
    rep_s_b, rep_s_f = rep_s.astype(bf16), rep_s.astype(f32)
    rep_t_tr = (_eff(_iota((c, cc), 1) >> sh, c, rev) == _eff(_iota((c, cc), 0), c, rev)).astype(bf16)
    between = ((j > rs) & (j <= rt)).astype(bf16)
    valid = _eff(_iota((cc, kd), 0) >> sh, c, rev) >= _eff(_iota((cc, kd), 0) & (c - 1), c, rev)
    tril = (_eff(_iota((c, c), 0), c, rev) >= _eff(_iota((c, c), 1), c, rev)).astype(bf16)
    ones_c, ones_v = jnp.ones((c, c), bf16), jnp.ones((vd, c), bf16)
    cells = [(h, i) for h in range(hb) for i in range(n_sub)]

    def cut(a, h, i):
        return a[i * c:(i + 1) * c, h * kd:(h + 1) * kd]

    qs = {hi: cut(q, *hi) for hi in cells}
    ks = {hi: cut(k, *hi) for hi in cells}
    lfs = {hi: cut(lf, *hi) for hi in cells}
    vb = {hi: cut(v, *hi).astype(bf16) for hi in cells}
    seg = {hi: _dot_sel(between, lfs[hi]) for hi in cells}
    e = {hi: jnp.where(valid, jnp.exp(seg[hi]), 0.0) for hi in cells}
    z = {hi: _dot(rep_t, qs[hi].astype(bf16)) * _dot(rep_s_b, ks[hi].astype(bf16)) * e[hi] for hi in cells}
    w = {hi: jnp.sum(z[hi], axis=1, keepdims=True) * rep_s_f for hi in cells}
    attn = {hi: _dot_sel(rep_t_tr, w[hi]) for hi in cells}
    cum = {hi: _dot_sel(tril, lfs[hi]) for hi in cells}
    tot = {hi: _dot_sel(ones_c, lfs[hi]) for hi in cells}
    tot_v = {hi: jnp.exp(_dot_sel(ones_v, lfs[hi])) for hi in cells}
    y_in = {hi: _dot(attn[hi].astype(bf16), vb[hi]) for hi in cells}
    q_e = {hi: (qs[hi] * jnp.exp(cum[hi])).astype(bf16) for hi in cells}
    k_e = {hi: (ks[hi] * jnp.exp(tot[hi] - cum[hi])).astype(bf16) for hi in cells}
    sts = [st[h] for h in range(hb)]
    ys = {}
    for i in (range(n_sub - 1, -1, -1) if rev else range(n_sub)):
        for h in range(hb):
            ys[(h, i)] = y_in[(h, i)] + _dot_nt(q_e[(h, i)], sts[h].astype(bf16))
            sts[h] = sts[h] * tot_v[(h, i)] + _dot_tn(vb[(h, i)], k_e[(h, i)])
    y = jnp.concatenate([jnp.concatenate([ys[(h, i)] for i in range(n_sub)], axis=0) for h in range(hb)], axis=1)
    return y, jnp.concatenate([a[None] for a in sts], axis=0)


def _dl_chunk(q, k, v, bq, laq, bk, lak, s, *, rev, hb, rr):
    qc = CHUNK
    kd = GDN_HEAD
    row, col = _eff(_iota((qc, qc), 0), qc, rev), _eff(_iota((qc, qc), 1), qc, rev)
    tril = (row >= col).astype(bf16)
    triu = (row <= col).astype(f32)
    ones = jnp.ones((qc, qc), bf16)
    ones_k = jnp.ones((kd, qc), bf16)
    eye = (row == col).astype(f32)
    heads = range(hb)
    lanes = [slice(j * kd, (j + 1) * kd) for j in heads]
    kf = [k[:, lanes[j // rr]] for j in heads]
    qb = [q[:, lanes[j // rr]].astype(bf16) for j in heads]
    kb = [a.astype(bf16) for a in kf]
    kk = [_dot_nt(kb[j], kb[j]) for j in heads]
    qk = [_dot_nt(qb[j], kb[j]) for j in heads]
    eseg = [jnp.exp(jnp.minimum(_dot_sel(tril, laq[j]) - _dot_sel(ones, laq[j] * triu), 0.0)) for j in heads]
    cum_k = [_dot_sel(tril, lak[:, lanes[j]]) for j in heads]
    tot = [_dot_sel(ones, lak[:, lanes[j]]) for j in heads]
    tot_k = [_dot_sel(ones_k, lak[:, lanes[j]]) for j in heads]
    ecum = [jnp.exp(a) for a in cum_k]
    mp = [-(bq[j] * kk[j] * jnp.where(row > col, eseg[j], 0.0)) for j in heads]
    tinv = [eye + a for a in mp]
    for _ in range(_log2(qc) - 1):
        mp = [_dot3(a, a) for a in mp]
        tinv = [t + _dot3(t, a) for t, a in zip(tinv, mp)]
    u = [_dot3(tinv[j], v[:, lanes[j]] * bk[:, lanes[j]]) for j in heads]
    w = [_dot3(tinv[j], kf[j] * bk[:, lanes[j]] * ecum[j]) for j in heads]
    sb = [s[j].astype(bf16) for j in heads]
    v_new = [u[j] - _dot(w[j].astype(bf16), sb[j]) for j in heads]
    ys = [_dot((qk[j] * jnp.where(row >= col, eseg[j], 0.0)).astype(bf16), v_new[j].astype(bf16)) + ecum[j] * _dot(qb[j], sb[j])
          for j in heads]
    ss = [jnp.exp(tot_k[j]) * s[j] + _dot_tn(kb[j], (v_new[j] * jnp.exp(tot[j] - cum_k[j])).astype(bf16)) for j in heads]
    return jnp.concatenate(ys, axis=1), jnp.concatenate([a[None] for a in ss], axis=0)


def _scan_call(chunk, ins, maps, blocks, state_shape, y_shape, y_block, y_map, units, nc, ncx, rev, name,
               dy=None, sprev=None):
    n = len(ins)
    params = pltpu.CompilerParams(dimension_semantics=("parallel", "arbitrary"), vmem_limit_bytes=VMEM_LIMIT)
    state_block = (None, None) + state_shape
    zeros = (0,) * len(state_shape)

    def chunk_at(pos):
        return jnp.where(pos < ncx, ncx - 1 - pos, nc + ncx - 1 - pos) if rev else pos

    def at(m, pos_of):
        return lambda u, c: m(u, chunk_at(pos_of(c)))

    if dy is None:
        def fwd_body(*refs):
            in_refs, y_ref, sp_ref, s_scr = refs[:n], refs[n], refs[n + 1], refs[n + 2]

            @pl.when(pl.program_id(1) == 0)
            def _():
                s_scr[...] = jnp.zeros_like(s_scr)

            s = s_scr[...]
            sp_ref[...] = s
            y, s_new = chunk(*[r[...] for r in in_refs], s)
            y_ref[...] = y
            s_scr[...] = s_new

        same = lambda c: c
        return pl.pallas_call(
            fwd_body, grid=(units, nc),
            in_specs=[pl.BlockSpec(b, at(m, same)) for b, m in zip(blocks, maps)],
            out_specs=[pl.BlockSpec(y_block, at(y_map, same)), pl.BlockSpec(state_block, lambda u, c: (u, c) + zeros)],
            out_shape=[jax.ShapeDtypeStruct(y_shape, f32), jax.ShapeDtypeStruct((units, nc) + state_shape, f32)],
            scratch_shapes=[pltpu.VMEM(state_shape, f32)],
            compiler_params=params, name=name,
        )(*ins)

    def bwd_body(*refs):
        in_refs, sp_ref, dy_ref = refs[:n], refs[n], refs[n + 1]
        out_refs, ds_scr = refs[n + 2:2 * n + 2], refs[2 * n + 2]

        @pl.when(pl.program_id(1) == 0)
        def _():
            ds_scr[...] = jnp.zeros_like(ds_scr)

        _, vjp = jax.vjp(chunk, *[r[...] for r in in_refs], sp_ref[...])
        grads = vjp((dy_ref[...], ds_scr[...]))
        for r, g in zip(out_refs, grads[:n]):
            r[...] = g
        ds_scr[...] = grads[n]

    back = lambda c: nc - 1 - c
    return pl.pallas_call(
        bwd_body, grid=(units, nc),
        in_specs=[pl.BlockSpec(b, at(m, back)) for b, m in zip(blocks, maps)]
        + [pl.BlockSpec(state_block, lambda u, c: (u, nc - 1 - c) + zeros), pl.BlockSpec(y_block, at(y_map, back))],
        out_specs=[pl.BlockSpec(b, at(m, back)) for b, m in zip(blocks, maps)],
        out_shape=[jax.ShapeDtypeStruct(a.shape, f32) for a in ins],
        scratch_shapes=[pltpu.VMEM(state_shape, f32)],
        compiler_params=params, name=name,
    )(*ins, sprev, dy)


def _scan_name(kind, rev, t, dy):
    return f"{kind}_{'bwd' if dy is not None else 'fwd'}_{'rev' if rev else 'fore'}_{t}"


def _sd_call(g, r, p, rev, ncx, ins, dy=None, sprev=None):
    q, _, v, _, _ = ins
    t = q.shape[0]
    n = q.shape[1] // g
    ub = SD_UNITS_PER_STEP
    im = lambda u, c: (c, u)
    imx = lambda u, c: (u, c, 0)
    bq, bv, bx = (CHUNK, ub * n), (CHUNK, ub * r * p), (ub, CHUNK, r * CHUNK)
    return _scan_call(functools.partial(_sd_chunk, r=r, p=p, rev=rev, ub=ub), list(ins), [im, im, im, imx, im],
                      [bq, bq, bv, bx, bv], (ub, n, r * p), v.shape, bv, im, g // ub, t // CHUNK, ncx, rev,
                      _scan_name(f"sd{n}x{r}x{p}", rev, t, dy), dy, sprev)


def _vd_call(rev, ncx, ins, dy=None, sprev=None):
    q = ins[0]
    t, h = q.shape[0], q.shape[1] // HGRN_EXPAND
    hb = HGRN_HEADS_PER_STEP
    im = lambda u, c: (c, u)
    blk = (CHUNK, hb * HGRN_EXPAND)
    return _scan_call(functools.partial(_vd_chunk, rev=rev, hb=hb), list(ins), [im] * 4, [blk] * 4,
                      (hb, HGRN_EXPAND, HGRN_EXPAND), q.shape, blk, im, h // hb, t // CHUNK, ncx, rev,
                      _scan_name("vd", rev, t, dy), dy, sprev)


def _dl_call(rev, ncx, ins, dy=None, sprev=None):
    q, _, v, _, _, _, _ = ins
    t = q.shape[0]
    hv = v.shape[1] // GDN_HEAD
    rr = hv // (q.shape[1] // GDN_HEAD)
    hb = GDN_HEADS_PER_STEP
    im = lambda u, c: (c, u)
    imx = lambda u, c: (u, c, 0)
    bqk, bv, bx = (CHUNK, hb // rr * GDN_HEAD), (CHUNK, hb * GDN_HEAD), (hb, CHUNK, CHUNK)
    return _scan_call(functools.partial(_dl_chunk, rev=rev, hb=hb, rr=rr), list(ins), [im, im, im, imx, imx, im, im],
                      [bqk, bqk, bv, bx, bx, bv, bv], (hb, GDN_HEAD, GDN_HEAD), v.shape, bv, im, hv // hb, t // CHUNK, ncx, rev,
                      _scan_name("dl", rev, t, dy), dy, sprev)


_SCAN_CALLS = {"ssd": functools.partial(_sd_call, SSD_GROUPS, SSD_HEADS // SSD_GROUPS, SSD_HEAD_DIM),
               "ret": functools.partial(_sd_call, RET_HEADS, 1, RET_V), "hgrn": _vd_call, "gdn": _dl_call}
_SCANS = {}


def _scan(kind, rev, lc):
    key = (kind, rev, lc)
    if key not in _SCANS:
        call = functools.partial(_SCAN_CALLS[kind], rev, lc // CHUNK)

        @jax.custom_vjp
        def scan(*ins):
            return call(ins)[0]

        def fwd(*ins):
            y, sprev = call(ins)
            return y, (ins, sprev)

        def bwd(res, dy):
            ins, sprev = res
            return tuple(call(ins, dy, sprev))

        scan.defvjp(fwd, bwd)
        _SCANS[key] = scan
    return _SCANS[key]


def _place():
    x, y, c = lax.axis_index("x"), lax.axis_index("y"), lax.axis_index("c")
    chips = [(1 - x, y), (x, 1 - y), (1 - x, 1 - y)]
    return x, y, c, chips


def _small_all_gather(block):
    m_per, n = block.shape

    def body(x_ref, out_ref, send_sems, recv_sems, local_sem):
        x, y, c, chips = _place()
        me, sibling = (x, y, c), (x, y, 1 - c)

        def rows(px, py, pc):
            return out_ref.at[pl.ds((4 * px + 2 * py + pc) * m_per, m_per), :]

        def copy(k, blk, to, src=None):
            return pltpu.make_async_remote_copy(
                src_ref=rows(*blk) if src is None else src, dst_ref=rows(*blk),
                send_sem=send_sems.at[k], recv_sem=recv_sems.at[k], device_id=to, device_id_type=MESH)

        mine = pltpu.make_async_copy(x_ref, rows(*me), local_sem)
        mine.start()
        first = [copy(0, me, sibling, src=x_ref)]
        first += [copy(1 + j, me, (*chip, c), src=x_ref) for j, chip in enumerate(chips)]
        for cp in first:
            cp.start()
        passed = [copy(4 + j, (*chip, c), sibling) for j, chip in enumerate(chips)]
        for j, chip in enumerate(chips):
            copy(1 + j, (*chip, c), me).wait_recv()
            passed[j].start()
        copy(0, sibling, me).wait_recv()
        for j, chip in enumerate(chips):
            copy(4 + j, (*chip, 1 - c), me).wait_recv()
        for cp in first + passed:
            cp.wait_send()
        mine.wait()

    return pl.pallas_call(
        body,
        out_shape=jax.ShapeDtypeStruct((N_DEV * m_per, n), block.dtype),
        in_specs=[pl.BlockSpec(memory_space=pltpu.VMEM)],
        out_specs=pl.BlockSpec(memory_space=pltpu.VMEM),
        scratch_shapes=[pltpu.SemaphoreType.DMA((7,)), pltpu.SemaphoreType.DMA((7,)), pltpu.SemaphoreType.DMA],
        compiler_params=pltpu.CompilerParams(vmem_limit_bytes=VMEM_LIMIT),
        name=f"small_all_gather_{m_per}x{n}",
    )(block)


def _any_spec():
    return pl.BlockSpec(memory_space=pl.ANY)


def _weight_all_gather(pack):
    rows, width = pack.shape
    half = rows // 2

    def body(w_ref, out_ref, send_sems, recv_sems):
        x, y, c, chips = _place()
        sibling = (x, y, 1 - c)

        def part(px, py, pc):
            return out_ref.at[2 * px + py, pl.ds(pl.multiple_of(pc * half, 16), half), :]

        def copy(k, blk, to, src=None):
            return pltpu.make_async_remote_copy(
                src_ref=part(*blk) if src is None else src, dst_ref=part(*blk),
                send_sem=send_sems.at[k], recv_sem=recv_sems.at[k], device_id=to, device_id_type=MESH)

        my_half = w_ref.at[pl.ds(pl.multiple_of(c * half, 16), half), :]
        first = [copy(j, (x, y, c), (*chip, c), src=my_half) for j, chip in enumerate(chips)]
        for cp in first:
            cp.start()
        passed = [copy(3 + j, (*chip, c), sibling) for j, chip in enumerate(chips)]
        for j, chip in enumerate(chips):
            copy(j, (*chip, c), (x, y, c)).wait_recv()
            passed[j].start()
        for j, chip in enumerate(chips):
            copy(3 + j, (*chip, 1 - c), (x, y, c)).wait_recv()
        for cp in first + passed:
            cp.wait_send()

    landed = pl.pallas_call(
        body,
        out_shape=jax.ShapeDtypeStruct((N_CHIPS, rows, width), pack.dtype),
        in_specs=[_any_spec()], out_specs=_any_spec(),
        scratch_shapes=[pltpu.SemaphoreType.DMA((6,)), pltpu.SemaphoreType.DMA((6,))],
        name="weight_all_gather",
    )(pack)
    return lax.dynamic_update_index_in_dim(landed, pack, 2 * lax.axis_index("x") + lax.axis_index("y"), 0)


def _sibling_swap(send):
    def body(s_ref, out_ref, send_sem, recv_sem):
        x, y, c, _ = _place()
        cp = pltpu.make_async_remote_copy(src_ref=s_ref, dst_ref=out_ref, send_sem=send_sem, recv_sem=recv_sem,
                                          device_id=(x, y, 1 - c), device_id_type=MESH)
        cp.start()
        cp.wait()

    return pl.pallas_call(
        body, out_shape=jax.ShapeDtypeStruct(send.shape, send.dtype),
        in_specs=[_any_spec()], out_specs=_any_spec(),
        scratch_shapes=[pltpu.SemaphoreType.DMA, pltpu.SemaphoreType.DMA],
        name=f"sibling_swap_{'x'.join(map(str, send.shape))}_{jnp.dtype(send.dtype).name}",
    )(send)


def _sibling_share(mine):
    def body(m_ref, out_ref, send_sem, recv_sem):
        x, y, c, _ = _place()
        sibling = (x, y, 1 - c)
        give = pltpu.make_async_remote_copy(src_ref=m_ref, dst_ref=out_ref.at[c], send_sem=send_sem, recv_sem=recv_sem,
                                            device_id=sibling, device_id_type=MESH)
        give.start()
        take = pltpu.make_async_remote_copy(src_ref=m_ref, dst_ref=out_ref.at[1 - c], send_sem=send_sem, recv_sem=recv_sem,
                                            device_id=sibling, device_id_type=MESH)
        take.wait_recv()
        give.wait_send()

    landed = pl.pallas_call(
        body, out_shape=jax.ShapeDtypeStruct((2,) + mine.shape, mine.dtype),
        in_specs=[_any_spec()], out_specs=_any_spec(),
        scratch_shapes=[pltpu.SemaphoreType.DMA, pltpu.SemaphoreType.DMA],
        name="sibling_share",
    )(mine)
    return lax.dynamic_update_index_in_dim(landed, mine, lax.axis_index("c"), 0)


def _chip_exchange(parts):
    _, rows, width = parts.shape

    def body(p_ref, out_ref, send_sems, recv_sems):
        x, y, c, chips = _place()
        copies = [pltpu.make_async_remote_copy(
            src_ref=p_ref.at[2 * px + py], dst_ref=out_ref.at[j], send_sem=send_sems.at[j], recv_sem=recv_sems.at[j],
            device_id=(px, py, c), device_id_type=MESH) for j, (px, py) in enumerate(chips)]
        for cp in copies:
            cp.start()
        for cp in copies:
            cp.wait()

    return pl.pallas_call(
        body, out_shape=jax.ShapeDtypeStruct((3, rows, width), parts.dtype),
        in_specs=[_any_spec()], out_specs=_any_spec(),
        scratch_shapes=[pltpu.SemaphoreType.DMA((3,)), pltpu.SemaphoreType.DMA((3,))],
        name="chip_exchange",
    )(parts)


def _row_tile(rows, width, n_arrays):
    budget = VMEM_LIMIT // (4 * 2 * n_arrays * width * 2)
    return _pick(rows, tuple(t for t in (1024, 512, 256, 128, 64, 32, 16, 8) if t <= max(budget, 8)))


def _add_n(arrays, out_dtype):
    rows, width = arrays[0].shape
    n = len(arrays)
    tr = _row_tile(rows, width, n + 1)

    def body(*refs):
        acc = refs[0][...].astype(f32)
        for r in refs[1:n]:
            acc = acc + r[...].astype(f32)
        refs[n][...] = acc.astype(out_dtype)

    spec = pl.BlockSpec((tr, width), lambda i: (i, 0))
    return pl.pallas_call(
        body, grid=(rows // tr,), in_specs=[spec] * n, out_specs=spec,
        out_shape=jax.ShapeDtypeStruct((rows, width), out_dtype),
        compiler_params=pltpu.CompilerParams(dimension_semantics=("parallel",), vmem_limit_bytes=VMEM_LIMIT),
        name=f"add{n}_{rows}x{width}_{jnp.dtype(out_dtype).name}",
    )(*arrays)


def _adamw(w, g, m, v):
    rows, width = w.shape
    tr = _row_tile(rows, width, 7)
    c1 = 1.0 / (1.0 - ADAM_B1 ** ADAM_STEP)
    c2 = 1.0 / (1.0 - ADAM_B2 ** ADAM_STEP)

    def body(w_ref, g_ref, m_ref, v_ref, d_ref, mo_ref, vo_ref):
        gg = g_ref[...]
        m_new = ADAM_B1 * m_ref[...] + (1.0 - ADAM_B1) * gg
        v_new = ADAM_B2 * v_ref[...] + (1.0 - ADAM_B2) * (gg * gg)
        d_ref[...] = -ADAM_LR * ((m_new * c1) / (jnp.sqrt(v_new * c2) + ADAM_EPS) + ADAM_WD * w_ref[...])
        mo_ref[...] = m_new
        vo_ref[...] = v_new

    spec = pl.BlockSpec((tr, width), lambda i: (i, 0))
    sds = jax.ShapeDtypeStruct((rows, width), f32)
    return pl.pallas_call(
        body, grid=(rows // tr,), in_specs=[spec] * 4, out_specs=[spec] * 3, out_shape=[sds] * 3,
        compiler_params=pltpu.CompilerParams(dimension_semantics=("parallel",), vmem_limit_bytes=VMEM_LIMIT),
        name=f"adamw_{rows}x{width}",
    )(w, g, m, v)


def _rmsnorm(x, g):
    return x * lax.rsqrt(jnp.mean(x * x, axis=-1, keepdims=True) + NORM_EPS) * g


def _conv(u, w, lc):
    t = u.shape[0]
    pos = jnp.arange(t)[:, None]
    zero = jnp.zeros((1, u.shape[1]), u.dtype)
    prev = jnp.where((pos == 0) | (pos == lc), 0.0, jnp.concatenate([zero, u[:-1]], axis=0))
    nxt = jnp.where((pos == lc - 1) | (pos == t - 1), 0.0, jnp.concatenate([u[1:], zero], axis=0))
    return w[0] * prev + w[1] * u + w[2] * nxt


def _lanes_q(a):
    return jnp.broadcast_to(a.T[:, :, None], (a.shape[1], a.shape[0], CHUNK))


def _ssd_mixer(h, lc, start, w_main, w_dt, conv_w, conv_b, dt_bias, a_log, d_skip, norm_g, w_out):
    t = h.shape[0]
    r = SSD_HEADS // SSD_GROUPS
    u = pmm(h, w_main)
    dt = pmm(h, w_dt)
    z, xbc = u[:, :SSD_D_INNER], u[:, SSD_D_INNER:]
    xbc = jax.nn.silu(_conv(xbc, conv_w, lc) + conv_b)
    xs = xbc[:, :SSD_D_INNER]
    bm = xbc[:, SSD_D_INNER:SSD_D_INNER + SSD_GROUPS * SSD_STATE]
    cm = xbc[:, SSD_D_INNER + SSD_GROUPS * SSD_STATE:]
    dt = jax.nn.softplus(dt.reshape(t, 2, SSD_HEADS) + dt_bias)
    log_a = -jnp.exp(a_log) * dt
    per_lane = lambda a: jnp.repeat(a, SSD_HEAD_DIM, axis=1)
    y = jnp.repeat(d_skip, SSD_HEAD_DIM) * xs
    for d in (0, 1):
        lap = per_lane(log_a[:, d])
        laq = lap.reshape(t, SSD_GROUPS, r * CHUNK).transpose(1, 0, 2)
        y = y + _scan("ssd", d == 1, lc)(cm, bm, xs * per_lane(dt[:, d]), laq, lap)
    y = y[start:] * jax.nn.silu(z[start:])
    n = t - start
    y = _rmsnorm(y.reshape(n, SSD_GROUPS, -1), norm_g.reshape(SSD_GROUPS, -1)).reshape(n, SSD_D_INNER)
    return pmm(y, w_out)


def _rope(a, rows):
    pos = jnp.arange(rows * GRID_W)
    row = (pos // GRID_W).astype(f32)
    col = (pos % GRID_W).astype(f32)
    half = a.shape[-1] // 2
    inv_freq = ROPE_BASE ** (-jnp.arange(0, half, 2, dtype=f32) / half)

    def rot(u, p):
        ang = p[:, None] * inv_freq
        cos, sin = jnp.cos(ang)[:, None, :], jnp.sin(ang)[:, None, :]
        u1, u2 = jnp.split(u, 2, axis=-1)
        return jnp.concatenate([u1 * cos - u2 * sin, u2 * cos + u1 * sin], axis=-1)

    return jnp.concatenate([rot(a[..., :half], row), rot(a[..., half:], col)], axis=-1)


def _ret_mixer(h, lc, start, w_in, log_decay, w_out):
    t = h.shape[0]
    u = pmm(h, w_in)
    q = u[:, :D_MODEL].reshape(t, RET_HEADS, RET_QK)
    k = u[:, D_MODEL:2 * D_MODEL].reshape(t, RET_HEADS, RET_QK) * RET_QK ** -0.5
    v = u[:, 2 * D_MODEL:2 * D_MODEL + RET_DV]
    gate = u[:, 2 * D_MODEL + RET_DV:]
    rows = (t - lc) // GRID_W
    q = jnp.concatenate([q[:lc], _rope(q[lc:], rows)], axis=0).reshape(t, D_MODEL)
    k = jnp.concatenate([k[:lc], _rope(k[lc:], rows)], axis=0).reshape(t, D_MODEL)
    y = 0.0
    for d in (0, 1):
        laq = jnp.broadcast_to(log_decay[d][:, None, None], (RET_HEADS, t, CHUNK))
        lap = jnp.broadcast_to(jnp.repeat(log_decay[d], RET_V)[None, :], (t, RET_DV))
        y = y + _scan("ret", d == 1, lc)(q, k, v, laq, lap)
    n = t - start
    y = y[start:].reshape(n, RET_HEADS, RET_V)
    mu = jnp.mean(y, axis=-1, keepdims=True)
    var = jnp.mean(jnp.square(y - mu), axis=-1, keepdims=True)
    y = ((y - mu) * lax.rsqrt(var + NORM_EPS)).reshape(n, RET_DV) * jax.nn.silu(gate[start:])
    return pmm(y, w_out)


def _hgrn_mixer(h, lc, start, w_in, lb, norm_g, w_out):
    t = h.shape[0]
    u = pmm(h, w_in)
    q, f_f, f_b, inp, gate = (u[:, i * D_MODEL:(i + 1) * D_MODEL] for i in range(5))

    def gates(f):
        log_f = jnp.logaddexp(jnp.log(lb), jnp.log1p(-lb) + jax.nn.log_sigmoid(f))
        return log_f, (1 - lb) * jax.nn.sigmoid(-f)

    lf_f, k_f = gates(f_f)
    lf_b, k_b = gates(f_b)
    y = _scan("hgrn", False, lc)(q, k_f, inp, lf_f) + _scan("hgrn", True, lc)(q, k_b, inp, lf_b)
    n = t - start
    y = _rmsnorm(y[start:].reshape(n, HGRN_HEADS, HGRN_EXPAND), norm_g.reshape(HGRN_HEADS, HGRN_EXPAND))
    y = y.reshape(n, D_MODEL) * jax.nn.silu(gate[start:])
    return pmm(y, w_out)


def _l2norm(a):
    return a * lax.rsqrt(jnp.sum(a * a, axis=-1, keepdims=True) + 1e-6)


def _gdn_mixer(h, lc, start, w_main, w_gate, conv_w, dt_bias, a_log, norm_g, w_out):
    t = h.shape[0]
    u = pmm(h, w_main)
    ba = pmm(h, w_gate)
    qkv = jax.nn.silu(_conv(u[:, :GDN_CONV_CH], conv_w, lc))
    z = u[:, GDN_CONV_CH:]
    q = (_l2norm(qkv[:, :GDN_DK].reshape(t, GDN_K_HEADS, GDN_HEAD)) * GDN_HEAD ** -0.5).reshape(t, GDN_DK)
    k = _l2norm(qkv[:, GDN_DK:2 * GDN_DK].reshape(t, GDN_K_HEADS, GDN_HEAD)).reshape(t, GDN_DK)
    v = qkv[:, 2 * GDN_DK:]
    beta = jax.nn.sigmoid(ba[:, :2 * GDN_V_HEADS].reshape(t, 2, GDN_V_HEADS))
    log_a = -jnp.exp(a_log) * jax.nn.softplus(ba[:, 2 * GDN_V_HEADS:].reshape(t, 2, GDN_V_HEADS) + dt_bias)
    per_lane = lambda a: jnp.repeat(a, GDN_HEAD, axis=1)
    y = 0.0
    for d in (0, 1):
        y = y + _scan("gdn", d == 1, lc)(q, k, v, _lanes_q(beta[:, d]), _lanes_q(log_a[:, d]),
                                         per_lane(beta[:, d]), per_lane(log_a[:, d]))
    n = t - start
    y = _rmsnorm(y[start:].reshape(n, GDN_V_HEADS, GDN_HEAD), norm_g) * jax.nn.silu(
        z[start:].reshape(n, GDN_V_HEADS, GDN_HEAD))
    return pmm(y.reshape(n, GDN_DV), w_out)


def _local_loss(x, mod, big, small, ctx, target):
    lc = ctx.shape[0]
    tok = jnp.concatenate([ctx, x], axis=0)
    t = tok.shape[0]
    is_ctx = (jnp.arange(t) < lc)[:, None]
    p = jax.nn.softmax(small["hgrn_lb_logits"], axis=0)
    for i in range(DEPTH):
        last = i == DEPTH - 1
        start = lc if last else 0
        sh1, sc1, g1, sh2, sc2, g2 = (jnp.where(is_ctx, m[1], m[0]) for m in jnp.split(mod[i], 6, axis=-1))
        h = _rmsnorm(tok, small["norm_g"][i, 0]) * (1 + sc1) + sh1
        if i == 0:
            y = _ssd_mixer(h, lc, start, big["ssd_w_main"], big["ssd_w_dt"], small["ssd_conv_w"][0], small["ssd_conv_b"][0],
                           small["ssd_dt_bias"][0], small["ssd_a_log"][0], small["ssd_d"][0], small["ssd_norm_g"][0],
                           big["ssd_w_out"])
        elif i == 1:
            y = _ret_mixer(h, lc, start, big["ret_w_in"], small["ret_log_decay"][0], big["ret_w_out"])
        elif i == 2:
            lb = jnp.cumsum(p, axis=0)[i] - p[0]
            y = _hgrn_mixer(h, lc, start, big["hgrn_w_in"], lb, small["hgrn_norm_g"][0], big["hgrn_w_out"])
        else:
            y = _gdn_mixer(h, lc, start, big["gdn_w_main"], big["gdn_w_gate"], small["gdn_conv_w"][0],
                           small["gdn_dt_bias"][0], small["gdn_a_log"][0], small["gdn_norm_g"][0], big["gdn_w_out"])
        if last:
            tok, sc2, sh2, g1, g2 = tok[lc:], sc2[lc:], sh2[lc:], g1[lc:], g2[lc:]
        tok = tok + g1 * y
        h = _rmsnorm(tok, small["norm_g"][i, 1]) * (1 + sc2) + sh2
        hid = jnp.square(jax.nn.relu(pmm(h, big["mlp_w1"][i])))
        tok = tok + g2 * pmm(hid, big["mlp_w2"][i])
    out = _rmsnorm(tok, small["final_g"])
    return 0.5 * jnp.sum(jnp.mean(jnp.square(out - target), axis=-1))


def _rows_of(a, width):
    return a.reshape(-1, width)


def _pad_rows(flat, width, mult=8):
    rows = -(-flat.shape[0] // width)
    rows = -(-rows // mult) * mult
    return jnp.pad(flat, (0, rows * width - flat.shape[0])).reshape(rows, width)


def _unpack_weights(gathered, shard_shapes):
    full, r0 = {}, 0
    for name in BIG:
        shp = shard_shapes[name]
        rows = math.prod(shp) // PACK_W
        blk = gathered[:, r0:r0 + rows].reshape((N_CHIPS,) + shp)
        r0 += rows
        if name in COL_SHARDED:
            full[name] = jnp.moveaxis(blk, 0, 2).reshape(shp[0], shp[1], N_CHIPS * shp[2])
        else:
            full[name] = jnp.moveaxis(blk, 0, 1).reshape(shp[0], N_CHIPS * shp[1], shp[2])
    return full


def _pack_grads(grads, shard_shapes):
    parts = []
    for name in BIG:
        shp = shard_shapes[name]
        for g in grads[name]:
            if name in COL_SHARDED:
                blk = jnp.moveaxis(g.reshape(shp[1], N_CHIPS, shp[2]), 1, 0)
            else:
                blk = g.reshape(N_CHIPS, shp[1], shp[2])
            parts.append(blk.reshape(N_CHIPS, -1, PACK_W))
    return jnp.concatenate(parts, axis=1)


def _silu_grad(a):
    s = jax.nn.sigmoid(a)
    return s * (1 + a * (1 - s))


def kernel(x, c, ctx, c_ctx, ada_w, ada_b, norm_g, mlp_w1, mlp_w2, final_g, ssd_w_in, ssd_conv_w, ssd_conv_b, ssd_dt_bias, ssd_a_log, ssd_d, ssd_norm_g, ssd_w_out, ret_w_in, ret_log_decay, ret_w_out, hgrn_w_in, hgrn_lb_logits, hgrn_norm_g, hgrn_w_out, gdn_w_in, gdn_conv_w, gdn_dt_bias, gdn_a_log, gdn_norm_g, gdn_w_out, loss_target, m_c_ctx, m_ada_w, m_ada_b, m_norm_g, m_mlp_w1, m_mlp_w2, m_final_g, m_ssd_w_in, m_ssd_conv_w, m_ssd_conv_b, m_ssd_dt_bias, m_ssd_a_log, m_ssd_d, m_ssd_norm_g, m_ssd_w_out, m_ret_w_in, m_ret_log_decay, m_ret_w_out, m_hgrn_w_in, m_hgrn_lb_logits, m_hgrn_norm_g, m_hgrn_w_out, m_gdn_w_in, m_gdn_conv_w, m_gdn_dt_bias, m_gdn_a_log, m_gdn_norm_g, m_gdn_w_out, v_c_ctx, v_ada_w, v_ada_b, v_norm_g, v_mlp_w1, v_mlp_w2, v_final_g, v_ssd_w_in, v_ssd_conv_w, v_ssd_conv_b, v_ssd_dt_bias, v_ssd_a_log, v_ssd_d, v_ssd_norm_g, v_ssd_w_out, v_ret_w_in, v_ret_log_decay, v_ret_w_out, v_hgrn_w_in, v_hgrn_lb_logits, v_hgrn_norm_g, v_hgrn_w_out, v_gdn_w_in, v_gdn_conv_w, v_gdn_dt_bias, v_gdn_a_log, v_gdn_norm_g, v_gdn_w_out):
    env = dict(locals())
    w_loc = {n: env[n] for n in WEIGHTS}
    m_loc = {n: env["m_" + n] for n in WEIGHTS}
    v_loc = {n: env["v_" + n] for n in WEIGHTS}
    chip = 2 * lax.axis_index("x") + lax.axis_index("y")
    core = lax.axis_index("c")
    dev = 2 * chip + core
    d = D_MODEL

    sharded_small = [n for n, ax in SMALL if ax is not None]
    flat1 = jnp.concatenate([c.reshape(-1)] + [w_loc[n].reshape(-1) for n in sharded_small])
    g1 = _small_all_gather(_pad_rows(flat1, SMALL_W)).reshape(N_DEV, -1)
    c_all = g1[:, :d]
    small, off = {n: w_loc[n] for n, ax in SMALL if ax is None}, d
    for n in sharded_small:
        ax = dict(SMALL)[n]
        size = w_loc[n].size
        pieces = [g1[2 * k, off:off + size].reshape(w_loc[n].shape) for k in range(N_CHIPS)]
        small[n] = jnp.concatenate(pieces, axis=ax)
        off += size

    n_sh = ada_w.shape[2]
    cond_in = jnp.concatenate([c_all, c_ctx[None]], axis=0)
    cond = jnp.pad(jax.nn.silu(cond_in), ((0, 16 - N_DEV - 1), (0, 0)))
    ada_b_sh = lax.dynamic_slice_in_dim(ada_b, chip * n_sh, n_sh, axis=1)
    mod_sh = jnp.stack([_mm(cond, ada_w[i], "nn") + ada_b_sh[i] for i in range(DEPTH)])
    g2 = _small_all_gather(mod_sh.reshape(-1, SMALL_W)).reshape(N_DEV, DEPTH, 16, n_sh)
    mod_all = jnp.concatenate([g2[2 * k] for k in range(N_CHIPS)], axis=-1)
    mod_loc = jnp.stack([lax.dynamic_index_in_dim(mod_all, dev, axis=1, keepdims=False), mod_all[:, N_DEV]], axis=1)

    shard_shapes = {n: w_loc[n].shape for n in BIG}
    pack = jnp.concatenate([_rows_of(w_loc[n].astype(bf16), PACK_W) for n in BIG], axis=0)
    full = _unpack_weights(_weight_all_gather(pack), shard_shapes)
    big = {"mlp_w1": tuple(full["mlp_w1"][i] for i in range(DEPTH)),
           "mlp_w2": tuple(full["mlp_w2"][i] for i in range(DEPTH)),
           "ssd_w_main": full["ssd_w_in"][0, :, :SSD_MAIN], "ssd_w_dt": full["ssd_w_in"][0, :, SSD_MAIN:],
           "ssd_w_out": full["ssd_w_out"][0], "ret_w_in": full["ret_w_in"][0], "ret_w_out": full["ret_w_out"][0],
           "hgrn_w_in": full["hgrn_w_in"][0], "hgrn_w_out": full["hgrn_w_out"][0],
           "gdn_w_main": full["gdn_w_in"][0, :, :GDN_MAIN], "gdn_w_gate": full["gdn_w_in"][0, :, GDN_MAIN:],
           "gdn_w_out": full["gdn_w_out"][0]}

    small_diff = {n: small[n] for n, _ in SMALL if n not in ("c_ctx", "ada_b")}
    loss_loc, (gx, gmod, gbig, gsmall) = jax.value_and_grad(_local_loss, argnums=(0, 1, 2, 3))(
        x[0], mod_loc, big, small_diff, ctx[0], loss_target[0])
    loss = lax.psum(loss_loc, ("x", "y", "c"))

    small_names = [n for n, _ in SMALL if n not in ("c_ctx", "ada_b")]
    flat3 = jnp.concatenate([gmod.reshape(-1)] + [gsmall[n].reshape(-1) for n in small_names])
    g3 = _small_all_gather(_pad_rows(flat3, SMALL_W)).reshape(N_DEV, -1)
    gmod_all = g3[:, :gmod.size].reshape(N_DEV, DEPTH, 2, 6 * d)
    ctx_row = gmod_all[0, :, 1]
    for b in range(1, N_DEV):
        ctx_row = ctx_row + gmod_all[b, :, 1]
    dmod = jnp.concatenate([jnp.moveaxis(gmod_all[:, :, 0], 0, 1), ctx_row[:, None],
                            jnp.zeros((DEPTH, 16 - N_DEV - 1, 6 * d), f32)], axis=1)
    grad_small, off = {}, gmod.size
    for n in small_names:
        size = small[n].size
        tot = g3[0, off:off + size]
        for b in range(1, N_DEV):
            tot = tot + g3[b, off:off + size]
        grad_small[n] = tot.reshape(small[n].shape)
        off += size
    grad_small["ada_b"] = jnp.sum(dmod, axis=1)
    dmod_sh = lax.dynamic_slice_in_dim(dmod, chip * n_sh, n_sh, axis=2)
    grad_ada_w = jnp.stack([_mm(cond, dmod_sh[i], "tn") for i in range(DEPTH)])
    dcond = _mm(dmod_sh[0], ada_w[0], "nt")
    for i in range(1, DEPTH):
        dcond = dcond + _mm(dmod_sh[i], ada_w[i], "nt")
    g4 = _small_all_gather(_pad_rows(dcond[N_DEV], SMALL_W)).reshape(N_DEV, -1)[:, :d]
    dcond_ctx = g4[0] + g4[2] + g4[4] + g4[6]
    grad_small["c_ctx"] = dcond_ctx * _silu_grad(c_ctx)
    for n, ax in SMALL:
        if ax is not None:
            width = w_loc[n].shape[ax]
            grad_small[n] = lax.dynamic_slice_in_dim(grad_small[n], chip * width, width, axis=ax)

    gfull = {"mlp_w1": gbig["mlp_w1"], "mlp_w2": gbig["mlp_w2"],
             "ssd_w_in": [jnp.concatenate([gbig["ssd_w_main"], gbig["ssd_w_dt"]], axis=1)],
             "ssd_w_out": [gbig["ssd_w_out"]], "ret_w_in": [gbig["ret_w_in"]], "ret_w_out": [gbig["ret_w_out"]],
             "hgrn_w_in": [gbig["hgrn_w_in"]], "hgrn_w_out": [gbig["hgrn_w_out"]],
             "gdn_w_in": [jnp.concatenate([gbig["gdn_w_main"], gbig["gdn_w_gate"]], axis=1)],
             "gdn_w_out": [gbig["gdn_w_out"]]}
    gp = _pack_grads(gfull, shard_shapes)
    rows = gp.shape[1]
    half = rows // 2
    gp = gp.reshape(N_CHIPS, 2, half, PACK_W)
    mine = lax.dynamic_index_in_dim(gp, core, axis=1, keepdims=False)
    theirs = lax.dynamic_index_in_dim(gp, 1 - core, axis=1, keepdims=False)
    got = _sibling_swap(theirs)
    pair = _add_n([mine.reshape(-1, PACK_W), got.reshape(-1, PACK_W)], bf16).reshape(N_CHIPS, half, PACK_W)
    landed = _chip_exchange(pair)
    own = lax.dynamic_index_in_dim(pair, chip, axis=0, keepdims=False)
    red_half = _add_n([own, landed[0], landed[1], landed[2]], f32)
    red = _sibling_share(red_half).reshape(rows, PACK_W)
    grad_big, r0 = {}, 0
    for n in BIG:
        nrows = w_loc[n].size // PACK_W
        grad_big[n] = red[r0:r0 + nrows].reshape(w_loc[n].shape)
        r0 += nrows

    grads = dict(grad_big)
    grads.update(grad_small)
    grads["ada_w"] = grad_ada_w
    delta, new_m, new_v = {}, {}, {}
    for n in BIG + ("ada_w",):
        width = w_loc[n].shape[-1]
        outs = _adamw(*[a.reshape(-1, width) for a in (w_loc[n], grads[n], m_loc[n], v_loc[n])])
        delta[n], new_m[n], new_v[n] = (o.reshape(w_loc[n].shape) for o in outs)
    names = [n for n, _ in SMALL]
    packs = [_pad_rows(jnp.concatenate([t[n].reshape(-1) for n in names]), SMALL_W) for t in (w_loc, grads, m_loc, v_loc)]
    outs = _adamw(*packs)
    off = 0
    for n in names:
        size = w_loc[n].size
        delta[n], new_m[n], new_v[n] = (o.reshape(-1)[off:off + size].reshape(w_loc[n].shape) for o in outs)
        off += size

    return (loss, gx[None], *[grads[n] for n in WEIGHTS], *[delta[n] for n in WEIGHTS],
            *[new_m[n] for n in WEIGHTS], *[new_v[n] for n in WEIGHTS])
```
